```python
import jax, jax.numpy as jnp
from jax import lax
import numpy as np

D_MODEL = 4096
BATCH = 2
SEQ = 4096
DEPTH = 1
DEC_BATCH = 32
DEC_SEQ = 1
PAST_LEN = 8192
PAGE_SIZE = 128

HEAD_DIM = 128
D_MIX = D_MODEL
NSA_HEADS = D_MIX // 2 // HEAD_DIM
NSA_KV_HEADS = 4
NSA_GROUP = NSA_HEADS // NSA_KV_HEADS
RET_HEADS = D_MIX // 2 // HEAD_DIM
RET_DK = HEAD_DIM
RET_DV = HEAD_DIM
CMP_BLOCK = 32
CMP_STRIDE = 16
CMP_HIDDEN = 2 * HEAD_DIM
SLC_BLOCK = 64
SLC_TOPK = 16
WINDOW = 512
WIN_QBLOCK = 128
SLC_QBLOCK = 64
RET_CHUNK = 128
D_FF = 11008
EPS = 1e-6
BIG = 1e30
D_IN_PROJ = NSA_HEADS * HEAD_DIM + 6 * NSA_KV_HEADS * HEAD_DIM + 3 * NSA_HEADS + RET_HEADS * (2 * RET_DK + 2 * RET_DV)

kernel_name = "hymba_nsa_retnet_macaron_step"


def _split_points():
    sizes = [NSA_HEADS * HEAD_DIM, 6 * NSA_KV_HEADS * HEAD_DIM, 3 * NSA_HEADS,
             RET_HEADS * RET_DK, RET_HEADS * RET_DK, RET_HEADS * RET_DV]
    return [int(c) for c in np.cumsum(sizes)]


def _rmsnorm(x, g):
    xf = x.astype(jnp.float32)
    y = xf * lax.rsqrt(jnp.mean(xf * xf, axis=-1, keepdims=True) + EPS)
    return (y * g.astype(jnp.float32)).astype(x.dtype)


def _swiglu(x, w1, w3, w2):
    return (jax.nn.silu(x @ w1) * (x @ w3)) @ w2


def _alibi_slopes():
    i = jnp.arange(NSA_HEADS, dtype=jnp.float32)
    return jnp.exp2(-8.0 * (i + 1.0) / NSA_HEADS).reshape(NSA_KV_HEADS, NSA_GROUP)


def _masked_softmax(s, mask):
    s = jnp.where(mask, s, -jnp.inf)
    m = jnp.max(s, axis=-1, keepdims=True)
    m = jnp.where(jnp.isfinite(m), m, 0.0)
    e = jnp.where(mask, jnp.exp(s - m), 0.0)
    return e / jnp.maximum(jnp.sum(e, axis=-1, keepdims=True), 1e-30)


def _compress(rows, pe, w1, b1, w2, b2):
    b, lk, hk, d = rows.shape
    chunks = rows.reshape(b, lk // CMP_STRIDE, CMP_STRIDE, hk, d)
    first = jnp.einsum('bncgd,cde->bnge', chunks + pe[:CMP_STRIDE][None, None, :, None, :], w1[:CMP_STRIDE])
    second = jnp.einsum('bncgd,cde->bnge', chunks + pe[CMP_STRIDE:][None, None, :, None, :], w1[CMP_STRIDE:])
    hid = jax.nn.gelu(first[:, :-1] + second[:, 1:] + b1)
    return hid @ w2 + b2


def _compressed_branch(q, k_raw, v_raw, q_pos, kn, cmp, slopes):
    pe, w1, b1, w2, b2 = cmp
    kc = _rmsnorm(_compress(k_raw, pe[0], w1[0], b1[0], w2[0], b2[0]), kn)
    vc = _compress(v_raw, pe[1], w1[1], b1[1], w2[1], b2[1])
    n_cmp = kc.shape[1]
    blk_end = jnp.arange(n_cmp, dtype=jnp.int32) * CMP_STRIDE + (CMP_BLOCK - 1)
    dist = q_pos[:, None] - blk_end[None, :]
    s = jnp.einsum('bqhgd,bnhd->bhgqn', q, kc).astype(jnp.float32)
    s = s - slopes[None, :, :, None, None] * dist.astype(jnp.float32)
    p = _masked_softmax(s, dist >= 0)
    o = jnp.einsum('bhgqn,bnhd->bqhgd', p.astype(vc.dtype), vc)
    return o, p


def _select_blocks(p_cmp, q_pos, n_slc):
    imp = jnp.sum(p_cmp, axis=2)
    n_cmp = imp.shape[-1]
    cs = jnp.arange(n_cmp, dtype=jnp.int32)[:, None] * CMP_STRIDE
    ss = jnp.arange(n_slc, dtype=jnp.int32)[None, :] * SLC_BLOCK
    overlap = jnp.minimum(cs + CMP_BLOCK, ss + SLC_BLOCK) - jnp.maximum(cs, ss)
    m = jnp.maximum(overlap, 0).astype(jnp.float32) / CMP_BLOCK
    imp = jnp.einsum('bhqn,nj->bhqj', imp, m)
    j = jnp.arange(n_slc, dtype=jnp.int32)[None, :]
    cur = (q_pos // SLC_BLOCK)[:, None]
    forced = (j == 0) | (j == cur) | (j == cur - 1)
    score = jnp.where(j <= cur, jnp.where(forced, BIG, imp), -BIG)
    _, idx = lax.top_k(score, min(SLC_TOPK, n_slc))
    return idx


def _selected_branch(q, k_raw, v_raw, idx, q_pos, kn, slopes):
    b, lk, hk, d = k_raw.shape
    lq = q.shape[1]
    n_slc = lk // SLC_BLOCK
    kb = _rmsnorm(k_raw, kn).reshape(b, n_slc, SLC_BLOCK, hk, d).transpose(0, 3, 1, 2, 4)
    vb = v_raw.reshape(b, n_slc, SLC_BLOCK, hk, d).transpose(0, 3, 1, 2, 4)
    qb = SLC_QBLOCK if lq % SLC_QBLOCK == 0 else lq
    nq = lq // qb
    topk = idx.shape[-1]
    q_blk = q.reshape(b, nq, qb, hk, NSA_GROUP, d).transpose(1, 0, 2, 3, 4, 5)
    idx_blk = idx.reshape(b, hk, nq, qb, topk).transpose(2, 0, 1, 3, 4)
    pos_blk = q_pos.reshape(nq, qb)
    bi = jnp.arange(b)[:, None, None, None]
    hi = jnp.arange(hk)[None, :, None, None]
    offs = jnp.arange(SLC_BLOCK, dtype=jnp.int32)

    def one_block(args):
        qq, ii, pp = args
        kg = kb[bi, hi, ii].reshape(b, hk, qb, topk * SLC_BLOCK, d)
        vg = vb[bi, hi, ii].reshape(b, hk, qb, topk * SLC_BLOCK, d)
        kpos = (ii[..., None] * SLC_BLOCK + offs).reshape(b, hk, qb, topk * SLC_BLOCK)
        dist = pp[None, None, :, None] - kpos
        s = jnp.einsum('bqhgd,bhqkd->bhgqk', qq, kg).astype(jnp.float32)
        s = s - slopes[None, :, :, None, None] * dist[:, :, None].astype(jnp.float32)
        p = _masked_softmax(s, (dist >= 0)[:, :, None])
        return jnp.einsum('bhgqk,bhqkd->bqhgd', p.astype(vg.dtype), vg)

    o = lax.map(one_block, (q_blk, idx_blk, pos_blk))
    return o.transpose(1, 0, 2, 3, 4, 5).reshape(b, lq, hk, NSA_GROUP, d)


def _window_attend(q, k, v, q_pos, k_pos, slopes):
    dist = q_pos[:, :, None] - k_pos[:, None, :]
    mask = (dist >= 0) & (dist < WINDOW) & (k_pos >= 0)[:, None, :]
    s = jnp.einsum('bnqhgd,bnkhd->bnhgqk', q, k).astype(jnp.float32)
    s = s - slopes[None, None, :, :, None, None] * dist[None, :, None, None].astype(jnp.float32)
    p = _masked_softmax(s, mask[None, :, None, None])
    return jnp.einsum('bnhgqk,bnkhd->bnqhgd', p.astype(v.dtype), v)


def _window_prompt(q, k, v, slopes):
    b, l, hk, d = k.shape
    nb = l // WIN_QBLOCK
    kp = jnp.pad(k, ((0, 0), (WINDOW, 0), (0, 0), (0, 0)))
    vp = jnp.pad(v, ((0, 0), (WINDOW, 0), (0, 0), (0, 0)))
    gidx = jnp.arange(nb, dtype=jnp.int32)[:, None] * WIN_QBLOCK + jnp.arange(WIN_QBLOCK + WINDOW, dtype=jnp.int32)[None, :]
    q_pos = jnp.arange(l, dtype=jnp.int32).reshape(nb, WIN_QBLOCK)
    qb = q.reshape(b, nb, WIN_QBLOCK, hk, NSA_GROUP, d)
    o = _window_attend(qb, kp[:, gidx], vp[:, gidx], q_pos, gidx - WINDOW, slopes)
    return o.reshape(b, l, hk, NSA_GROUP, d)


def _retention(q, k, v, s0, chunk):
    b, l, h, dk = q.shape
    dv = v.shape[-1]
    n = l // chunk
    f32 = jnp.float32
    log_g = jnp.log1p(-jnp.exp2(-5.0 - jnp.arange(h, dtype=f32)))
    i = jnp.arange(chunk, dtype=f32)
    diff = i[:, None] - i[None, :]
    intra = jnp.where(diff >= 0, jnp.exp(jnp.maximum(diff, 0.0)[None] * log_g[:, None, None]), 0.0)
    q_dec = jnp.exp((i[:, None] + 1.0) * log_g[None, :])
    k_dec = jnp.exp((chunk - 1.0 - i)[:, None] * log_g[None, :])
    c_dec = jnp.exp(chunk * log_g)
    qs = q.astype(f32).reshape(b, n, chunk, h, dk).transpose(1, 0, 2, 3, 4)
    ks = (k.astype(f32) * (dk ** -0.5)).reshape(b, n, chunk, h, dk).transpose(1, 0, 2, 3, 4)
    vs = v.astype(f32).reshape(b, n, chunk, h, dv).transpose(1, 0, 2, 3, 4)

    def step(state, inp):
        qc, kc, vc = inp
        att = jnp.einsum('bqhd,bkhd->bhqk', qc, kc) * intra
        o = jnp.einsum('bhqk,bkhe->bqhe', att, vc) + jnp.einsum('bqhd,bhde->bqhe', qc * q_dec[None, :, :, None], state)
        state = state * c_dec[None, :, None, None] + jnp.einsum('bkhd,bkhe->bhde', kc * k_dec[None, :, :, None], vc)
        return state, o

    s_fin, o = lax.scan(step, s0.astype(f32), (qs, ks, vs))
    return o.transpose(1, 0, 2, 3, 4).reshape(b, l, h, dv), s_fin


def _group_norm(o, g):
    b, l, h, dv = o.shape
    of = o.astype(jnp.float32)
    mu = jnp.mean(of, axis=-1, keepdims=True)
    var = jnp.mean(jnp.square(of - mu), axis=-1, keepdims=True)
    return ((of - mu) * lax.rsqrt(var + EPS)).reshape(b, l, h * dv) * g.astype(jnp.float32)


def _token_mixing(n, w_in, nsa_q_norm, nsa_k_norm, cmp_params, ret_gn_g, past_kv, past_win, ret_state):
    b, l, _ = n.shape
    q, kv, gates, rq, rk, rv, rg = jnp.split(n @ w_in, _split_points(), axis=-1)
    pos0 = 0 if past_kv is None else past_kv.shape[1]
    q_pos = pos0 + jnp.arange(l, dtype=jnp.int32)
    slopes = _alibi_slopes()
    q = (_rmsnorm(q.reshape(b, l, NSA_HEADS, HEAD_DIM), nsa_q_norm) * (HEAD_DIM ** -0.5)).reshape(b, l, NSA_KV_HEADS, NSA_GROUP, HEAD_DIM)
    kv = kv.reshape(b, l, 6, NSA_KV_HEADS, HEAD_DIM)
    kv_new = kv[:, :, :4]
    win_new = kv[:, :, 4:]
    parts = [kv_new] if past_kv is None else [past_kv.astype(kv.dtype), kv_new]
    lk = pos0 + l
    pad = (-lk) % SLC_BLOCK
    if pad:
        parts = parts + [jnp.zeros((b, pad) + kv_new.shape[2:], kv.dtype)]
    full = parts[0] if len(parts) == 1 else jnp.concatenate(parts, axis=1)
    o_cmp, p_cmp = _compressed_branch(q, full[:, :, 0], full[:, :, 1], q_pos, nsa_k_norm[0], cmp_params, slopes)
    idx = _select_blocks(p_cmp, q_pos, full.shape[1] // SLC_BLOCK)
    o_slc = _selected_branch(q, full[:, :, 2], full[:, :, 3], idx, q_pos, nsa_k_norm[1], slopes)
    if past_win is None:
        o_win = _window_prompt(q, _rmsnorm(win_new[:, :, 0], nsa_k_norm[2]), win_new[:, :, 1], slopes)
        new_win = win_new[:, -min(WINDOW, l):]
    else:
        wb = past_win.shape[1]
        wall = jnp.concatenate([past_win.astype(kv.dtype), win_new], axis=1)
        k_pos = (pos0 - wb + jnp.arange(wb + l, dtype=jnp.int32))[None]
        o_win = _window_attend(q[:, None], _rmsnorm(wall[:, :, 0], nsa_k_norm[2])[:, None], wall[:, :, 1][:, None],
                               q_pos[None], k_pos, slopes)[:, 0]
        new_win = wall[:, -wb:]
    g = jax.nn.sigmoid(gates.astype(jnp.float32)).reshape(b, l, 3, NSA_KV_HEADS, NSA_GROUP, 1)
    o_nsa = (g[:, :, 0] * o_cmp + g[:, :, 1] * o_slc + g[:, :, 2] * o_win).astype(n.dtype).reshape(b, l, NSA_HEADS * HEAD_DIM)
    s0 = jnp.zeros((b, RET_HEADS, RET_DK, RET_DV), jnp.float32) if ret_state is None else ret_state
    chunk = RET_CHUNK if l % RET_CHUNK == 0 else l
    o_ret, s_fin = _retention(rq.reshape(b, l, RET_HEADS, RET_DK), rk.reshape(b, l, RET_HEADS, RET_DK),
                              rv.reshape(b, l, RET_HEADS, RET_DV), s0, chunk)
    o_ret = (jax.nn.silu(rg.astype(jnp.float32)) * _group_norm(o_ret, ret_gn_g)).astype(n.dtype)
    return jnp.concatenate([o_nsa, o_ret], axis=-1), kv_new, new_win, s_fin.astype(n.dtype)


def _layer(x, norm_g, ffn_w1, ffn_w3, ffn_w2, w_in, nsa_q_norm, nsa_k_norm, cmp_params, ret_gn_g, w_out,
           past_kv, past_win, ret_state):
    h = x + 0.5 * _swiglu(_rmsnorm(x, norm_g[0]), ffn_w1[0], ffn_w3[0], ffn_w2[0])
    mixed, kv_new, new_win, s_fin = _token_mixing(_rmsnorm(h, norm_g[1]), w_in, nsa_q_norm, nsa_k_norm,
                                                  cmp_params, ret_gn_g, past_kv, past_win, ret_state)
    h = h + mixed @ w_out
    y = h + 0.5 * _swiglu(_rmsnorm(h, norm_g[2]), ffn_w1[1], ffn_w3[1], ffn_w2[1])
    return y, kv_new, new_win, s_fin


def setup_inputs(seed: int = 0) -> dict:
    key = jax.random.key(seed)
    ks = jax.random.split(key, 20)
    f32 = jnp.float32
    n_pages = PAST_LEN // PAGE_SIZE
    n_used = DEC_BATCH * n_pages
    n_pool = n_used + (n_used + 3) // 4
    wb = min(WINDOW, PAST_LEN)

    def nrm(k, shape, scale=1.0):
        return jax.random.normal(k, shape, f32) * scale

    return {
        "x_prompt": nrm(ks[0], (BATCH, SEQ, D_MODEL)),
        "x_sample": nrm(ks[1], (DEC_BATCH, DEC_SEQ, D_MODEL)),
        "cache_nsa_kv": nrm(ks[2], (DEPTH, n_pool, PAGE_SIZE, 4, NSA_KV_HEADS, HEAD_DIM)),
        "state_win_kv": nrm(ks[3], (DEPTH, DEC_BATCH, wb, 2, NSA_KV_HEADS, HEAD_DIM)),
        "state_ret": nrm(ks[4], (DEPTH, DEC_BATCH, RET_HEADS, RET_DK, RET_DV), 0.3),
        "page_table": jax.random.permutation(ks[5], n_pool)[:n_used].reshape(DEC_BATCH, n_pages).astype(jnp.int32),
        "norm_g": 1.0 + nrm(ks[6], (DEPTH, 3, D_MODEL), 0.02),
        "ffn_w1": nrm(ks[7], (DEPTH, 2, D_MODEL, D_FF), D_MODEL ** -0.5),
        "ffn_w3": nrm(ks[8], (DEPTH, 2, D_MODEL, D_FF), D_MODEL ** -0.5),
        "ffn_w2": nrm(ks[9], (DEPTH, 2, D_FF, D_MODEL), D_FF ** -0.5),
        "w_in": nrm(ks[10], (DEPTH, D_MODEL, D_IN_PROJ), D_MODEL ** -0.5),
        "nsa_q_norm": 1.0 + nrm(ks[11], (DEPTH, HEAD_DIM), 0.02),
        "nsa_k_norm": 1.0 + nrm(ks[12], (DEPTH, 3, HEAD_DIM), 0.02),
        "cmp_pe": nrm(ks[13], (DEPTH, 2, CMP_BLOCK, HEAD_DIM), 0.1),
        "cmp_w1": nrm(ks[14], (DEPTH, 2, CMP_BLOCK, HEAD_DIM, CMP_HIDDEN), (CMP_BLOCK * HEAD_DIM) ** -0.5),
        "cmp_b1": nrm(ks[15], (DEPTH, 2, CMP_HIDDEN), 0.02),
        "cmp_w2": nrm(ks[16], (DEPTH, 2, CMP_HIDDEN, HEAD_DIM), CMP_HIDDEN ** -0.5),
        "cmp_b2": nrm(ks[17], (DEPTH, 2, HEAD_DIM), 0.02),
        "ret_gn_g": 1.0 + nrm(ks[18], (DEPTH, RET_HEADS * RET_DV), 0.02),
        "w_out": nrm(ks[19], (DEPTH, D_MIX, D_MODEL), D_MIX ** -0.5),
    }


def reference(x_prompt, x_sample, cache_nsa_kv, state_win_kv, state_ret, page_table, norm_g, ffn_w1, ffn_w3,
              ffn_w2, w_in, nsa_q_norm, nsa_k_norm, cmp_pe, cmp_w1, cmp_b1, cmp_w2, cmp_b2, ret_gn_g, w_out):
    dec_b, n_pages = page_table.shape
    past_len = n_pages * cache_nsa_kv.shape[2]
    yp, ys = x_prompt, x_sample
    kvp_l, kvs_l, winp_l, wins_l, retp_l, rets_l = [], [], [], [], [], []
    for l in range(DEPTH):
        cmp_params = (cmp_pe[l], cmp_w1[l], cmp_b1[l], cmp_w2[l], cmp_b2[l])
        past = cache_nsa_kv[l][page_table].reshape(dec_b, past_len, 4, NSA_KV_HEADS, HEAD_DIM)
        yp, kvp, winp, retp = _layer(yp, norm_g[l], ffn_w1[l], ffn_w3[l], ffn_w2[l], w_in[l], nsa_q_norm[l],
                                     nsa_k_norm[l], cmp_params, ret_gn_g[l], w_out[l], None, None, None)
        ys, kvs, wins, rets = _layer(ys, norm_g[l], ffn_w1[l], ffn_w3[l], ffn_w2[l], w_in[l], nsa_q_norm[l],
                                     nsa_k_norm[l], cmp_params, ret_gn_g[l], w_out[l], past, state_win_kv[l],
                                     state_ret[l])
        kvp_l.append(kvp); kvs_l.append(kvs); winp_l.append(winp); wins_l.append(wins)
        retp_l.append(retp); rets_l.append(rets)
    return (yp, ys, jnp.stack(kvp_l), jnp.stack(kvs_l), jnp.stack(winp_l), jnp.stack(wins_l),
            jnp.stack(retp_l), jnp.stack(rets_l))
```

```python
import functools
import math

import numpy as np
import jax
import jax.numpy as jnp
from jax import lax
from jax.experimental import pallas as pl
from jax.experimental.pallas import tpu as pltpu

F32 = jnp.float32
BF16 = jnp.bfloat16

HEAD_DIM = 128
NSA_HEADS = 16
NSA_KV_HEADS = 4
NSA_GROUP = NSA_HEADS // NSA_KV_HEADS
RET_HEADS = 16
CMP_BLOCK = 32
CMP_STRIDE = 16
CMP_HIDDEN = 2 * HEAD_DIM
SLC_BLOCK = 64
SLC_TOPK = 16
WINDOW = 512
RET_CHUNK = 128
EPS = 1e-6
BIG = 1e30
NEG = -1e30
LANE = 128
SUBLANE = 8
VMEM_LIMIT = 56 * 1024 * 1024

GQ_W = NSA_GROUP * HEAD_DIM


def _cparams(sem):
    return pltpu.CompilerParams(dimension_semantics=sem, vmem_limit_bytes=VMEM_LIMIT)


def _tile(n, pref):
    if n <= pref:
        return n
    t = pref
    while n % t:
        t //= 2
    return t


def _dot(a, b):
    return jnp.dot(a, b, preferred_element_type=F32)


def _dot_nt(a, b):
    return lax.dot_general(a, b, (((1,), (1,)), ((), ())), preferred_element_type=F32)


def _gelu_tanh(x):
    return 0.5 * x * (1.0 + jnp.tanh(math.sqrt(2.0 / math.pi) * (x + 0.044715 * (x * x * x))))


def _silu(x):
    return x * (1.0 / (1.0 + jnp.exp(-x)))


def _sigmoid(x):
    return 1.0 / (1.0 + jnp.exp(-x))


def _row_rms(x, g):
    return x * lax.rsqrt(jnp.mean(x * x, axis=-1, keepdims=True) + EPS) * g


def _rmsnorm_kernel(x_ref, g_ref, o_ref):
    o_ref[...] = _row_rms(x_ref[...], g_ref[...]).astype(o_ref.dtype)


def rmsnorm_bf16(x, g):
    t, d = x.shape
    tr = _tile(t, 256)
    return pl.pallas_call(
        _rmsnorm_kernel,
        grid=(t // tr,),
        in_specs=[pl.BlockSpec((tr, d), lambda i: (i, 0)), pl.BlockSpec((1, d), lambda i: (0, 0))],
        out_specs=pl.BlockSpec((tr, d), lambda i: (i, 0)),
        out_shape=jax.ShapeDtypeStruct((t, d), BF16),
        compiler_params=_cparams(("parallel",)),
        name="rmsnorm",
    )(x, g.reshape(1, d))


def _mm_kernel(a_ref, w_ref, o_ref):
    o_ref[...] = _dot(a_ref[...], w_ref[...]).astype(o_ref.dtype)


def _mm_res_kernel(a_ref, w_ref, r_ref, o_ref):
    o_ref[...] = (r_ref[...] + _dot(a_ref[...], w_ref[...])).astype(o_ref.dtype)


def matmul(a, w, res=None, out_dtype=F32):
    m, k = a.shape
    n = w.shape[1]
    tm = _tile(m, 1024)
    tn = _tile(n, 512)
    in_specs = [pl.BlockSpec((tm, k), lambda i, j: (i, 0)), pl.BlockSpec((k, tn), lambda i, j: (0, j))]
    args = [a, w]
    body = _mm_kernel
    if res is not None:
        in_specs.append(pl.BlockSpec((tm, tn), lambda i, j: (i, j)))
        args.append(res)
        body = _mm_res_kernel
    return pl.pallas_call(
        body,
        grid=(m // tm, n // tn),
        in_specs=in_specs,
        out_specs=pl.BlockSpec((tm, tn), lambda i, j: (i, j)),
        out_shape=jax.ShapeDtypeStruct((m, n), out_dtype),
        compiler_params=_cparams(("parallel", "arbitrary")),
        name="matmul",
    )(*args)


def _ffn_kernel(n_ref, x_ref, w1_ref, w3_ref, w2_ref, o_ref):
    f = pl.program_id(1)
    n = n_ref[...]
    a = _dot(n, w1_ref[...])
    b = _dot(n, w3_ref[...])
    h = (_silu(a) * b * 0.5).astype(BF16)
    upd = _dot(h, w2_ref[...])

    @pl.when(f == 0)
    def _():
        o_ref[...] = x_ref[...] + upd

    @pl.when(f != 0)
    def _():
        o_ref[...] += upd


def ffn_half_step(n, x, w1, w3, w2):
    t, d = x.shape
    dff = w1.shape[1]
    tm = _tile(t, 512)
    tf = _tile(dff, 256)
    return pl.pallas_call(
        _ffn_kernel,
        grid=(t // tm, dff // tf),
        in_specs=[
            pl.BlockSpec((tm, d), lambda i, f: (i, 0), pipeline_mode=pl.Buffered(1)),
            pl.BlockSpec((tm, d), lambda i, f: (i, 0), pipeline_mode=pl.Buffered(1)),
            pl.BlockSpec((d, tf), lambda i, f: (0, f)),
            pl.BlockSpec((d, tf), lambda i, f: (0, f)),
            pl.BlockSpec((tf, d), lambda i, f: (f, 0)),
        ],
        out_specs=pl.BlockSpec((tm, d), lambda i, f: (i, 0)),
        out_shape=jax.ShapeDtypeStruct((t, d), F32),
        compiler_params=_cparams(("parallel", "arbitrary")),
        name="ffn",
    )(n, x, w1, w3, w2)


def _headnorm_kernel(x_ref, g_ref, o_ref, *, scale):
    o_ref[...] = (_row_rms(x_ref[...], g_ref[...]) * scale).astype(o_ref.dtype)


def headnorm_bf16(x, col0, nheads, g, scale=1.0):
    t = x.shape[0]
    tr = _tile(t, 1024)
    c0 = col0 // HEAD_DIM
    return pl.pallas_call(
        functools.partial(_headnorm_kernel, scale=scale),
        grid=(t // tr, nheads),
        in_specs=[pl.BlockSpec((tr, HEAD_DIM), lambda i, j: (i, c0 + j)),
                  pl.BlockSpec((1, HEAD_DIM), lambda i, j: (0, 0))],
        out_specs=pl.BlockSpec((tr, HEAD_DIM), lambda i, j: (i, j)),
        out_shape=jax.ShapeDtypeStruct((t, nheads * HEAD_DIM), BF16),
        compiler_params=_cparams(("parallel", "parallel")),
        name="headnorm",
    )(x, g.reshape(1, HEAD_DIM))


def _vt_kernel(x_ref, o_ref):
    o_ref[...] = x_ref[...].T.astype(o_ref.dtype)


def values_transposed(x, col0, b, l):
    tr = _tile(l, 512)
    nl = l // tr
    c0 = col0 // HEAD_DIM
    return pl.pallas_call(
        _vt_kernel,
        grid=(b, NSA_KV_HEADS, nl),
        in_specs=[pl.BlockSpec((tr, HEAD_DIM), lambda bi, h, i: (bi * nl + i, c0 + h))],
        out_specs=pl.BlockSpec((None, None, HEAD_DIM, tr), lambda bi, h, i: (bi, h, 0, i)),
        out_shape=jax.ShapeDtypeStruct((b, NSA_KV_HEADS, HEAD_DIM, l), BF16),
        compiler_params=_cparams(("parallel", "parallel", "parallel")),
        name="values_t",
    )(x)


def _compress_kernel(x_ref, pe_ref, w1_ref, b1_ref, w2_ref, b2_ref, kn_ref, o_ref, xs_ref, *, l, is_k):
    nc = l // CMP_STRIDE
    xs_ref[pl.ds(0, l), :] = x_ref[...]
    xs_ref[pl.ds(l, CMP_BLOCK), :] = jnp.zeros((CMP_BLOCK, HEAD_DIM), F32)
    acc = jnp.zeros((nc, CMP_HIDDEN), F32)
    for r in range(CMP_BLOCK):
        xr = xs_ref[pl.ds(r, nc, stride=CMP_STRIDE), :] + pe_ref[pl.ds(r, 1), :]
        acc = acc + _dot(xr.astype(BF16), w1_ref[r])
    hid = _gelu_tanh(acc + b1_ref[...])
    c = _dot(hid.astype(BF16), w2_ref[...]) + b2_ref[...]
    if is_k:
        o_ref[...] = _row_rms(c, kn_ref[...]).astype(o_ref.dtype)
    else:
        o_ref[...] = c.T.astype(o_ref.dtype)


def compress_prompt(kv, which, b, l, pe, w1, b1, w2, b2, kn):
    nc = l // CMP_STRIDE
    is_k = which == 0
    oshape = (b, NSA_KV_HEADS, nc, HEAD_DIM) if is_k else (b, NSA_KV_HEADS, HEAD_DIM, nc)
    oblock = (None, None, nc, HEAD_DIM) if is_k else (None, None, HEAD_DIM, nc)
    return pl.pallas_call(
        functools.partial(_compress_kernel, l=l, is_k=is_k),
        grid=(b, NSA_KV_HEADS),
        in_specs=[
            pl.BlockSpec((l, HEAD_DIM), lambda bi, h: (bi, which * NSA_KV_HEADS + h)),
            pl.BlockSpec((CMP_BLOCK, HEAD_DIM), lambda bi, h: (0, 0)),
            pl.BlockSpec((CMP_BLOCK, HEAD_DIM, CMP_HIDDEN), lambda bi, h: (0, 0, 0)),
            pl.BlockSpec((1, CMP_HIDDEN), lambda bi, h: (0, 0)),
            pl.BlockSpec((CMP_HIDDEN, HEAD_DIM), lambda bi, h: (0, 0)),
            pl.BlockSpec((1, HEAD_DIM), lambda bi, h: (0, 0)),
            pl.BlockSpec((1, HEAD_DIM), lambda bi, h: (0, 0)),
        ],
        out_specs=pl.BlockSpec(oblock, lambda bi, h: (bi, h, 0, 0)),
        out_shape=jax.ShapeDtypeStruct(oshape, BF16),
        scratch_shapes=[pltpu.VMEM((l + CMP_BLOCK, HEAD_DIM), F32)],
        compiler_params=_cparams(("parallel", "parallel")),
        name="compress_prompt",
    )(kv, pe, w1, b1.reshape(1, -1), w2, b2.reshape(1, -1), kn.reshape(1, -1))


def _select_topk(score, j, ntop):
    ns = score.shape[0]
    bias = jnp.full(score.shape, NEG, F32)
    for _ in range(ntop):
        mx = jnp.max(score, axis=0, keepdims=True)
        idx = jnp.min(jnp.where(score == mx, j, ns), axis=0, keepdims=True)
        hit = j == idx
        bias = jnp.where(hit, 0.0, bias)
        score = jnp.where(hit, -jnp.inf, score)
    return bias


def _cmp_attn_kernel(slope_ref, q_ref, kc_ref, vct_ref, mt_ref, o_ref, sel_ref, *, tq, nc, ns):
    kvh = pl.program_id(1)
    qi = pl.program_id(2)
    qp = qi * tq + lax.broadcasted_iota(jnp.int32, (nc, tq), 1)
    blk_end = lax.broadcasted_iota(jnp.int32, (nc, tq), 0) * CMP_STRIDE + (CMP_BLOCK - 1)
    dist = qp - blk_end
    mask = dist >= 0
    distf = dist.astype(F32)
    kc = kc_ref[...]
    vct = vct_ref[...]
    imp = jnp.zeros((nc, tq), F32)
    for g in range(NSA_GROUP):
        s = _dot_nt(kc, q_ref[:, g * HEAD_DIM:(g + 1) * HEAD_DIM])
        s = jnp.where(mask, s - slope_ref[kvh * NSA_GROUP + g] * distf, -jnp.inf)
        m = jnp.max(s, axis=0, keepdims=True)
        m = jnp.where(m == -jnp.inf, 0.0, m)
        e = jnp.where(mask, jnp.exp(s - m), 0.0)
        p = e / jnp.maximum(jnp.sum(e, axis=0, keepdims=True), 1e-30)
        imp = imp + p
        o_ref[:, g * HEAD_DIM:(g + 1) * HEAD_DIM] = _dot(vct, p.astype(BF16)).T
    score = jnp.dot(mt_ref[...], imp, preferred_element_type=F32, precision=lax.Precision.HIGHEST)
    j = lax.broadcasted_iota(jnp.int32, (ns, tq), 0)
    cur = (qi * tq + lax.broadcasted_iota(jnp.int32, (ns, tq), 1)) // SLC_BLOCK
    forced = (j == 0) | (j == cur) | (j == cur - 1)
    score = jnp.where(j <= cur, jnp.where(forced, BIG, score), -BIG)
    sel_ref[...] = _select_topk(score, j, min(SLC_TOPK, ns))


def _overlap_matrix(n_cmp, n_slc):
    cs = np.arange(n_cmp)[:, None] * CMP_STRIDE
    ss = np.arange(n_slc)[None, :] * SLC_BLOCK
    ov = np.minimum(cs + CMP_BLOCK, ss + SLC_BLOCK) - np.maximum(cs, ss)
    return (np.maximum(ov, 0).astype(np.float32) / CMP_BLOCK)


def cmp_attention_prompt(qn, kc, vct, slopes, b, l):
    nc = l // CMP_STRIDE
    ns = l // SLC_BLOCK
    tq = _tile(l, 512)
    nq = l // tq
    mt = jnp.asarray(_overlap_matrix(nc, ns).T)
    return pl.pallas_call(
        functools.partial(_cmp_attn_kernel, tq=tq, nc=nc, ns=ns),
        grid=(b, NSA_KV_HEADS, nq),
        in_specs=[
            pl.BlockSpec(memory_space=pltpu.SMEM),
            pl.BlockSpec((tq, GQ_W), lambda bi, h, i: (bi * nq + i, h)),
            pl.BlockSpec((None, None, nc, HEAD_DIM), lambda bi, h, i: (bi, h, 0, 0)),
            pl.BlockSpec((None, None, HEAD_DIM, nc), lambda bi, h, i: (bi, h, 0, 0)),
            pl.BlockSpec((ns, nc), lambda bi, h, i: (0, 0)),
        ],
        out_specs=[
            pl.BlockSpec((tq, GQ_W), lambda bi, h, i: (bi * nq + i, h)),
            pl.BlockSpec((None, None, ns, tq), lambda bi, h, i: (bi, h, 0, i)),
        ],
        out_shape=[jax.ShapeDtypeStruct((b * l, NSA_HEADS * HEAD_DIM), F32),
                   jax.ShapeDtypeStruct((b, NSA_KV_HEADS, ns, l), F32)],
        compiler_params=_cparams(("parallel", "parallel", "parallel")),
        name="cmp_attention",
    )(slopes, qn, kc, vct, mt)


def _flash_kernel(slope_ref, q_ref, k_ref, vt_ref, *rest, tq, tk, nkt, windowed):
    if windowed:
        o_ref, m_ref, l_ref, acc_ref = rest
        sel_ref = None
    else:
        sel_ref, o_ref, m_ref, l_ref, acc_ref = rest
    kvh = pl.program_id(1)
    qi = pl.program_id(2)
    kt = pl.program_id(3)
    if windowed:
        ktile = qi - (nkt - 1) + kt
        valid = ktile >= 0
    else:
        ktile = kt
        valid = kt * tk <= qi * tq + (tq - 1)

    @pl.when(kt == 0)
    def _():
        m_ref[...] = jnp.full(m_ref.shape, NEG, F32)
        l_ref[...] = jnp.zeros(l_ref.shape, F32)
        acc_ref[...] = jnp.zeros(acc_ref.shape, F32)

    @pl.when(valid)
    def _():
        kpos = ktile * tk + lax.broadcasted_iota(jnp.int32, (tk, tq), 0)
        qpos = qi * tq + lax.broadcasted_iota(jnp.int32, (tk, tq), 1)
        dist = qpos - kpos
        distf = dist.astype(F32)
        if windowed:
            bias = jnp.where((dist >= 0) & (dist < WINDOW), 0.0, NEG)
        else:
            sel = sel_ref[...]
            sel = jnp.concatenate(
                [jnp.broadcast_to(sel[r:r + 1, :], (SLC_BLOCK, tq)) for r in range(tk // SLC_BLOCK)], axis=0)
            bias = jnp.where(dist >= 0, sel, NEG)
        k = k_ref[...]
        vt = vt_ref[...]
        for g in range(NSA_GROUP):
            s = _dot_nt(k, q_ref[:, g * HEAD_DIM:(g + 1) * HEAD_DIM])
            s = s - slope_ref[kvh * NSA_GROUP + g] * distf + bias
            m_prev = m_ref[pl.ds(g, 1), :]
            m_new = jnp.maximum(m_prev, jnp.max(s, axis=0, keepdims=True))
            alpha = jnp.exp(m_prev - m_new)
            e = jnp.exp(s - m_new)
            l_ref[pl.ds(g, 1), :] = alpha * l_ref[pl.ds(g, 1), :] + jnp.sum(e, axis=0, keepdims=True)
            acc_ref[g] = alpha * acc_ref[g] + _dot(vt, e.astype(BF16))
            m_ref[pl.ds(g, 1), :] = m_new

    @pl.when(kt == nkt - 1)
    def _():
        for g in range(NSA_GROUP):
            o = acc_ref[g] / jnp.maximum(l_ref[pl.ds(g, 1), :], 1e-30)
            o_ref[:, g * HEAD_DIM:(g + 1) * HEAD_DIM] = o.T


def flash_prompt(qn, kn, vt, slopes, b, l, sel=None):
    windowed = sel is None
    tq = _tile(l, 512)
    tk = tq
    nq = l // tq
    nkt = (min(WINDOW, l) // tk + 1) if windowed else l // tk
    if windowed:
        kmap = lambda qi, kt: jnp.maximum(qi - (nkt - 1) + kt, 0)
    else:
        kmap = lambda qi, kt: jnp.minimum(kt, (qi * tq + tq - 1) // tk)
    in_specs = [
        pl.BlockSpec(memory_space=pltpu.SMEM),
        pl.BlockSpec((tq, GQ_W), lambda bi, h, qi, kt: (bi * nq + qi, h)),
        pl.BlockSpec((tk, HEAD_DIM), lambda bi, h, qi, kt: (bi * nq + kmap(qi, kt), h)),
        pl.BlockSpec((None, None, HEAD_DIM, tk), lambda bi, h, qi, kt: (bi, h, 0, kmap(qi, kt))),
    ]
    args = [slopes, qn, kn, vt]
    if not windowed:
        in_specs.append(pl.BlockSpec((None, None, tk // SLC_BLOCK, tq),
                                     lambda bi, h, qi, kt: (bi, h, kmap(qi, kt), qi)))
        args.append(sel)
    return pl.pallas_call(
        functools.partial(_flash_kernel, tq=tq, tk=tk, nkt=nkt, windowed=windowed),
        grid=(b, NSA_KV_HEADS, nq, nkt),
        in_specs=in_specs,
        out_specs=pl.BlockSpec((tq, GQ_W), lambda bi, h, qi, kt: (bi * nq + qi, h)),
        out_shape=jax.ShapeDtypeStruct((b * l, NSA_HEADS * HEAD_DIM), F32),
        scratch_shapes=[pltpu.VMEM((SUBLANE, tq), F32), pltpu.VMEM((SUBLANE, tq), F32),
                        pltpu.VMEM((NSA_GROUP, HEAD_DIM, tq), F32)],
        compiler_params=_cparams(("parallel", "parallel", "parallel", "arbitrary")),
        name="window_attention" if windowed else "selected_attention",
    )(*args)


def _combine_kernel(g_ref, c_ref, s_ref, w_ref, o_ref):
    gates = _sigmoid(g_ref[...])
    for h in range(NSA_HEADS):
        cols = slice(h * HEAD_DIM, (h + 1) * HEAD_DIM)
        o = (gates[:, h:h + 1] * c_ref[:, cols]
             + gates[:, NSA_HEADS + h:NSA_HEADS + h + 1] * s_ref[:, cols]
             + gates[:, 2 * NSA_HEADS + h:2 * NSA_HEADS + h + 1] * w_ref[:, cols])
        o_ref[:, cols] = o.astype(o_ref.dtype)


def combine_branches(gates, o_cmp, o_slc, o_win):
    t, w = o_cmp.shape
    tr = _tile(t, 256)
    spec = pl.BlockSpec((tr, w), lambda i: (i, 0))
    return pl.pallas_call(
        _combine_kernel,
        grid=(t // tr,),
        in_specs=[pl.BlockSpec((tr, gates.shape[1]), lambda i: (i, 0)), spec, spec, spec],
        out_specs=spec,
        out_shape=jax.ShapeDtypeStruct((t, w), BF16),
        compiler_params=_cparams(("parallel",)),
        name="combine",
    )(gates, o_cmp, o_slc, o_win)


def _group_norm_gate(o, rg, gn):
    mu = jnp.mean(o, axis=-1, keepdims=True)
    var = jnp.mean(jnp.square(o - mu), axis=-1, keepdims=True)
    return _silu(rg) * ((o - mu) * lax.rsqrt(var + EPS) * gn)


def _retention_kernel(q_ref, k_ref, v_ref, rg_ref, intra_ref, qd_ref, kd_ref, cd_ref, gn_ref,
                      o_ref, s_ref, st_ref, *, tl, hb):
    li = pl.program_id(2)
    nl = pl.num_programs(2)

    @pl.when(li == 0)
    def _():
        st_ref[...] = jnp.zeros(st_ref.shape, F32)

    def chunk(c, carry):
        rows = pl.ds(pl.multiple_of(c * RET_CHUNK, RET_CHUNK), RET_CHUNK)
        for h in range(hb):
            cols = slice(h * HEAD_DIM, (h + 1) * HEAD_DIM)
            qc = q_ref[rows, cols]
            kc = k_ref[rows, cols] * (HEAD_DIM ** -0.5)
            vc = v_ref[rows, cols].astype(BF16)
            state = st_ref[h]
            att = _dot_nt(qc.astype(BF16), kc.astype(BF16)) * intra_ref[h]
            o = _dot(att.astype(BF16), vc) + _dot((qc * qd_ref[h]).astype(BF16), state.astype(BF16))
            st_ref[h] = state * cd_ref[h] + _dot((kc * kd_ref[h]).T.astype(BF16), vc)
            o_ref[rows, cols] = _group_norm_gate(o, rg_ref[rows, cols], gn_ref[:, cols]).astype(o_ref.dtype)
        return carry

    lax.fori_loop(0, tl // RET_CHUNK, chunk, 0)

    @pl.when(li == nl - 1)
    def _():
        s_ref[...] = st_ref[...]


def _retention_tables(chunk):
    h = jnp.arange(RET_HEADS, dtype=F32)
    log_g = jnp.log1p(-jnp.exp2(-5.0 - h))
    i = jnp.arange(chunk, dtype=F32)
    diff = i[:, None] - i[None, :]
    intra = jnp.where(diff >= 0, jnp.exp(jnp.maximum(diff, 0.0)[None] * log_g[:, None, None]), 0.0)
    q_dec = jnp.exp((i[None, :] + 1.0) * log_g[:, None])
    k_dec = jnp.exp((chunk - 1.0 - i)[None, :] * log_g[:, None])
    c_dec = jnp.exp(chunk * log_g)
    return intra, q_dec, k_dec, c_dec


def retention_prompt(r, gn, b, l):
    intra, q_dec, k_dec, c_dec = _retention_tables(RET_CHUNK)
    bc = lambda t: jnp.broadcast_to(t[:, :, None], (RET_HEADS, RET_CHUNK, HEAD_DIM))
    qd, kd = bc(q_dec), bc(k_dec)
    cd = jnp.broadcast_to(c_dec[:, None, None], (RET_HEADS, HEAD_DIM, HEAD_DIM))
    hb = 4
    ng = RET_HEADS // hb
    tl = _tile(l, 1024)
    nl = l // tl
    w = hb * HEAD_DIM
    rspec = lambda part: pl.BlockSpec((tl, w), lambda bi, g, li: (bi * nl + li, part * ng + g))
    tspec = pl.BlockSpec((hb, RET_CHUNK, HEAD_DIM), lambda bi, g, li: (g, 0, 0))
    return pl.pallas_call(
        functools.partial(_retention_kernel, tl=tl, hb=hb),
        grid=(b, ng, nl),
        in_specs=[rspec(0), rspec(1), rspec(2), rspec(3), tspec, tspec, tspec, tspec,
                  pl.BlockSpec((1, w), lambda bi, g, li: (0, g))],
        out_specs=[pl.BlockSpec((tl, w), lambda bi, g, li: (bi * nl + li, g)),
                   pl.BlockSpec((None, hb, HEAD_DIM, HEAD_DIM), lambda bi, g, li: (bi, g, 0, 0))],
        out_shape=[jax.ShapeDtypeStruct((b * l, RET_HEADS * HEAD_DIM), BF16),
                   jax.ShapeDtypeStruct((b, RET_HEADS, HEAD_DIM, HEAD_DIM), F32)],
        scratch_shapes=[pltpu.VMEM((hb, HEAD_DIM, HEAD_DIM), F32)],
        compiler_params=_cparams(("parallel", "parallel", "arbitrary")),
        name="retention_prompt",
    )(r, r, r, r, intra, qd, kd, cd, gn.reshape(1, -1))


def _group_rows(kvh):
    row = lax.broadcasted_iota(jnp.int32, (NSA_HEADS, 1), 0)
    return (row >= kvh * NSA_GROUP) & (row < (kvh + 1) * NSA_GROUP)


def _sample_cmp_kernel(pt_ref, *refs, pp, npg, pos0):
    pages = refs[:pp]
    (q_ref, slope_ref, pea_ref, peb_ref, w1a_ref, w1b_ref, b1_ref, w2_ref, b2_ref, kn_ref, m_ref,
     o_ref, idx_ref, r_ref, x_ref, f_ref, s_ref) = refs[pp:]
    jp = pl.program_id(1)
    cpp = PAGE_ROWS // CMP_STRIDE
    nck = npg * cpp
    half = NSA_KV_HEADS * HEAD_DIM

    @pl.when(jp == 0)
    def _():
        s_ref[...] = jnp.zeros(s_ref.shape, F32)

    for t in range(pp):
        for w in range(2):
            for kvh in range(NSA_KV_HEADS):
                g = w * NSA_KV_HEADS + kvh
                r_ref[t, g] = pages[t][:, g * HEAD_DIM:(g + 1) * HEAD_DIM]
                for c in range(CMP_STRIDE):
                    x_ref[w, pl.ds((t * NSA_KV_HEADS + kvh) * cpp, cpp), pl.ds(c * HEAD_DIM, HEAD_DIM)] = (
                        r_ref[t, g, pl.ds(c, cpp, stride=CMP_STRIDE), :])
    for w in range(2):
        x = x_ref[w]
        fa = _dot((x + pea_ref[w]).astype(BF16), w1a_ref[w])
        sb = _dot((x + peb_ref[w]).astype(BF16), w1b_ref[w])
        for t in range(pp):
            for kvh in range(NSA_KV_HEADS):
                src = slice((t * NSA_KV_HEADS + kvh) * cpp, (t * NSA_KV_HEADS + kvh + 1) * cpp)
                dst = pl.ds(pl.multiple_of((jp * pp + t) * cpp, cpp), cpp)
                f_ref[w, kvh, dst, :] = fa[src]
                s_ref[w, kvh, dst, :] = sb[src]

    @pl.when(jp == pl.num_programs(1) - 1)
    def _():
        q = q_ref[...]
        n_io = lax.broadcasted_iota(jnp.int32, (NSA_HEADS, nck), 1)
        dist = pos0 - (n_io * CMP_STRIDE + (CMP_BLOCK - 1))
        mask = dist >= 0
        bias = slope_ref[:, :nck] * dist.astype(F32)
        o_acc = jnp.zeros((NSA_HEADS, HEAD_DIM), F32)
        imp_rows = []
        for kvh in range(NSA_KV_HEADS):
            def cblock(w):
                hid = _gelu_tanh(f_ref[w, kvh, pl.ds(0, nck), :] + s_ref[w, kvh, pl.ds(1, nck), :] + b1_ref[w])
                return _dot(hid.astype(BF16), w2_ref[w]) + b2_ref[w]
            kc = _row_rms(cblock(0), kn_ref[...]).astype(BF16)
            vc = cblock(1).astype(BF16)
            s = jnp.where(mask, _dot_nt(q, kc) - bias, -jnp.inf)
            m = jnp.max(s, axis=1, keepdims=True)
            m = jnp.where(m == -jnp.inf, 0.0, m)
            e = jnp.where(mask, jnp.exp(s - m), 0.0)
            p = e / jnp.maximum(jnp.sum(e, axis=1, keepdims=True), 1e-30)
            grp = _group_rows(kvh)
            o_acc = o_acc + jnp.where(grp, _dot(p.astype(BF16), vc), 0.0)
            imp_rows.append(jnp.sum(jnp.where(grp, p, 0.0), axis=0, keepdims=True))
        o_ref[...] = o_acc
        imp = jnp.concatenate(imp_rows + [jnp.zeros((SUBLANE - NSA_KV_HEADS, nck), F32)], axis=0)
        score = jnp.dot(imp, m_ref[...], preferred_element_type=F32, precision=lax.Precision.HIGHEST)
        nsp = score.shape[1]
        j = lax.broadcasted_iota(jnp.int32, (SUBLANE, nsp), 1)
        cur = pos0 // SLC_BLOCK
        forced = (j == 0) | (j == cur) | (j == cur - 1)
        score = jnp.where(j <= cur, jnp.where(forced, BIG, score), -BIG)
        lane = lax.broadcasted_iota(jnp.int32, (SUBLANE, LANE), 1)
        out = jnp.zeros((SUBLANE, LANE), jnp.int32)
        for t in range(SLC_TOPK):
            mx = jnp.max(score, axis=1, keepdims=True)
            idx = jnp.min(jnp.where(score == mx, j, nsp), axis=1, keepdims=True)
            out = jnp.where(lane == t, idx, out)
            score = jnp.where(j == idx, -jnp.inf, score)
        idx_ref[...] = out


PAGE_ROWS = 128


def sample_cmp_select(cache2, page_table, qn3, slopes_b, pe, w1, b1, w2, b2, kn, pos0):
    db, npg = page_table.shape
    pp = _tile(npg, 8)
    cpp = PAGE_ROWS // CMP_STRIDE
    nck = npg * cpp
    half = NSA_KV_HEADS * HEAD_DIM
    n_slc = (pos0 + 1 + SLC_BLOCK - 1) // SLC_BLOCK
    nsp = -(-n_slc // LANE) * LANE
    m = np.zeros((nck, nsp), np.float32)
    m[:, :n_slc] = _overlap_matrix(nck, n_slc)
    flat = lambda a: a.reshape(2, 1, CMP_STRIDE * HEAD_DIM)
    pea, peb = flat(pe[:, :CMP_STRIDE]), flat(pe[:, CMP_STRIDE:])
    w1a = w1[:, :CMP_STRIDE].reshape(2, CMP_STRIDE * HEAD_DIM, CMP_HIDDEN)
    w1b = w1[:, CMP_STRIDE:].reshape(2, CMP_STRIDE * HEAD_DIM, CMP_HIDDEN)
    const = lambda shape: pl.BlockSpec(shape, lambda bi, jp, pt: (0,) * len(shape))
    page_specs = [
        pl.BlockSpec((None, PAGE_ROWS, 2 * half), functools.partial(lambda bi, jp, pt, t: (pt[bi, jp * pp + t], 0, 0), t=t))
        for t in range(pp)]
    grid_spec = pltpu.PrefetchScalarGridSpec(
        num_scalar_prefetch=1,
        grid=(db, npg // pp),
        in_specs=page_specs + [
            pl.BlockSpec((None, NSA_HEADS, HEAD_DIM), lambda bi, jp, pt: (bi, 0, 0)),
            const((NSA_HEADS, slopes_b.shape[1])),
            const((2, 1, CMP_STRIDE * HEAD_DIM)), const((2, 1, CMP_STRIDE * HEAD_DIM)),
            const((2, CMP_STRIDE * HEAD_DIM, CMP_HIDDEN)), const((2, CMP_STRIDE * HEAD_DIM, CMP_HIDDEN)),
            const((2, 1, CMP_HIDDEN)), const((2, CMP_HIDDEN, HEAD_DIM)), const((2, 1, HEAD_DIM)),
            const((1, HEAD_DIM)), const((nck, nsp)),
        ],
        out_specs=[pl.BlockSpec((None, NSA_HEADS, HEAD_DIM), lambda bi, jp, pt: (bi, 0, 0)),
                   pl.BlockSpec((None, SUBLANE, LANE), lambda bi, jp, pt: (bi, 0, 0))],
        scratch_shapes=[
            pltpu.VMEM((pp, 2 * NSA_KV_HEADS, PAGE_ROWS, HEAD_DIM), F32),
            pltpu.VMEM((2, pp * NSA_KV_HEADS * cpp, CMP_STRIDE * HEAD_DIM), F32),
            pltpu.VMEM((2, NSA_KV_HEADS, nck, CMP_HIDDEN), F32),
            pltpu.VMEM((2, NSA_KV_HEADS, nck + SUBLANE, CMP_HIDDEN), F32),
        ],
    )
    return pl.pallas_call(
        functools.partial(_sample_cmp_kernel, pp=pp, npg=npg, pos0=pos0),
        grid_spec=grid_spec,
        out_shape=[jax.ShapeDtypeStruct((db, NSA_HEADS, HEAD_DIM), F32),
                   jax.ShapeDtypeStruct((db, SUBLANE, LANE), jnp.int32)],
        compiler_params=_cparams(("parallel", "arbitrary")),
        name="sample_cmp_select",
    )(page_table, *([cache2] * pp), qn3, slopes_b, pea, peb, w1a, w1b, b1.reshape(2, 1, -1), w2,
      b2.reshape(2, 1, -1), kn.reshape(1, -1), jnp.asarray(m))


def _softmax_with_new_key(s, s_new):
    m = jnp.maximum(jnp.max(s, axis=1, keepdims=True), s_new)
    e = jnp.exp(s - m)
    e_new = jnp.exp(s_new - m)
    return e, e_new, jnp.maximum(jnp.sum(e, axis=1, keepdims=True) + e_new, 1e-30)


def _bf16_round(x):
    return x.astype(BF16).astype(F32)


def _sample_slc_kernel(blk_ref, cache_ref, q_ref, kpos_ref, slope_ref, knew_ref, vnew_ref, kn_ref,
                       o_ref, kbuf, vbuf, sem, *, pos0, kcol, vcol):
    bi = pl.program_id(0)
    nsel = SLC_TOPK * SLC_BLOCK

    def copies(kvh, t):
        blk = blk_ref[(bi * NSA_KV_HEADS + kvh) * SLC_TOPK + t]
        dst = pl.ds(t * SLC_BLOCK, SLC_BLOCK)
        ck = pltpu.make_async_copy(cache_ref.at[blk, :, pl.ds(kcol + kvh * HEAD_DIM, HEAD_DIM)],
                                   kbuf.at[kvh, dst, :], sem.at[0])
        cv = pltpu.make_async_copy(cache_ref.at[blk, :, pl.ds(vcol + kvh * HEAD_DIM, HEAD_DIM)],
                                   vbuf.at[kvh, dst, :], sem.at[1])
        return ck, cv

    for kvh in range(NSA_KV_HEADS):
        for t in range(SLC_TOPK):
            ck, cv = copies(kvh, t)
            ck.start()
            cv.start()
    for kvh in range(NSA_KV_HEADS):
        for t in range(SLC_TOPK):
            ck, cv = copies(kvh, t)
            ck.wait()
            cv.wait()

    q = q_ref[...]
    o_acc = jnp.zeros((NSA_HEADS, HEAD_DIM), F32)
    for kvh in range(NSA_KV_HEADS):
        kn = _row_rms(kbuf[kvh], kn_ref[...]).astype(BF16)
        dist = pos0 - kpos_ref[pl.ds(kvh, 1), :]
        s = _dot_nt(q, kn) - slope_ref[:, :nsel] * dist.astype(F32)
        s = jnp.where(dist >= 0, s, NEG)
        knew = _row_rms(knew_ref[pl.ds(kvh, 1), :], kn_ref[...])
        s_new = jnp.sum(q.astype(F32) * _bf16_round(knew), axis=1, keepdims=True)
        e, e_new, l = _softmax_with_new_key(s, s_new)
        o = _dot(e.astype(BF16), vbuf[kvh].astype(BF16)) + _bf16_round(e_new) * _bf16_round(vnew_ref[pl.ds(kvh, 1), :])
        o_acc = o_acc + jnp.where(_group_rows(kvh), o / l, 0.0)
    o_ref[...] = o_acc


def sample_slc_attention(cache_half, blk, qn3, kpos, slopes_b, knew, vnew, kn, pos0):
    db = qn3.shape[0]
    nsel = SLC_TOPK * SLC_BLOCK
    half = NSA_KV_HEADS * HEAD_DIM
    grid_spec = pltpu.PrefetchScalarGridSpec(
        num_scalar_prefetch=1,
        grid=(db,),
        in_specs=[
            pl.BlockSpec(memory_space=pl.ANY),
            pl.BlockSpec((None, NSA_HEADS, HEAD_DIM), lambda bi, blk: (bi, 0, 0)),
            pl.BlockSpec((None, NSA_KV_HEADS, nsel), lambda bi, blk: (bi, 0, 0)),
            pl.BlockSpec((NSA_HEADS, slopes_b.shape[1]), lambda bi, blk: (0, 0)),
            pl.BlockSpec((None, NSA_KV_HEADS, HEAD_DIM), lambda bi, blk: (bi, 0, 0)),
            pl.BlockSpec((None, NSA_KV_HEADS, HEAD_DIM), lambda bi, blk: (bi, 0, 0)),
            pl.BlockSpec((1, HEAD_DIM), lambda bi, blk: (0, 0)),
        ],
        out_specs=pl.BlockSpec((None, NSA_HEADS, HEAD_DIM), lambda bi, blk: (bi, 0, 0)),
        scratch_shapes=[pltpu.VMEM((NSA_KV_HEADS, nsel, HEAD_DIM), F32),
                        pltpu.VMEM((NSA_KV_HEADS, nsel, HEAD_DIM), F32),
                        pltpu.SemaphoreType.DMA((2,))],
    )
    return pl.pallas_call(
        functools.partial(_sample_slc_kernel, pos0=pos0, kcol=2 * half, vcol=3 * half),
        grid_spec=grid_spec,
        out_shape=jax.ShapeDtypeStruct((db, NSA_HEADS, HEAD_DIM), F32),
        compiler_params=_cparams(("arbitrary",)),
        name="sample_slc_attention",
    )(blk, cache_half, qn3, kpos, slopes_b, knew, vnew, kn.reshape(1, -1))


def _sample_win_kernel(sw_ref, q_ref, slope_ref, new_ref, kn_ref, g_ref, oc_ref, os_ref, o_ref, nw_ref, *, wb):
    half = NSA_KV_HEADS * HEAD_DIM
    q = q_ref[...]
    row = lax.broadcasted_iota(jnp.int32, (NSA_HEADS, wb), 1)
    dist = wb - row
    valid = dist < WINDOW
    bias = slope_ref[:, :wb] * dist.astype(F32)
    o_acc = jnp.zeros((NSA_HEADS, HEAD_DIM), F32)
    for kvh in range(NSA_KV_HEADS):
        cols = slice(kvh * HEAD_DIM, (kvh + 1) * HEAD_DIM)
        kn = _row_rms(sw_ref[:, cols], kn_ref[...]).astype(BF16)
        s = jnp.where(valid, _dot_nt(q, kn) - bias, NEG)
        knew = _row_rms(new_ref[:, cols], kn_ref[...])
        s_new = jnp.sum(q.astype(F32) * _bf16_round(knew), axis=1, keepdims=True)
        e, e_new, l = _softmax_with_new_key(s, s_new)
        vcols = slice(half + kvh * HEAD_DIM, half + (kvh + 1) * HEAD_DIM)
        o = _dot(e.astype(BF16), sw_ref[:, vcols].astype(BF16)) + _bf16_round(e_new) * _bf16_round(new_ref[:, vcols])
        o_acc = o_acc + jnp.where(_group_rows(kvh), o / l, 0.0)
    g = _sigmoid(g_ref[...])
    o_ref[...] = (g[0] * oc_ref[...] + g[1] * os_ref[...] + g[2] * o_acc).astype(o_ref.dtype)
    nw_ref[pl.ds(0, wb - 1), :] = sw_ref[pl.ds(1, wb - 1), :]
    nw_ref[pl.ds(wb - 1, 1), :] = new_ref[...]


def sample_win_attention(sw, qn3, slopes_b, win_new, kn, gates_b, o_cmp, o_slc):
    db, wb, wcols = sw.shape
    hspec = pl.BlockSpec((None, NSA_HEADS, HEAD_DIM), lambda bi: (bi, 0, 0))
    return pl.pallas_call(
        functools.partial(_sample_win_kernel, wb=wb),
        grid=(db,),
        in_specs=[
            pl.BlockSpec((None, wb, wcols), lambda bi: (bi, 0, 0)),
            hspec,
            pl.BlockSpec((NSA_HEADS, slopes_b.shape[1]), lambda bi: (0, 0)),
            pl.BlockSpec((None, 1, wcols), lambda bi: (bi, 0, 0)),
            pl.BlockSpec((1, HEAD_DIM), lambda bi: (0, 0)),
            pl.BlockSpec((None, 3, NSA_HEADS, HEAD_DIM), lambda bi: (bi, 0, 0, 0)),
            hspec, hspec,
        ],
        out_specs=[hspec, pl.BlockSpec((None, wb, wcols), lambda bi: (bi, 0, 0))],
        out_shape=[jax.ShapeDtypeStruct((db, NSA_HEADS, HEAD_DIM), BF16),
                   jax.ShapeDtypeStruct((db, wb, wcols), F32)],
        compiler_params=_cparams(("parallel",)),
        name="sample_win_attention",
    )(sw, qn3, slopes_b, win_new.reshape(db, 1, wcols), kn.reshape(1, -1), gates_b, o_cmp, o_slc)


def _sample_ret_kernel(st_ref, r_ref, gam_ref, gn_ref, o_ref, s_ref):
    q = r_ref[0]
    k = r_ref[1] * (HEAD_DIM ** -0.5)
    v = r_ref[2]
    rg = r_ref[3]
    gam = gam_ref[...]
    qb, kb, vb = _bf16_round(q), _bf16_round(k), _bf16_round(v)
    att = jnp.sum(qb * kb, axis=1, keepdims=True)
    kt = kb.T
    qd = (q * gam).astype(BF16)
    rows = []
    for h in range(RET_HEADS):
        state = st_ref[h]
        rows.append(_dot(qd[h:h + 1, :], state.astype(BF16)))
        s_ref[h] = state * gam[h:h + 1, :] + kt[:, h:h + 1] * vb[h:h + 1, :]
    o = _bf16_round(att) * vb + jnp.concatenate(rows, axis=0)
    o_ref[...] = _group_norm_gate(o, rg, gn_ref[...]).astype(o_ref.dtype)


def sample_retention(state, r4, gn):
    db = state.shape[0]
    _, q_dec, _, _ = _retention_tables(1)
    gam = jnp.broadcast_to(q_dec, (RET_HEADS, HEAD_DIM))
    return pl.pallas_call(
        _sample_ret_kernel,
        grid=(db,),
        in_specs=[
            pl.BlockSpec((None, RET_HEADS, HEAD_DIM, HEAD_DIM), lambda bi: (bi, 0, 0, 0)),
            pl.BlockSpec((None, 4, RET_HEADS, HEAD_DIM), lambda bi: (bi, 0, 0, 0)),
            pl.BlockSpec((RET_HEADS, HEAD_DIM), lambda bi: (0, 0)),
            pl.BlockSpec((RET_HEADS, HEAD_DIM), lambda bi: (0, 0)),
        ],
        out_specs=[pl.BlockSpec((None, RET_HEADS, HEAD_DIM), lambda bi: (bi, 0, 0)),
                   pl.BlockSpec((None, RET_HEADS, HEAD_DIM, HEAD_DIM), lambda bi: (bi, 0, 0, 0))],
        out_shape=[jax.ShapeDtypeStruct((db, RET_HEADS, HEAD_DIM), BF16),
                   jax.ShapeDtypeStruct((db, RET_HEADS, HEAD_DIM, HEAD_DIM), F32)],
        compiler_params=_cparams(("parallel",)),
        name="sample_retention",
    )(state, r4, gam, gn.reshape(RET_HEADS, HEAD_DIM))


def _alibi_slopes():
    i = np.arange(NSA_HEADS, dtype=np.float32)
    return np.exp2(-8.0 * (i + 1.0) / NSA_HEADS).astype(np.float32)


def _in_proj_weights(w_in):
    nq = NSA_HEADS * HEAD_DIM
    nkv = 4 * NSA_KV_HEADS * HEAD_DIM
    nwin = 2 * NSA_KV_HEADS * HEAD_DIM
    ng = 3 * NSA_HEADS
    nr = 4 * RET_HEADS * HEAD_DIM
    o1, o2, o3, o4 = nq, nq + nkv, nq + nkv + nwin, nq + nkv + nwin + ng
    assert w_in.shape[1] == o4 + nr
    wg = jnp.pad(w_in[:, o3:o4], ((0, 0), (0, LANE - ng)))
    return [w.astype(BF16) for w in (w_in[:, :o1], w_in[:, o1:o2], w_in[:, o2:o3], wg, w_in[:, o4:])]


def kernel(x_prompt, x_sample, cache_nsa_kv, state_win_kv, state_ret, page_table, norm_g, ffn_w1, ffn_w3, ffn_w2,
           w_in, nsa_q_norm, nsa_k_norm, cmp_pe, cmp_w1, cmp_b1, cmp_w2, cmp_b2, ret_gn_g, w_out):
    assert cache_nsa_kv.shape[0] == 1 and x_sample.shape[1] == 1
    b, l, d = x_prompt.shape
    db = x_sample.shape[0]
    npg = page_table.shape[1]
    n_pool = cache_nsa_kv.shape[1]
    assert cache_nsa_kv.shape[2] == PAGE_ROWS
    pos0 = npg * PAGE_ROWS
    half = NSA_KV_HEADS * HEAD_DIM

    w1 = ffn_w1[0].astype(BF16)
    w3 = ffn_w3[0].astype(BF16)
    w2 = ffn_w2[0].astype(BF16)
    wq, wkv, wwin, wg, wr = _in_proj_weights(w_in[0])
    wo = w_out[0].astype(BF16)
    ng = norm_g[0]
    qg, kg = nsa_q_norm[0], nsa_k_norm[0]
    pe = cmp_pe[0]
    cw1 = cmp_w1[0].astype(BF16)
    cb1 = cmp_b1[0]
    cw2 = cmp_w2[0].astype(BF16)
    cb2 = cmp_b2[0]
    gn = ret_gn_g[0]
    slopes = jnp.asarray(_alibi_slopes())
    q_scale = HEAD_DIM ** -0.5

    def front(x):
        h = ffn_half_step(rmsnorm_bf16(x, ng[0]), x, w1[0], w3[0], w2[0])
        n = rmsnorm_bf16(h, ng[1])
        return h, [matmul(n, w) for w in (wq, wkv, wwin, wg, wr)]

    def back(h, mixed):
        h = matmul(mixed, wo, res=h)
        return ffn_half_step(rmsnorm_bf16(h, ng[2]), h, w1[1], w3[1], w2[1])

    hp, (q, kv, win, gates, r) = front(x_prompt.reshape(b * l, d))
    qn = headnorm_bf16(q, 0, NSA_HEADS, qg, q_scale)
    kc = compress_prompt(kv, 0, b, l, pe[0], cw1[0], cb1[0], cw2[0], cb2[0], kg[0])
    vct = compress_prompt(kv, 1, b, l, pe[1], cw1[1], cb1[1], cw2[1], cb2[1], kg[0])
    o_cmp, sel = cmp_attention_prompt(qn, kc, vct, slopes, b, l)
    k_slc = headnorm_bf16(kv, 2 * half, NSA_KV_HEADS, kg[1])
    v_slc = values_transposed(kv, 3 * half, b, l)
    o_slc = flash_prompt(qn, k_slc, v_slc, slopes, b, l, sel=sel)
    k_win = headnorm_bf16(win, 0, NSA_KV_HEADS, kg[2])
    v_win = values_transposed(win, half, b, l)
    o_win = flash_prompt(qn, k_win, v_win, slopes, b, l)
    o_nsa = combine_branches(gates, o_cmp, o_slc, o_win)
    o_ret, ret_p = retention_prompt(r, gn, b, l)
    y_prompt = back(hp, jnp.concatenate([o_nsa, o_ret], axis=1)).reshape(b, l, d)
    kv_prompt = kv.reshape(1, b, l, 4, NSA_KV_HEADS, HEAD_DIM)
    wl = min(WINDOW, l)
    win_prompt = win.reshape(b, l, 2, NSA_KV_HEADS, HEAD_DIM)[:, l - wl:][None]

    hs, (q, kv, win, gates, r) = front(x_sample.reshape(db, d))
    qn3 = headnorm_bf16(q, 0, NSA_HEADS, qg, q_scale).reshape(db, NSA_HEADS, HEAD_DIM)
    slopes_b = jnp.broadcast_to(slopes[:, None], (NSA_HEADS, max(npg * (PAGE_ROWS // CMP_STRIDE),
                                                                 SLC_TOPK * SLC_BLOCK, state_win_kv.shape[2])))
    o_cmp, idx = sample_cmp_select(cache_nsa_kv.reshape(n_pool, PAGE_ROWS, 4 * half), page_table, qn3, slopes_b,
                                   pe, cw1, cb1, cw2, cb2, kg[0], pos0)
    idx = idx[:, :NSA_KV_HEADS, :SLC_TOPK]
    n_past = pos0 // SLC_BLOCK
    in_cache = idx < n_past
    pidx = jnp.minimum(idx, n_past - 1)
    per_page = PAGE_ROWS // SLC_BLOCK
    page = jnp.take_along_axis(page_table, (pidx // per_page).reshape(db, -1), axis=1).reshape(pidx.shape)
    blk = (page * per_page + pidx % per_page).astype(jnp.int32).reshape(-1)
    offs = jnp.arange(SLC_BLOCK, dtype=jnp.int32)
    kpos = jnp.where(in_cache[..., None], idx[..., None] * SLC_BLOCK + offs, pos0 + 1).reshape(db, NSA_KV_HEADS, -1)
    kv4 = kv.reshape(db, 4, NSA_KV_HEADS, HEAD_DIM)
    o_slc = sample_slc_attention(cache_nsa_kv.reshape(n_pool * per_page, SLC_BLOCK, 4 * half), blk, qn3, kpos,
                                 slopes_b, kv4[:, 2], kv4[:, 3], kg[1], pos0)
    gates_b = jnp.broadcast_to(gates[:, :3 * NSA_HEADS].reshape(db, 3, NSA_HEADS, 1), (db, 3, NSA_HEADS, HEAD_DIM))
    wb = state_win_kv.shape[2]
    o_nsa, win_s = sample_win_attention(state_win_kv.reshape(db, wb, 2 * half), qn3, slopes_b, win, kg[2], gates_b,
                                        o_cmp, o_slc)
    o_ret, ret_s = sample_retention(state_ret[0], r.reshape(db, 4, RET_HEADS, HEAD_DIM), gn)
    mixed = jnp.concatenate([o_nsa.reshape(db, -1), o_ret.reshape(db, -1)], axis=1)
    y_sample = back(hs, mixed).reshape(db, 1, d)
    kv_sample = kv.reshape(1, db, 1, 4, NSA_KV_HEADS, HEAD_DIM)
    win_sample = win_s.reshape(1, db, wb, 2, NSA_KV_HEADS, HEAD_DIM)

    return (y_prompt, y_sample, kv_prompt, kv_sample, win_prompt, win_sample, ret_p[None], ret_s[None])
```

```python
import functools
import math

import numpy as np
import jax
import jax.numpy as jnp
from jax import lax
from jax.experimental import pallas as pl
from jax.experimental.pallas import tpu as pltpu

F32 = jnp.float32
BF16 = jnp.bfloat16

HEAD_DIM = 128
NSA_HEADS = 16
NSA_KV_HEADS = 4
NSA_GROUP = NSA_HEADS // NSA_KV_HEADS
RET_HEADS = 16
CMP_BLOCK = 32
CMP_STRIDE = 16
CMP_HIDDEN = 2 * HEAD_DIM
SLC_BLOCK = 64
SLC_TOPK = 16
WINDOW = 512
RET_CHUNK = 128
EPS = 1e-6
BIG = 1e30
NEG = -1e30
LANE = 128
SUBLANE = 8
VMEM_LIMIT = 56 * 1024 * 1024

GQ_W = NSA_GROUP * HEAD_DIM


def _cparams(sem):
    return pltpu.CompilerParams(dimension_semantics=sem, vmem_limit_bytes=VMEM_LIMIT)


def _tile(n, pref):
    if n <= pref:
        return n
    t = pref
    while n % t:
        t //= 2
    return t


def _dot(a, b):
    return jnp.dot(a, b, preferred_element_type=F32)


def _dot_nt(a, b):
    return lax.dot_general(a, b, (((1,), (1,)), ((), ())), preferred_element_type=F32)


def _gelu_tanh(x):
    return 0.5 * x * (1.0 + jnp.tanh(math.sqrt(2.0 / math.pi) * (x + 0.044715 * (x * x * x))))


def _silu(x):
    return x * (1.0 / (1.0 + jnp.exp(-x)))


def _sigmoid(x):
    return 1.0 / (1.0 + jnp.exp(-x))


def _row_rms(x, g):
    return x * lax.rsqrt(jnp.mean(x * x, axis=-1, keepdims=True) + EPS) * g


def _rmsnorm_kernel(x_ref, g_ref, o_ref):
    o_ref[...] = _row_rms(x_ref[...], g_ref[...]).astype(o_ref.dtype)


def rmsnorm_bf16(x, g):
    t, d = x.shape
    tr = _tile(t, 256)
    return pl.pallas_call(
        _rmsnorm_kernel,
        grid=(t // tr,),
        in_specs=[pl.BlockSpec((tr, d), lambda i: (i, 0)), pl.BlockSpec((1, d), lambda i: (0, 0))],
        out_specs=pl.BlockSpec((tr, d), lambda i: (i, 0)),
        out_shape=jax.ShapeDtypeStruct((t, d), BF16),
        compiler_params=_cparams(("parallel",)),
        name="rmsnorm",
    )(x, g.reshape(1, d))


def _mm_kernel(a_ref, w_ref, o_ref):
    o_ref[...] = _dot(a_ref[...], w_ref[...]).astype(o_ref.dtype)


def _mm_res_kernel(a_ref, w_ref, r_ref, o_ref):
    o_ref[...] = (r_ref[...] + _dot(a_ref[...], w_ref[...])).astype(o_ref.dtype)


def matmul(a, w, res=None, out_dtype=F32):
    m, k = a.shape
    n = w.shape[1]
    tm = _tile(m, 1024)
    tn = _tile(n, 512)
    in_specs = [pl.BlockSpec((tm, k), lambda i, j: (i, 0)), pl.BlockSpec((k, tn), lambda i, j: (0, j))]
    args = [a, w]
    body = _mm_kernel
    if res is not None:
        in_specs.append(pl.BlockSpec((tm, tn), lambda i, j: (i, j)))
        args.append(res)
        body = _mm_res_kernel
    return pl.pallas_call(
        body,
        grid=(m // tm, n // tn),
        in_specs=in_specs,
        out_specs=pl.BlockSpec((tm, tn), lambda i, j: (i, j)),
        out_shape=jax.ShapeDtypeStruct((m, n), out_dtype),
        compiler_params=_cparams(("parallel", "arbitrary")),
        name="matmul",
    )(*args)


def _ffn_kernel(x_hbm, g_ref, w1_ref, w3_ref, w2_ref, o_ref, n_ref, sem, *, tm):
    i = pl.program_id(0)
    f = pl.program_id(1)

    @pl.when(f == 0)
    def _():
        cp = pltpu.make_async_copy(x_hbm.at[pl.ds(pl.multiple_of(i * tm, tm), tm), :], o_ref, sem.at[0])
        cp.start()
        cp.wait()
        n_ref[...] = _row_rms(o_ref[...], g_ref[...]).astype(BF16)

    n = n_ref[...]
    h = (_silu(_dot(n, w1_ref[...])) * _dot(n, w3_ref[...]) * 0.5).astype(BF16)
    o_ref[...] += _dot(h, w2_ref[...])


def ffn_half_step(x, g, w1, w3, w2):
    t, d = x.shape
    dff = w1.shape[1]
    tm = _tile(t, 1024)
    tf = _tile(dff, 256)
    return pl.pallas_call(
        functools.partial(_ffn_kernel, tm=tm),
        grid=(t // tm, dff // tf),
        in_specs=[
            pl.BlockSpec(memory_space=pl.ANY),
            pl.BlockSpec((1, d), lambda i, f: (0, 0)),
            pl.BlockSpec((d, tf), lambda i, f: (0, f)),
            pl.BlockSpec((d, tf), lambda i, f: (0, f)),
            pl.BlockSpec((tf, d), lambda i, f: (f, 0)),
        ],
        out_specs=pl.BlockSpec((tm, d), lambda i, f: (i, 0), pipeline_mode=pl.Buffered(1)),
        out_shape=jax.ShapeDtypeStruct((t, d), F32),
        scratch_shapes=[pltpu.VMEM((tm, d), BF16), pltpu.SemaphoreType.DMA((1,))],
        compiler_params=_cparams(("parallel", "arbitrary")),
        name="ffn",
    )(x, g.reshape(1, d), w1, w3, w2)


def _headnorm_kernel(x_ref, g_ref, o_ref, *, scale):
    o_ref[...] = (_row_rms(x_ref[...], g_ref[...]) * scale).astype(o_ref.dtype)


def headnorm_bf16(x, col0, nheads, g, scale=1.0):
    t = x.shape[0]
    tr = _tile(t, 1024)
    c0 = col0 // HEAD_DIM
    return pl.pallas_call(
        functools.partial(_headnorm_kernel, scale=scale),
        grid=(t // tr, nheads),
        in_specs=[pl.BlockSpec((tr, HEAD_DIM), lambda i, j: (i, c0 + j)),
                  pl.BlockSpec((1, HEAD_DIM), lambda i, j: (0, 0))],
        out_specs=pl.BlockSpec((tr, HEAD_DIM), lambda i, j: (i, j)),
        out_shape=jax.ShapeDtypeStruct((t, nheads * HEAD_DIM), BF16),
        compiler_params=_cparams(("parallel", "parallel")),
        name="headnorm",
    )(x, g.reshape(1, HEAD_DIM))


def _vt_kernel(x_ref, o_ref):
    o_ref[...] = x_ref[...].T.astype(o_ref.dtype)


def values_transposed(x, col0, b, l):
    tr = _tile(l, 512)
    nl = l // tr
    c0 = col0 // HEAD_DIM
    return pl.pallas_call(
        _vt_kernel,
        grid=(b, NSA_KV_HEADS, nl),
        in_specs=[pl.BlockSpec((tr, HEAD_DIM), lambda bi, h, i: (bi * nl + i, c0 + h))],
        out_specs=pl.BlockSpec((None, None, HEAD_DIM, tr), lambda bi, h, i: (bi, h, 0, i)),
        out_shape=jax.ShapeDtypeStruct((b, NSA_KV_HEADS, HEAD_DIM, l), BF16),
        compiler_params=_cparams(("parallel", "parallel", "parallel")),
        name="values_t",
    )(x)


def _compress_kernel(x_ref, pe_ref, w1_ref, b1_ref, w2_ref, b2_ref, kn_ref, o_ref, xs_ref, *, l, is_k):
    nc = l // CMP_STRIDE
    xs_ref[pl.ds(0, l), :] = x_ref[...]
    xs_ref[pl.ds(l, CMP_BLOCK), :] = jnp.zeros((CMP_BLOCK, HEAD_DIM), F32)
    acc = jnp.zeros((nc, CMP_HIDDEN), F32)
    for r in range(CMP_BLOCK):
        xr = xs_ref[pl.ds(r, nc, stride=CMP_STRIDE), :] + pe_ref[pl.ds(r, 1), :]
        acc = acc + _dot(xr.astype(BF16), w1_ref[r])
    hid = _gelu_tanh(acc + b1_ref[...])
    c = _dot(hid.astype(BF16), w2_ref[...]) + b2_ref[...]
    if is_k:
        o_ref[...] = _row_rms(c, kn_ref[...]).astype(o_ref.dtype)
    else:
        o_ref[...] = c.T.astype(o_ref.dtype)


def compress_prompt(kv, which, b, l, pe, w1, b1, w2, b2, kn):
    nc = l // CMP_STRIDE
    is_k = which == 0
    oshape = (b, NSA_KV_HEADS, nc, HEAD_DIM) if is_k else (b, NSA_KV_HEADS, HEAD_DIM, nc)
    oblock = (None, None, nc, HEAD_DIM) if is_k else (None, None, HEAD_DIM, nc)
    return pl.pallas_call(
        functools.partial(_compress_kernel, l=l, is_k=is_k),
        grid=(b, NSA_KV_HEADS),
        in_specs=[
            pl.BlockSpec((l, HEAD_DIM), lambda bi, h: (bi, which * NSA_KV_HEADS + h)),
            pl.BlockSpec((CMP_BLOCK, HEAD_DIM), lambda bi, h: (0, 0)),
            pl.BlockSpec((CMP_BLOCK, HEAD_DIM, CMP_HIDDEN), lambda bi, h: (0, 0, 0)),
            pl.BlockSpec((1, CMP_HIDDEN), lambda bi, h: (0, 0)),
            pl.BlockSpec((CMP_HIDDEN, HEAD_DIM), lambda bi, h: (0, 0)),
            pl.BlockSpec((1, HEAD_DIM), lambda bi, h: (0, 0)),
            pl.BlockSpec((1, HEAD_DIM), lambda bi, h: (0, 0)),
        ],
        out_specs=pl.BlockSpec(oblock, lambda bi, h: (bi, h, 0, 0)),
        out_shape=jax.ShapeDtypeStruct(oshape, BF16),
        scratch_shapes=[pltpu.VMEM((l + CMP_BLOCK, HEAD_DIM), F32)],
        compiler_params=_cparams(("parallel", "parallel")),
        name="compress_prompt",
    )(kv, pe, w1, b1.reshape(1, -1), w2, b2.reshape(1, -1), kn.reshape(1, -1))


def _select_topk(score, j, ntop):
    ns = score.shape[0]
    bias = jnp.full(score.shape, NEG, F32)
    for _ in range(ntop):
        mx = jnp.max(score, axis=0, keepdims=True)
        idx = jnp.min(jnp.where(score == mx, j, ns), axis=0, keepdims=True)
        hit = j == idx
        bias = jnp.where(hit, 0.0, bias)
        score = jnp.where(hit, -jnp.inf, score)
    return bias


def _cmp_attn_kernel(slope_ref, q_ref, kc_ref, vct_ref, mt_ref, o_ref, sel_ref, *, tq, nc, ns):
    kvh = pl.program_id(1)
    qi = pl.program_id(2)
    qp = qi * tq + lax.broadcasted_iota(jnp.int32, (nc, tq), 1)
    blk_end = lax.broadcasted_iota(jnp.int32, (nc, tq), 0) * CMP_STRIDE + (CMP_BLOCK - 1)
    dist = qp - blk_end
    mask = dist >= 0
    distf = dist.astype(F32)
    kc = kc_ref[...]
    vct = vct_ref[...]
    imp = jnp.zeros((nc, tq), F32)
    for g in range(NSA_GROUP):
        s = _dot_nt(kc, q_ref[:, g * HEAD_DIM:(g + 1) * HEAD_DIM])
        s = jnp.where(mask, s - slope_ref[kvh * NSA_GROUP + g] * distf, -jnp.inf)
        m = jnp.max(s, axis=0, keepdims=True)
        m = jnp.where(m == -jnp.inf, 0.0, m)
        e = jnp.where(mask, jnp.exp(s - m), 0.0)
        p = e / jnp.maximum(jnp.sum(e, axis=0, keepdims=True), 1e-30)
        imp = imp + p
        o_ref[:, g * HEAD_DIM:(g + 1) * HEAD_DIM] = _dot(vct, p.astype(BF16)).T
    score = jnp.dot(mt_ref[...], imp, preferred_element_type=F32, precision=lax.Precision.HIGHEST)
    j = lax.broadcasted_iota(jnp.int32, (ns, tq), 0)
    cur = (qi * tq + lax.broadcasted_iota(jnp.int32, (ns, tq), 1)) // SLC_BLOCK
    forced = (j == 0) | (j == cur) | (j == cur - 1)
    score = jnp.where(j <= cur, jnp.where(forced, BIG, score), -BIG)
    sel_ref[...] = _select_topk(score, j, min(SLC_TOPK, ns))


def _overlap_matrix(n_cmp, n_slc):
    cs = np.arange(n_cmp)[:, None] * CMP_STRIDE
    ss = np.arange(n_slc)[None, :] * SLC_BLOCK
    ov = np.minimum(cs + CMP_BLOCK, ss + SLC_BLOCK) - np.maximum(cs, ss)
    return (np.maximum(ov, 0).astype(np.float32) / CMP_BLOCK)


def cmp_attention_prompt(qn, kc, vct, slopes, b, l):
    nc = l // CMP_STRIDE
    ns = l // SLC_BLOCK
    tq = _tile(l, 512)
    nq = l // tq
    mt = jnp.asarray(_overlap_matrix(nc, ns).T)
    return pl.pallas_call(
        functools.partial(_cmp_attn_kernel, tq=tq, nc=nc, ns=ns),
        grid=(b, NSA_KV_HEADS, nq),
        in_specs=[
            pl.BlockSpec(memory_space=pltpu.SMEM),
            pl.BlockSpec((tq, GQ_W), lambda bi, h, i: (bi * nq + i, h)),
            pl.BlockSpec((None, None, nc, HEAD_DIM), lambda bi, h, i: (bi, h, 0, 0)),
            pl.BlockSpec((None, None, HEAD_DIM, nc), lambda bi, h, i: (bi, h, 0, 0)),
            pl.BlockSpec((ns, nc), lambda bi, h, i: (0, 0)),
        ],
        out_specs=[
            pl.BlockSpec((tq, GQ_W), lambda bi, h, i: (bi * nq + i, h)),
            pl.BlockSpec((None, None, ns, tq), lambda bi, h, i: (bi, h, 0, i)),
        ],
        out_shape=[jax.ShapeDtypeStruct((b * l, NSA_HEADS * HEAD_DIM), F32),
                   jax.ShapeDtypeStruct((b, NSA_KV_HEADS, ns, l), F32)],
        compiler_params=_cparams(("parallel", "parallel", "parallel")),
        name="cmp_attention",
    )(slopes, qn, kc, vct, mt)


def _flash_kernel(slope_ref, q_ref, k_ref, vt_ref, *rest, tq, tk, nkt, windowed):
    if windowed:
        o_ref, m_ref, l_ref, acc_ref = rest
        sel_ref = None
    else:
        sel_ref, o_ref, m_ref, l_ref, acc_ref = rest
    kvh = pl.program_id(1)
    qi = pl.program_id(2)
    kt = pl.program_id(3)
    if windowed:
        ktile = qi - (nkt - 1) + kt
        valid = ktile >= 0
    else:
        ktile = kt
        valid = kt * tk <= qi * tq + (tq - 1)

    @pl.when(kt == 0)
    def _():
        m_ref[...] = jnp.full(m_ref.shape, NEG, F32)
        l_ref[...] = jnp.zeros(l_ref.shape, F32)
        acc_ref[...] = jnp.zeros(acc_ref.shape, F32)

    @pl.when(valid)
    def _():
        kpos = ktile * tk + lax.broadcasted_iota(jnp.int32, (tk, tq), 0)
        qpos = qi * tq + lax.broadcasted_iota(jnp.int32, (tk, tq), 1)
        dist = qpos - kpos
        distf = dist.astype(F32)
        if windowed:
            bias = jnp.where((dist >= 0) & (dist < WINDOW), 0.0, NEG)
        else:
            sel = sel_ref[...]
            sel = jnp.concatenate(
                [jnp.broadcast_to(sel[r:r + 1, :], (SLC_BLOCK, tq)) for r in range(tk // SLC_BLOCK)], axis=0)
            bias = jnp.where(dist >= 0, sel, NEG)
        k = k_ref[...]
        vt = vt_ref[...]
        for g in range(NSA_GROUP):
            s = _dot_nt(k, q_ref[:, g * HEAD_DIM:(g + 1) * HEAD_DIM])
            s = s - slope_ref[kvh * NSA_GROUP + g] * distf + bias
            m_prev = m_ref[pl.ds(g, 1), :]
            m_new = jnp.maximum(m_prev, jnp.max(s, axis=0, keepdims=True))
            alpha = jnp.exp(m_prev - m_new)
            e = jnp.exp(s - m_new)
            l_ref[pl.ds(g, 1), :] = alpha * l_ref[pl.ds(g, 1), :] + jnp.sum(e, axis=0, keepdims=True)
            acc_ref[g] = alpha * acc_ref[g] + _dot(vt, e.astype(BF16))
            m_ref[pl.ds(g, 1), :] = m_new

    @pl.when(kt == nkt - 1)
    def _():
        for g in range(NSA_GROUP):
            o = acc_ref[g] / jnp.maximum(l_ref[pl.ds(g, 1), :], 1e-30)
            o_ref[:, g * HEAD_DIM:(g + 1) * HEAD_DIM] = o.T


def flash_prompt(qn, kn, vt, slopes, b, l, sel=None):
    windowed = sel is None
    tq = _tile(l, 512)
    tk = tq
    nq = l // tq
    nkt = (min(WINDOW, l) // tk + 1) if windowed else l // tk
    if windowed:
        kmap = lambda qi, kt: jnp.maximum(qi - (nkt - 1) + kt, 0)
    else:
        kmap = lambda qi, kt: jnp.minimum(kt, (qi * tq + tq - 1) // tk)
    in_specs = [
        pl.BlockSpec(memory_space=pltpu.SMEM),
        pl.BlockSpec((tq, GQ_W), lambda bi, h, qi, kt: (bi * nq + qi, h)),
        pl.BlockSpec((tk, HEAD_DIM), lambda bi, h, qi, kt: (bi * nq + kmap(qi, kt), h)),
        pl.BlockSpec((None, None, HEAD_DIM, tk), lambda bi, h, qi, kt: (bi, h, 0, kmap(qi, kt))),
    ]
    args = [slopes, qn, kn, vt]
    if not windowed:
        in_specs.append(pl.BlockSpec((None, None, tk // SLC_BLOCK, tq),
                                     lambda bi, h, qi, kt: (bi, h, kmap(qi, kt), qi)))
        args.append(sel)
    return pl.pallas_call(
        functools.partial(_flash_kernel, tq=tq, tk=tk, nkt=nkt, windowed=windowed),
        grid=(b, NSA_KV_HEADS, nq, nkt),
        in_specs=in_specs,
        out_specs=pl.BlockSpec((tq, GQ_W), lambda bi, h, qi, kt: (bi * nq + qi, h)),
        out_shape=jax.ShapeDtypeStruct((b * l, NSA_HEADS * HEAD_DIM), F32),
        scratch_shapes=[pltpu.VMEM((SUBLANE, tq), F32), pltpu.VMEM((SUBLANE, tq), F32),
                        pltpu.VMEM((NSA_GROUP, HEAD_DIM, tq), F32)],
        compiler_params=_cparams(("parallel", "parallel", "parallel", "arbitrary")),
        name="window_attention" if windowed else "selected_attention",
    )(*args)


def _combine_kernel(g_ref, c_ref, s_ref, w_ref, o_ref):
    gates = _sigmoid(g_ref[...])
    for h in range(NSA_HEADS):
        cols = slice(h * HEAD_DIM, (h + 1) * HEAD_DIM)
        o = (gates[:, h:h + 1] * c_ref[:, cols]
             + gates[:, NSA_HEADS + h:NSA_HEADS + h + 1] * s_ref[:, cols]
             + gates[:, 2 * NSA_HEADS + h:2 * NSA_HEADS + h + 1] * w_ref[:, cols])
        o_ref[:, cols] = o.astype(o_ref.dtype)


def combine_branches(gates, o_cmp, o_slc, o_win):
    t, w = o_cmp.shape
    tr = _tile(t, 256)
    spec = pl.BlockSpec((tr, w), lambda i: (i, 0))
    return pl.pallas_call(
        _combine_kernel,
        grid=(t // tr,),
        in_specs=[pl.BlockSpec((tr, gates.shape[1]), lambda i: (i, 0)), spec, spec, spec],
        out_specs=spec,
        out_shape=jax.ShapeDtypeStruct((t, w), BF16),
        compiler_params=_cparams(("parallel",)),
        name="combine",
    )(gates, o_cmp, o_slc, o_win)


def _group_norm_gate(o, rg, gn):
    mu = jnp.mean(o, axis=-1, keepdims=True)
    var = jnp.mean(jnp.square(o - mu), axis=-1, keepdims=True)
    return _silu(rg) * ((o - mu) * lax.rsqrt(var + EPS) * gn)


def _retention_kernel(q_ref, k_ref, v_ref, rg_ref, intra_ref, qd_ref, kd_ref, cd_ref, gn_ref,
                      o_ref, s_ref, st_ref, *, tl, hb):
    li = pl.program_id(2)
    nl = pl.num_programs(2)

    @pl.when(li == 0)
    def _():
        st_ref[...] = jnp.zeros(st_ref.shape, F32)

    def chunk(c, carry):
        rows = pl.ds(pl.multiple_of(c * RET_CHUNK, RET_CHUNK), RET_CHUNK)
        for h in range(hb):
            cols = slice(h * HEAD_DIM, (h + 1) * HEAD_DIM)
            qc = q_ref[rows, cols]
            kc = k_ref[rows, cols] * (HEAD_DIM ** -0.5)
            vc = v_ref[rows, cols].astype(BF16)
            state = st_ref[h]
            att = _dot_nt(qc.astype(BF16), kc.astype(BF16)) * intra_ref[h]
            o = _dot(att.astype(BF16), vc) + _dot((qc * qd_ref[h]).astype(BF16), state.astype(BF16))
            st_ref[h] = state * cd_ref[h] + _dot((kc * kd_ref[h]).T.astype(BF16), vc)
            o_ref[rows, cols] = _group_norm_gate(o, rg_ref[rows, cols], gn_ref[:, cols]).astype(o_ref.dtype)
        return carry

    lax.fori_loop(0, tl // RET_CHUNK, chunk, 0)

    @pl.when(li == nl - 1)
    def _():
        s_ref[...] = st_ref[...]


def _retention_tables(chunk):
    h = jnp.arange(RET_HEADS, dtype=F32)
    log_g = jnp.log1p(-jnp.exp2(-5.0 - h))
    i = jnp.arange(chunk, dtype=F32)
    diff = i[:, None] - i[None, :]
    intra = jnp.where(diff >= 0, jnp.exp(jnp.maximum(diff, 0.0)[None] * log_g[:, None, None]), 0.0)
    q_dec = jnp.exp((i[None, :] + 1.0) * log_g[:, None])
    k_dec = jnp.exp((chunk - 1.0 - i)[None, :] * log_g[:, None])
    c_dec = jnp.exp(chunk * log_g)
    return intra, q_dec, k_dec, c_dec


def retention_prompt(r, gn, b, l):
    intra, q_dec, k_dec, c_dec = _retention_tables(RET_CHUNK)
    bc = lambda t: jnp.broadcast_to(t[:, :, None], (RET_HEADS, RET_CHUNK, HEAD_DIM))
    qd, kd = bc(q_dec), bc(k_dec)
    cd = jnp.broadcast_to(c_dec[:, None, None], (RET_HEADS, HEAD_DIM, HEAD_DIM))
    hb = 4
    ng = RET_HEADS // hb
    tl = _tile(l, 1024)
    nl = l // tl
    w = hb * HEAD_DIM
    rspec = lambda part: pl.BlockSpec((tl, w), lambda bi, g, li: (bi * nl + li, part * ng + g))
    tspec = pl.BlockSpec((hb, RET_CHUNK, HEAD_DIM), lambda bi, g, li: (g, 0, 0))
    return pl.pallas_call(
        functools.partial(_retention_kernel, tl=tl, hb=hb),
        grid=(b, ng, nl),
        in_specs=[rspec(0), rspec(1), rspec(2), rspec(3), tspec, tspec, tspec, tspec,
                  pl.BlockSpec((1, w), lambda bi, g, li: (0, g))],
        out_specs=[pl.BlockSpec((tl, w), lambda bi, g, li: (bi * nl + li, g)),
                   pl.BlockSpec((None, hb, HEAD_DIM, HEAD_DIM), lambda bi, g, li: (bi, g, 0, 0))],
        out_shape=[jax.ShapeDtypeStruct((b * l, RET_HEADS * HEAD_DIM), BF16),
                   jax.ShapeDtypeStruct((b, RET_HEADS, HEAD_DIM, HEAD_DIM), F32)],
        scratch_shapes=[pltpu.VMEM((hb, HEAD_DIM, HEAD_DIM), F32)],
        compiler_params=_cparams(("parallel", "parallel", "arbitrary")),
        name="retention_prompt",
    )(r, r, r, r, intra, qd, kd, cd, gn.reshape(1, -1))


def _group_rows(kvh):
    row = lax.broadcasted_iota(jnp.int32, (NSA_HEADS, 1), 0)
    return (row >= kvh * NSA_GROUP) & (row < (kvh + 1) * NSA_GROUP)


def _sample_cmp_kernel(pt_ref, *refs, pp, npg, pos0):
    pages = refs[:pp]
    (q_ref, slope_ref, pea_ref, peb_ref, w1a_ref, w1b_ref, b1_ref, w2_ref, b2_ref, kn_ref, m_ref,
     o_ref, idx_ref, x_ref, f_ref, s_ref) = refs[pp:]
    jp = pl.program_id(1)
    cpp = PAGE_ROWS // CMP_STRIDE
    nck = npg * cpp

    @pl.when(jp == 0)
    def _():
        s_ref[...] = jnp.zeros(s_ref.shape, F32)

    for t in range(pp):
        for w in range(2):
            for kvh in range(NSA_KV_HEADS):
                for c in range(CMP_STRIDE):
                    x_ref[w, pl.ds((t * NSA_KV_HEADS + kvh) * cpp, cpp), pl.ds(c * HEAD_DIM, HEAD_DIM)] = (
                        pages[t][pl.ds(c, cpp, stride=CMP_STRIDE), w * NSA_KV_HEADS + kvh, :])
    for w in range(2):
        x = x_ref[w]
        fa = _dot((x + pea_ref[w]).astype(BF16), w1a_ref[w])
        sb = _dot((x + peb_ref[w]).astype(BF16), w1b_ref[w])
        for t in range(pp):
            for kvh in range(NSA_KV_HEADS):
                src = slice((t * NSA_KV_HEADS + kvh) * cpp, (t * NSA_KV_HEADS + kvh + 1) * cpp)
                dst = pl.ds(pl.multiple_of((jp * pp + t) * cpp, cpp), cpp)
                f_ref[w, kvh, dst, :] = fa[src]
                s_ref[w, kvh, dst, :] = sb[src]

    @pl.when(jp == pl.num_programs(1) - 1)
    def _():
        q = q_ref[...]
        n_io = lax.broadcasted_iota(jnp.int32, (NSA_HEADS, nck), 1)
        dist = pos0 - (n_io * CMP_STRIDE + (CMP_BLOCK - 1))
        mask = dist >= 0
        bias = slope_ref[:, :nck] * dist.astype(F32)
        o_acc = jnp.zeros((NSA_HEADS, HEAD_DIM), F32)
        imp_rows = []
        for kvh in range(NSA_KV_HEADS):
            def cblock(w):
                hid = _gelu_tanh(f_ref[w, kvh, pl.ds(0, nck), :] + s_ref[w, kvh, pl.ds(1, nck), :] + b1_ref[w])
                return _dot(hid.astype(BF16), w2_ref[w]) + b2_ref[w]
            kc = _row_rms(cblock(0), kn_ref[...]).astype(BF16)
            vc = cblock(1).astype(BF16)
            s = jnp.where(mask, _dot_nt(q, kc) - bias, -jnp.inf)
            m = jnp.max(s, axis=1, keepdims=True)
            m = jnp.where(m == -jnp.inf, 0.0, m)
            e = jnp.where(mask, jnp.exp(s - m), 0.0)
            p = e / jnp.maximum(jnp.sum(e, axis=1, keepdims=True), 1e-30)
            grp = _group_rows(kvh)
            o_acc = o_acc + jnp.where(grp, _dot(p.astype(BF16), vc), 0.0)
            imp_rows.append(jnp.sum(jnp.where(grp, p, 0.0), axis=0, keepdims=True))
        o_ref[...] = o_acc
        imp = jnp.concatenate(imp_rows + [jnp.zeros((SUBLANE - NSA_KV_HEADS, nck), F32)], axis=0)
        score = jnp.dot(imp, m_ref[...], preferred_element_type=F32, precision=lax.Precision.HIGHEST)
        nsp = score.shape[1]
        j = lax.broadcasted_iota(jnp.int32, (SUBLANE, nsp), 1)
        cur = pos0 // SLC_BLOCK
        forced = (j == 0) | (j == cur) | (j == cur - 1)
        score = jnp.where(j <= cur, jnp.where(forced, BIG, score), -BIG)
        lane = lax.broadcasted_iota(jnp.int32, (SUBLANE, LANE), 1)
        out = jnp.zeros((SUBLANE, LANE), jnp.int32)
        for t in range(SLC_TOPK):
            mx = jnp.max(score, axis=1, keepdims=True)
            idx = jnp.min(jnp.where(score == mx, j, nsp), axis=1, keepdims=True)
            out = jnp.where(lane == t, idx, out)
            score = jnp.where(j == idx, -jnp.inf, score)
        idx_ref[...] = out


PAGE_ROWS = 128


def sample_cmp_select(cache4, page_table, qn3, slopes_b, pe, w1, b1, w2, b2, kn, pos0):
    db, npg = page_table.shape
    pp = _tile(npg, 8)
    cpp = PAGE_ROWS // CMP_STRIDE
    nck = npg * cpp
    ng = 2 * NSA_KV_HEADS
    n_slc = (pos0 + 1 + SLC_BLOCK - 1) // SLC_BLOCK
    nsp = -(-n_slc // LANE) * LANE
    m = np.zeros((nck, nsp), np.float32)
    m[:, :n_slc] = _overlap_matrix(nck, n_slc)
    flat = lambda a: a.reshape(2, 1, CMP_STRIDE * HEAD_DIM)
    pea, peb = flat(pe[:, :CMP_STRIDE]), flat(pe[:, CMP_STRIDE:])
    w1a = w1[:, :CMP_STRIDE].reshape(2, CMP_STRIDE * HEAD_DIM, CMP_HIDDEN)
    w1b = w1[:, CMP_STRIDE:].reshape(2, CMP_STRIDE * HEAD_DIM, CMP_HIDDEN)
    const = lambda shape: pl.BlockSpec(shape, lambda bi, jp, pt: (0,) * len(shape))
    page_specs = [
        pl.BlockSpec((None, PAGE_ROWS, ng, HEAD_DIM),
                     functools.partial(lambda bi, jp, pt, t: (pt[bi, jp * pp + t], 0, 0, 0), t=t))
        for t in range(pp)]
    grid_spec = pltpu.PrefetchScalarGridSpec(
        num_scalar_prefetch=1,
        grid=(db, npg // pp),
        in_specs=page_specs + [
            pl.BlockSpec((None, NSA_HEADS, HEAD_DIM), lambda bi, jp, pt: (bi, 0, 0)),
            const((NSA_HEADS, slopes_b.shape[1])),
            const((2, 1, CMP_STRIDE * HEAD_DIM)), const((2, 1, CMP_STRIDE * HEAD_DIM)),
            const((2, CMP_STRIDE * HEAD_DIM, CMP_HIDDEN)), const((2, CMP_STRIDE * HEAD_DIM, CMP_HIDDEN)),
            const((2, 1, CMP_HIDDEN)), const((2, CMP_HIDDEN, HEAD_DIM)), const((2, 1, HEAD_DIM)),
            const((1, HEAD_DIM)), const((nck, nsp)),
        ],
        out_specs=[pl.BlockSpec((None, NSA_HEADS, HEAD_DIM), lambda bi, jp, pt: (bi, 0, 0)),
                   pl.BlockSpec((None, SUBLANE, LANE), lambda bi, jp, pt: (bi, 0, 0))],
        scratch_shapes=[
            pltpu.VMEM((2, pp * NSA_KV_HEADS * cpp, CMP_STRIDE * HEAD_DIM), F32),
            pltpu.VMEM((2, NSA_KV_HEADS, nck, CMP_HIDDEN), F32),
            pltpu.VMEM((2, NSA_KV_HEADS, nck + SUBLANE, CMP_HIDDEN), F32),
        ],
    )
    return pl.pallas_call(
        functools.partial(_sample_cmp_kernel, pp=pp, npg=npg, pos0=pos0),
        grid_spec=grid_spec,
        out_shape=[jax.ShapeDtypeStruct((db, NSA_HEADS, HEAD_DIM), F32),
                   jax.ShapeDtypeStruct((db, SUBLANE, LANE), jnp.int32)],
        compiler_params=_cparams(("parallel", "arbitrary")),
        name="sample_cmp_select",
    )(page_table, *([cache4] * pp), qn3, slopes_b, pea, peb, w1a, w1b, b1.reshape(2, 1, -1), w2,
      b2.reshape(2, 1, -1), kn.reshape(1, -1), jnp.asarray(m))


def _softmax_with_new_key(s, s_new):
    m = jnp.maximum(jnp.max(s, axis=1, keepdims=True), s_new)
    e = jnp.exp(s - m)
    e_new = jnp.exp(s_new - m)
    return e, e_new, jnp.maximum(jnp.sum(e, axis=1, keepdims=True) + e_new, 1e-30)


def _bf16_round(x):
    return x.astype(BF16).astype(F32)


def _sample_slc_kernel(row_ref, cache_ref, q_ref, kpos_ref, slope_ref, new_ref, kn_ref, o_ref, buf, sem, *, pos0):
    bi = pl.program_id(0)
    nb = pl.num_programs(0)
    nsel = SLC_TOPK * SLC_BLOCK
    ng = 2 * NSA_KV_HEADS

    def gather(b, slot, start):
        for kvh in range(NSA_KV_HEADS):
            for t in range(SLC_TOPK):
                row0 = row_ref[(b * NSA_KV_HEADS + kvh) * SLC_TOPK + t]
                cp = pltpu.make_async_copy(cache_ref.at[pl.ds(row0, SLC_BLOCK), pl.ds(ng, ng), :],
                                           buf.at[slot, kvh, pl.ds(t * SLC_BLOCK, SLC_BLOCK)], sem.at[slot])
                if start:
                    cp.start()
                else:
                    cp.wait()

    slot = bi % 2

    @pl.when(bi == 0)
    def _():
        gather(bi, 0, True)

    @pl.when(bi + 1 < nb)
    def _():
        gather(bi + 1, 1 - slot, True)

    gather(bi, slot, False)

    q = q_ref[...]
    o_acc = jnp.zeros((NSA_HEADS, HEAD_DIM), F32)
    for kvh in range(NSA_KV_HEADS):
        k = buf[slot, kvh, :, kvh, :]
        v = buf[slot, kvh, :, NSA_KV_HEADS + kvh, :]
        kn = _row_rms(k, kn_ref[...]).astype(BF16)
        dist = pos0 - kpos_ref[pl.ds(kvh, 1), :]
        s = _dot_nt(q, kn) - slope_ref[:, :nsel] * dist.astype(F32)
        s = jnp.where(dist >= 0, s, NEG)
        knew = _row_rms(new_ref[pl.ds(kvh, 1), :], kn_ref[...])
        s_new = jnp.sum(q.astype(F32) * _bf16_round(knew), axis=1, keepdims=True)
        e, e_new, l = _softmax_with_new_key(s, s_new)
        vnew = new_ref[pl.ds(NSA_KV_HEADS + kvh, 1), :]
        o = _dot(e.astype(BF16), v.astype(BF16)) + _bf16_round(e_new) * _bf16_round(vnew)
        o_acc = o_acc + jnp.where(_group_rows(kvh), o / l, 0.0)
    o_ref[...] = o_acc


def sample_slc_attention(cache3, row0, qn3, kpos, slopes_b, new_kv, kn, pos0):
    db = qn3.shape[0]
    nsel = SLC_TOPK * SLC_BLOCK
    ng = 2 * NSA_KV_HEADS
    grid_spec = pltpu.PrefetchScalarGridSpec(
        num_scalar_prefetch=1,
        grid=(db,),
        in_specs=[
            pl.BlockSpec(memory_space=pl.ANY),
            pl.BlockSpec((None, NSA_HEADS, HEAD_DIM), lambda bi, r: (bi, 0, 0)),
            pl.BlockSpec((None, NSA_KV_HEADS, nsel), lambda bi, r: (bi, 0, 0)),
            pl.BlockSpec((NSA_HEADS, slopes_b.shape[1]), lambda bi, r: (0, 0)),
            pl.BlockSpec((None, ng, HEAD_DIM), lambda bi, r: (bi, 0, 0)),
            pl.BlockSpec((1, HEAD_DIM), lambda bi, r: (0, 0)),
        ],
        out_specs=pl.BlockSpec((None, NSA_HEADS, HEAD_DIM), lambda bi, r: (bi, 0, 0)),
        scratch_shapes=[pltpu.VMEM((2, NSA_KV_HEADS, nsel, ng, HEAD_DIM), F32),
                        pltpu.SemaphoreType.DMA((2,))],
    )
    return pl.pallas_call(
        functools.partial(_sample_slc_kernel, pos0=pos0),
        grid_spec=grid_spec,
        out_shape=jax.ShapeDtypeStruct((db, NSA_HEADS, HEAD_DIM), F32),
        compiler_params=_cparams(("arbitrary",)),
        name="sample_slc_attention",
    )(row0, cache3, qn3, kpos, slopes_b, new_kv, kn.reshape(1, -1))


def _sample_win_kernel(sw_ref, q_ref, slope_ref, new_ref, kn_ref, g_ref, oc_ref, os_ref, o_ref, nw_ref, *, wb):
    q = q_ref[...]
    row = lax.broadcasted_iota(jnp.int32, (NSA_HEADS, wb), 1)
    dist = wb - row
    valid = dist < WINDOW
    bias = slope_ref[:, :wb] * dist.astype(F32)
    o_acc = jnp.zeros((NSA_HEADS, HEAD_DIM), F32)
    for kvh in range(NSA_KV_HEADS):
        k = sw_ref[:, kvh, :]
        v = sw_ref[:, NSA_KV_HEADS + kvh, :]
        kn = _row_rms(k, kn_ref[...]).astype(BF16)
        s = jnp.where(valid, _dot_nt(q, kn) - bias, NEG)
        knew = _row_rms(new_ref[pl.ds(kvh, 1), :], kn_ref[...])
        s_new = jnp.sum(q.astype(F32) * _bf16_round(knew), axis=1, keepdims=True)
        e, e_new, l = _softmax_with_new_key(s, s_new)
        vnew = new_ref[pl.ds(NSA_KV_HEADS + kvh, 1), :]
        o = _dot(e.astype(BF16), v.astype(BF16)) + _bf16_round(e_new) * _bf16_round(vnew)
        o_acc = o_acc + jnp.where(_group_rows(kvh), o / l, 0.0)
    g = _sigmoid(g_ref[...])
    o_ref[...] = (g[0] * oc_ref[...] + g[1] * os_ref[...] + g[2] * o_acc).astype(o_ref.dtype)
    nw_ref[pl.ds(0, wb - 1)] = sw_ref[pl.ds(1, wb - 1)]
    nw_ref[wb - 1] = new_ref[...]


def sample_win_attention(sw, qn3, slopes_b, win_new, kn, gates_b, o_cmp, o_slc):
    db, wb, ng, _ = sw.shape
    hspec = pl.BlockSpec((None, NSA_HEADS, HEAD_DIM), lambda bi: (bi, 0, 0))
    wspec = pl.BlockSpec((None, wb, ng, HEAD_DIM), lambda bi: (bi, 0, 0, 0))
    return pl.pallas_call(
        functools.partial(_sample_win_kernel, wb=wb),
        grid=(db,),
        in_specs=[
            wspec,
            hspec,
            pl.BlockSpec((NSA_HEADS, slopes_b.shape[1]), lambda bi: (0, 0)),
            pl.BlockSpec((None, ng, HEAD_DIM), lambda bi: (bi, 0, 0)),
            pl.BlockSpec((1, HEAD_DIM), lambda bi: (0, 0)),
            pl.BlockSpec((None, 3, NSA_HEADS, HEAD_DIM), lambda bi: (bi, 0, 0, 0)),
            hspec, hspec,
        ],
        out_specs=[hspec, wspec],
        out_shape=[jax.ShapeDtypeStruct((db, NSA_HEADS, HEAD_DIM), BF16),
                   jax.ShapeDtypeStruct((db, wb, ng, HEAD_DIM), F32)],
        compiler_params=_cparams(("parallel",)),
        name="sample_win_attention",
    )(sw, qn3, slopes_b, win_new, kn.reshape(1, -1), gates_b, o_cmp, o_slc)


def _sample_ret_kernel(st_ref, r_ref, gam_ref, gn_ref, o_ref, s_ref):
    q = r_ref[0]
    k = r_ref[1] * (HEAD_DIM ** -0.5)
    v = r_ref[2]
    rg = r_ref[3]
    gam = gam_ref[...]
    qb, kb, vb = _bf16_round(q), _bf16_round(k), _bf16_round(v)
    att = jnp.sum(qb * kb, axis=1, keepdims=True)
    kt = kb.T
    qd = (q * gam).astype(BF16)
    rows = []
    for h in range(RET_HEADS):
        state = st_ref[h]
        rows.append(_dot(qd[h:h + 1, :], state.astype(BF16)))
        s_ref[h] = state * gam[h:h + 1, :] + kt[:, h:h + 1] * vb[h:h + 1, :]
    o = _bf16_round(att) * vb + jnp.concatenate(rows, axis=0)
    o_ref[...] = _group_norm_gate(o, rg, gn_ref[...]).astype(o_ref.dtype)


def sample_retention(state, r4, gn):
    db = state.shape[0]
    _, q_dec, _, _ = _retention_tables(1)
    gam = jnp.broadcast_to(q_dec, (RET_HEADS, HEAD_DIM))
    return pl.pallas_call(
        _sample_ret_kernel,
        grid=(db,),
        in_specs=[
            pl.BlockSpec((None, RET_HEADS, HEAD_DIM, HEAD_DIM), lambda bi: (bi, 0, 0, 0)),
            pl.BlockSpec((None, 4, RET_HEADS, HEAD_DIM), lambda bi: (bi, 0, 0, 0)),
            pl.BlockSpec((RET_HEADS, HEAD_DIM), lambda bi: (0, 0)),
            pl.BlockSpec((RET_HEADS, HEAD_DIM), lambda bi: (0, 0)),
        ],
        out_specs=[pl.BlockSpec((None, RET_HEADS, HEAD_DIM), lambda bi: (bi, 0, 0)),
                   pl.BlockSpec((None, RET_HEADS, HEAD_DIM, HEAD_DIM), lambda bi: (bi, 0, 0, 0))],
        out_shape=[jax.ShapeDtypeStruct((db, RET_HEADS, HEAD_DIM), BF16),
                   jax.ShapeDtypeStruct((db, RET_HEADS, HEAD_DIM, HEAD_DIM), F32)],
        compiler_params=_cparams(("parallel",)),
        name="sample_retention",
    )(state, r4, gam, gn.reshape(RET_HEADS, HEAD_DIM))


def _alibi_slopes():
    i = np.arange(NSA_HEADS, dtype=np.float32)
    return np.exp2(-8.0 * (i + 1.0) / NSA_HEADS).astype(np.float32)


def _in_proj_weights(w_in):
    nq = NSA_HEADS * HEAD_DIM
    nkv = 4 * NSA_KV_HEADS * HEAD_DIM
    nwin = 2 * NSA_KV_HEADS * HEAD_DIM
    ng = 3 * NSA_HEADS
    nr = 4 * RET_HEADS * HEAD_DIM
    o1, o2, o3, o4 = nq, nq + nkv, nq + nkv + nwin, nq + nkv + nwin + ng
    assert w_in.shape[1] == o4 + nr
    wg = jnp.pad(w_in[:, o3:o4], ((0, 0), (0, LANE - ng)))
    return [w.astype(BF16) for w in (w_in[:, :o1], w_in[:, o1:o2], w_in[:, o2:o3], wg, w_in[:, o4:])]


def kernel(x_prompt, x_sample, cache_nsa_kv, state_win_kv, state_ret, page_table, norm_g, ffn_w1, ffn_w3, ffn_w2,
           w_in, nsa_q_norm, nsa_k_norm, cmp_pe, cmp_w1, cmp_b1, cmp_w2, cmp_b2, ret_gn_g, w_out):
    assert cache_nsa_kv.shape[0] == 1 and x_sample.shape[1] == 1
    b, l, d = x_prompt.shape
    db = x_sample.shape[0]
    npg = page_table.shape[1]
    n_pool = cache_nsa_kv.shape[1]
    assert cache_nsa_kv.shape[2] == PAGE_ROWS
    pos0 = npg * PAGE_ROWS
    half = NSA_KV_HEADS * HEAD_DIM

    w1 = ffn_w1[0].astype(BF16)
    w3 = ffn_w3[0].astype(BF16)
    w2 = ffn_w2[0].astype(BF16)
    wq, wkv, wwin, wg, wr = _in_proj_weights(w_in[0])
    wo = w_out[0].astype(BF16)
    ng = norm_g[0]
    qg, kg = nsa_q_norm[0], nsa_k_norm[0]
    pe = cmp_pe[0]
    cw1 = cmp_w1[0].astype(BF16)
    cb1 = cmp_b1[0]
    cw2 = cmp_w2[0].astype(BF16)
    cb2 = cmp_b2[0]
    gn = ret_gn_g[0]
    slopes = jnp.asarray(_alibi_slopes())
    q_scale = HEAD_DIM ** -0.5

    def front(x):
        h = ffn_half_step(x, ng[0], w1[0], w3[0], w2[0])
        n = rmsnorm_bf16(h, ng[1])
        return h, [matmul(n, w) for w in (wq, wkv, wwin, wg, wr)]

    def back(h, mixed):
        h = matmul(mixed, wo, res=h)
        return ffn_half_step(h, ng[2], w1[1], w3[1], w2[1])

    hp, (q, kv, win, gates, r) = front(x_prompt.reshape(b * l, d))
    qn = headnorm_bf16(q, 0, NSA_HEADS, qg, q_scale)
    kc = compress_prompt(kv, 0, b, l, pe[0], cw1[0], cb1[0], cw2[0], cb2[0], kg[0])
    vct = compress_prompt(kv, 1, b, l, pe[1], cw1[1], cb1[1], cw2[1], cb2[1], kg[0])
    o_cmp, sel = cmp_attention_prompt(qn, kc, vct, slopes, b, l)
    k_slc = headnorm_bf16(kv, 2 * half, NSA_KV_HEADS, kg[1])
    v_slc = values_transposed(kv, 3 * half, b, l)
    o_slc = flash_prompt(qn, k_slc, v_slc, slopes, b, l, sel=sel)
    k_win = headnorm_bf16(win, 0, NSA_KV_HEADS, kg[2])
    v_win = values_transposed(win, half, b, l)
    o_win = flash_prompt(qn, k_win, v_win, slopes, b, l)
    o_nsa = combine_branches(gates, o_cmp, o_slc, o_win)
    o_ret, ret_p = retention_prompt(r, gn, b, l)
    y_prompt = back(hp, jnp.concatenate([o_nsa, o_ret], axis=1)).reshape(b, l, d)
    kv_prompt = kv.reshape(1, b, l, 4, NSA_KV_HEADS, HEAD_DIM)
    wl = min(WINDOW, l)
    win_prompt = win.reshape(b, l, 2, NSA_KV_HEADS, HEAD_DIM)[:, l - wl:][None]

    hs, (q, kv, win, gates, r) = front(x_sample.reshape(db, d))
    qn3 = headnorm_bf16(q, 0, NSA_HEADS, qg, q_scale).reshape(db, NSA_HEADS, HEAD_DIM)
    slopes_b = jnp.broadcast_to(slopes[:, None], (NSA_HEADS, max(npg * (PAGE_ROWS // CMP_STRIDE),
                                                                 SLC_TOPK * SLC_BLOCK, state_win_kv.shape[2])))
    o_cmp, idx = sample_cmp_select(cache_nsa_kv.reshape(n_pool, PAGE_ROWS, 4 * NSA_KV_HEADS, HEAD_DIM), page_table,
                                   qn3, slopes_b, pe, cw1, cb1, cw2, cb2, kg[0], pos0)
    idx = idx[:, :NSA_KV_HEADS, :SLC_TOPK]
    n_past = pos0 // SLC_BLOCK
    in_cache = idx < n_past
    pidx = jnp.minimum(idx, n_past - 1)
    per_page = PAGE_ROWS // SLC_BLOCK
    page = jnp.take_along_axis(page_table, (pidx // per_page).reshape(db, -1), axis=1).reshape(pidx.shape)
    row0 = (page * PAGE_ROWS + (pidx % per_page) * SLC_BLOCK).astype(jnp.int32).reshape(-1)
    offs = jnp.arange(SLC_BLOCK, dtype=jnp.int32)
    kpos = jnp.where(in_cache[..., None], idx[..., None] * SLC_BLOCK + offs, pos0 + 1).reshape(db, NSA_KV_HEADS, -1)
    o_slc = sample_slc_attention(cache_nsa_kv.reshape(n_pool * PAGE_ROWS, 4 * NSA_KV_HEADS, HEAD_DIM), row0, qn3, kpos,
                                 slopes_b, kv[:, 2 * half:].reshape(db, 2 * NSA_KV_HEADS, HEAD_DIM), kg[1], pos0)
    gates_b = jnp.broadcast_to(gates[:, :3 * NSA_HEADS].reshape(db, 3, NSA_HEADS, 1), (db, 3, NSA_HEADS, HEAD_DIM))
    wb = state_win_kv.shape[2]
    o_nsa, win_s = sample_win_attention(state_win_kv.reshape(db, wb, 2 * NSA_KV_HEADS, HEAD_DIM), qn3, slopes_b,
                                        win.reshape(db, 2 * NSA_KV_HEADS, HEAD_DIM), kg[2], gates_b, o_cmp, o_slc)
    o_ret, ret_s = sample_retention(state_ret[0], r.reshape(db, 4, RET_HEADS, HEAD_DIM), gn)
    mixed = jnp.concatenate([o_nsa.reshape(db, -1), o_ret.reshape(db, -1)], axis=1)
    y_sample = back(hs, mixed).reshape(db, 1, d)
    kv_sample = kv.reshape(1, db, 1, 4, NSA_KV_HEADS, HEAD_DIM)
    win_sample = win_s.reshape(1, db, wb, 2, NSA_KV_HEADS, HEAD_DIM)

    return (y_prompt, y_sample, kv_prompt, kv_sample, win_prompt, win_sample, ret_p[None], ret_s[None])
```

```python
import functools
import math

import numpy as np
import jax
import jax.numpy as jnp
from jax import lax
from jax.experimental import pallas as pl
from jax.experimental.pallas import tpu as pltpu

F32 = jnp.float32
BF16 = jnp.bfloat16

HEAD_DIM = 128
NSA_HEADS = 16
NSA_KV_HEADS = 4
NSA_GROUP = NSA_HEADS // NSA_KV_HEADS
RET_HEADS = 16
CMP_BLOCK = 32
CMP_STRIDE = 16
CMP_HIDDEN = 2 * HEAD_DIM
SLC_BLOCK = 64
SLC_TOPK = 16
WINDOW = 512
RET_CHUNK = 128
EPS = 1e-6
BIG = 1e30
NEG = -1e30
LANE = 128
SUBLANE = 8
VMEM_LIMIT = 56 * 1024 * 1024

GQ_W = NSA_GROUP * HEAD_DIM


def _cparams(sem):
    return pltpu.CompilerParams(dimension_semantics=sem, vmem_limit_bytes=VMEM_LIMIT)


def _tile(n, pref):
    if n <= pref:
        return n
    t = pref
    while n % t:
        t //= 2
    return t


def _dot(a, b):
    return jnp.dot(a, b, preferred_element_type=F32)


def _dot_nt(a, b):
    return lax.dot_general(a, b, (((1,), (1,)), ((), ())), preferred_element_type=F32)


def _gelu_tanh(x):
    return 0.5 * x * (1.0 + jnp.tanh(math.sqrt(2.0 / math.pi) * (x + 0.044715 * (x * x * x))))


def _silu(x):
    return x * (1.0 / (1.0 + jnp.exp(-x)))


def _sigmoid(x):
    return 1.0 / (1.0 + jnp.exp(-x))


def _row_rms(x, g):
    return x * lax.rsqrt(jnp.mean(x * x, axis=-1, keepdims=True) + EPS) * g


def _rmsnorm_kernel(x_ref, g_ref, o_ref):
    o_ref[...] = _row_rms(x_ref[...], g_ref[...]).astype(o_ref.dtype)


def rmsnorm_bf16(x, g):
    t, d = x.shape
    tr = _tile(t, 256)
    return pl.pallas_call(
        _rmsnorm_kernel,
        grid=(t // tr,),
        in_specs=[pl.BlockSpec((tr, d), lambda i: (i, 0)), pl.BlockSpec((1, d), lambda i: (0, 0))],
        out_specs=pl.BlockSpec((tr, d), lambda i: (i, 0)),
        out_shape=jax.ShapeDtypeStruct((t, d), BF16),
        compiler_params=_cparams(("parallel",)),
        name="rmsnorm",
    )(x, g.reshape(1, d))


def _mm_kernel(a_ref, w_ref, o_ref):
    o_ref[...] = _dot(a_ref[...], w_ref[...].astype(BF16)).astype(o_ref.dtype)


def _mm_res_kernel(a_ref, w_ref, r_ref, o_ref):
    o_ref[...] = (r_ref[...] + _dot(a_ref[...], w_ref[...].astype(BF16))).astype(o_ref.dtype)


def matmul(a, w, col0, n, res=None, out_dtype=F32):
    m, k = a.shape
    tm = _tile(m, 1024)
    tn = _tile(n, 512)
    assert w.shape[:2] == (1, k) and col0 % tn == 0
    c0 = col0 // tn
    in_specs = [pl.BlockSpec((tm, k), lambda i, j: (i, 0)), pl.BlockSpec((None, k, tn), lambda i, j: (0, 0, c0 + j))]
    args = [a, w]
    body = _mm_kernel
    if res is not None:
        in_specs.append(pl.BlockSpec((tm, tn), lambda i, j: (i, j)))
        args.append(res)
        body = _mm_res_kernel
    return pl.pallas_call(
        body,
        grid=(m // tm, n // tn),
        in_specs=in_specs,
        out_specs=pl.BlockSpec((tm, tn), lambda i, j: (i, j)),
        out_shape=jax.ShapeDtypeStruct((m, n), out_dtype),
        compiler_params=_cparams(("parallel", "arbitrary")),
        name="matmul",
    )(*args)


def _ffn_kernel(x_hbm, g_ref, w1_ref, w3_ref, w2_ref, o_ref, n_ref, sem, *, tm, rb):
    i = pl.program_id(0)
    f = pl.program_id(1)

    @pl.when(f == 0)
    def _():
        cp = pltpu.make_async_copy(x_hbm.at[pl.ds(pl.multiple_of(i * tm, tm), tm), :], o_ref, sem.at[0])
        cp.start()
        cp.wait()

        def norm_rows(r, carry):
            rows = pl.ds(pl.multiple_of(r * rb, rb), rb)
            n_ref[rows, :] = _row_rms(o_ref[rows, :], g_ref[...]).astype(BF16)
            return carry

        lax.fori_loop(0, tm // rb, norm_rows, 0)

    n = n_ref[...]
    h = (_silu(_dot(n, w1_ref[...].astype(BF16))) * _dot(n, w3_ref[...].astype(BF16)) * 0.5).astype(BF16)
    o_ref[...] += _dot(h, w2_ref[...].astype(BF16))


def ffn_half_step(x, g, w1, w3, w2, which):
    t, d = x.shape
    dff = w1.shape[3]
    tm = _tile(t, 1024)
    tf = _tile(dff, 256)
    return pl.pallas_call(
        functools.partial(_ffn_kernel, tm=tm, rb=_tile(tm, 64)),
        grid=(t // tm, dff // tf),
        in_specs=[
            pl.BlockSpec(memory_space=pl.ANY),
            pl.BlockSpec((1, d), lambda i, f: (0, 0)),
            pl.BlockSpec((None, None, d, tf), lambda i, f: (0, which, 0, f)),
            pl.BlockSpec((None, None, d, tf), lambda i, f: (0, which, 0, f)),
            pl.BlockSpec((None, None, tf, d), lambda i, f: (0, which, f, 0)),
        ],
        out_specs=pl.BlockSpec((tm, d), lambda i, f: (i, 0), pipeline_mode=pl.Buffered(1)),
        out_shape=jax.ShapeDtypeStruct((t, d), F32),
        scratch_shapes=[pltpu.VMEM((tm, d), BF16), pltpu.SemaphoreType.DMA((1,))],
        compiler_params=_cparams(("parallel", "arbitrary")),
        name="ffn",
    )(x, g.reshape(1, d), w1, w3, w2)


def _headnorm_kernel(x_ref, g_ref, o_ref, *, scale):
    o_ref[...] = (_row_rms(x_ref[...], g_ref[...]) * scale).astype(o_ref.dtype)


def headnorm_bf16(x, col0, nheads, g, scale=1.0):
    t = x.shape[0]
    tr = _tile(t, 1024)
    c0 = col0 // HEAD_DIM
    return pl.pallas_call(
        functools.partial(_headnorm_kernel, scale=scale),
        grid=(t // tr, nheads),
        in_specs=[pl.BlockSpec((tr, HEAD_DIM), lambda i, j: (i, c0 + j)),
                  pl.BlockSpec((1, HEAD_DIM), lambda i, j: (0, 0))],
        out_specs=pl.BlockSpec((tr, HEAD_DIM), lambda i, j: (i, j)),
        out_shape=jax.ShapeDtypeStruct((t, nheads * HEAD_DIM), BF16),
        compiler_params=_cparams(("parallel", "parallel")),
        name="headnorm",
    )(x, g.reshape(1, HEAD_DIM))


def _vt_kernel(x_ref, o_ref):
    o_ref[...] = x_ref[...].T.astype(o_ref.dtype)


def values_transposed(x, col0, b, l):
    tr = _tile(l, 512)
    nl = l // tr
    c0 = col0 // HEAD_DIM
    return pl.pallas_call(
        _vt_kernel,
        grid=(b, NSA_KV_HEADS, nl),
        in_specs=[pl.BlockSpec((tr, HEAD_DIM), lambda bi, h, i: (bi * nl + i, c0 + h))],
        out_specs=pl.BlockSpec((None, None, HEAD_DIM, tr), lambda bi, h, i: (bi, h, 0, i)),
        out_shape=jax.ShapeDtypeStruct((b, NSA_KV_HEADS, HEAD_DIM, l), BF16),
        compiler_params=_cparams(("parallel", "parallel", "parallel")),
        name="values_t",
    )(x)


def _compress_kernel(x_ref, pe_ref, w1_ref, b1_ref, w2_ref, b2_ref, kn_ref, o_ref, xs_ref, *, l, is_k):
    nc = l // CMP_STRIDE
    xs_ref[pl.ds(0, l), :] = x_ref[...]
    xs_ref[pl.ds(l, CMP_BLOCK), :] = jnp.zeros((CMP_BLOCK, HEAD_DIM), F32)
    acc = jnp.zeros((nc, CMP_HIDDEN), F32)
    for r in range(CMP_BLOCK):
        xr = xs_ref[pl.ds(r, nc, stride=CMP_STRIDE), :] + pe_ref[pl.ds(r, 1), :]
        acc = acc + _dot(xr.astype(BF16), w1_ref[r])
    hid = _gelu_tanh(acc + b1_ref[...])
    c = _dot(hid.astype(BF16), w2_ref[...]) + b2_ref[...]
    if is_k:
        o_ref[...] = _row_rms(c, kn_ref[...]).astype(o_ref.dtype)
    else:
        o_ref[...] = c.T.astype(o_ref.dtype)


def compress_prompt(kv, which, b, l, pe, w1, b1, w2, b2, kn):
    nc = l // CMP_STRIDE
    is_k = which == 0
    oshape = (b, NSA_KV_HEADS, nc, HEAD_DIM) if is_k else (b, NSA_KV_HEADS, HEAD_DIM, nc)
    oblock = (None, None, nc, HEAD_DIM) if is_k else (None, None, HEAD_DIM, nc)
    return pl.pallas_call(
        functools.partial(_compress_kernel, l=l, is_k=is_k),
        grid=(b, NSA_KV_HEADS),
        in_specs=[
            pl.BlockSpec((l, HEAD_DIM), lambda bi, h: (bi, which * NSA_KV_HEADS + h)),
            pl.BlockSpec((CMP_BLOCK, HEAD_DIM), lambda bi, h: (0, 0)),
            pl.BlockSpec((CMP_BLOCK, HEAD_DIM, CMP_HIDDEN), lambda bi, h: (0, 0, 0)),
            pl.BlockSpec((1, CMP_HIDDEN), lambda bi, h: (0, 0)),
            pl.BlockSpec((CMP_HIDDEN, HEAD_DIM), lambda bi, h: (0, 0)),
            pl.BlockSpec((1, HEAD_DIM), lambda bi, h: (0, 0)),
            pl.BlockSpec((1, HEAD_DIM), lambda bi, h: (0, 0)),
        ],
        out_specs=pl.BlockSpec(oblock, lambda bi, h: (bi, h, 0, 0)),
        out_shape=jax.ShapeDtypeStruct(oshape, BF16),
        scratch_shapes=[pltpu.VMEM((l + CMP_BLOCK, HEAD_DIM), F32)],
        compiler_params=_cparams(("parallel", "parallel")),
        name="compress_prompt",
    )(kv, pe, w1, b1.reshape(1, -1), w2, b2.reshape(1, -1), kn.reshape(1, -1))


def _select_topk(score, j, ntop):
    ns = score.shape[0]
    bias = jnp.full(score.shape, NEG, F32)
    for _ in range(ntop):
        mx = jnp.max(score, axis=0, keepdims=True)
        idx = jnp.min(jnp.where(score == mx, j, ns), axis=0, keepdims=True)
        hit = j == idx
        bias = jnp.where(hit, 0.0, bias)
        score = jnp.where(hit, -jnp.inf, score)
    return bias


def _cmp_attn_kernel(slope_ref, q_ref, kc_ref, vct_ref, mt_ref, o_ref, sel_ref, *, tq, nc, ns):
    kvh = pl.program_id(1)
    qi = pl.program_id(2)
    qp = qi * tq + lax.broadcasted_iota(jnp.int32, (nc, tq), 1)
    blk_end = lax.broadcasted_iota(jnp.int32, (nc, tq), 0) * CMP_STRIDE + (CMP_BLOCK - 1)
    dist = qp - blk_end
    mask = dist >= 0
    distf = dist.astype(F32)
    kc = kc_ref[...]
    vct = vct_ref[...]
    imp = jnp.zeros((nc, tq), F32)
    for g in range(NSA_GROUP):
        s = _dot_nt(kc, q_ref[:, g * HEAD_DIM:(g + 1) * HEAD_DIM])
        s = jnp.where(mask, s - slope_ref[kvh * NSA_GROUP + g] * distf, -jnp.inf)
        m = jnp.max(s, axis=0, keepdims=True)
        m = jnp.where(m == -jnp.inf, 0.0, m)
        e = jnp.where(mask, jnp.exp(s - m), 0.0)
        p = e / jnp.maximum(jnp.sum(e, axis=0, keepdims=True), 1e-30)
        imp = imp + p
        o_ref[:, g * HEAD_DIM:(g + 1) * HEAD_DIM] = _dot(vct, p.astype(BF16)).T
    score = jnp.dot(mt_ref[...], imp, preferred_element_type=F32, precision=lax.Precision.HIGHEST)
    j = lax.broadcasted_iota(jnp.int32, (ns, tq), 0)
    cur = (qi * tq + lax.broadcasted_iota(jnp.int32, (ns, tq), 1)) // SLC_BLOCK
    forced = (j == 0) | (j == cur) | (j == cur - 1)
    score = jnp.where(j <= cur, jnp.where(forced, BIG, score), -BIG)
    sel_ref[...] = _select_topk(score, j, min(SLC_TOPK, ns))


def _overlap_matrix(n_cmp, n_slc):
    cs = np.arange(n_cmp)[:, None] * CMP_STRIDE
    ss = np.arange(n_slc)[None, :] * SLC_BLOCK
    ov = np.minimum(cs + CMP_BLOCK, ss + SLC_BLOCK) - np.maximum(cs, ss)
    return (np.maximum(ov, 0).astype(np.float32) / CMP_BLOCK)


def cmp_attention_prompt(qn, kc, vct, slopes, b, l):
    nc = l // CMP_STRIDE
    ns = l // SLC_BLOCK
    tq = _tile(l, 512)
    nq = l // tq
    mt = jnp.asarray(_overlap_matrix(nc, ns).T)
    return pl.pallas_call(
        functools.partial(_cmp_attn_kernel, tq=tq, nc=nc, ns=ns),
        grid=(b, NSA_KV_HEADS, nq),
        in_specs=[
            pl.BlockSpec(memory_space=pltpu.SMEM),
            pl.BlockSpec((tq, GQ_W), lambda bi, h, i: (bi * nq + i, h)),
            pl.BlockSpec((None, None, nc, HEAD_DIM), lambda bi, h, i: (bi, h, 0, 0)),
            pl.BlockSpec((None, None, HEAD_DIM, nc), lambda bi, h, i: (bi, h, 0, 0)),
            pl.BlockSpec((ns, nc), lambda bi, h, i: (0, 0)),
        ],
        out_specs=[
            pl.BlockSpec((tq, GQ_W), lambda bi, h, i: (bi * nq + i, h)),
            pl.BlockSpec((None, None, ns, tq), lambda bi, h, i: (bi, h, 0, i)),
        ],
        out_shape=[jax.ShapeDtypeStruct((b * l, NSA_HEADS * HEAD_DIM), F32),
                   jax.ShapeDtypeStruct((b, NSA_KV_HEADS, ns, l), F32)],
        compiler_params=_cparams(("parallel", "parallel", "parallel")),
        name="cmp_attention",
    )(slopes, qn, kc, vct, mt)


def _flash_kernel(slope_ref, q_ref, k_ref, vt_ref, *rest, tq, tk, nkt, windowed):
    if windowed:
        o_ref, m_ref, l_ref, acc_ref = rest
        sel_ref = None
    else:
        sel_ref, o_ref, m_ref, l_ref, acc_ref = rest
    kvh = pl.program_id(1)
    qi = pl.program_id(2)
    kt = pl.program_id(3)
    if windowed:
        ktile = qi - (nkt - 1) + kt
        valid = ktile >= 0
    else:
        ktile = kt
        valid = kt * tk <= qi * tq + (tq - 1)

    @pl.when(kt == 0)
    def _():
        m_ref[...] = jnp.full(m_ref.shape, NEG, F32)
        l_ref[...] = jnp.zeros(l_ref.shape, F32)
        acc_ref[...] = jnp.zeros(acc_ref.shape, F32)

    @pl.when(valid)
    def _():
        kpos = ktile * tk + lax.broadcasted_iota(jnp.int32, (tk, tq), 0)
        qpos = qi * tq + lax.broadcasted_iota(jnp.int32, (tk, tq), 1)
        dist = qpos - kpos
        distf = dist.astype(F32)
        if windowed:
            bias = jnp.where((dist >= 0) & (dist < WINDOW), 0.0, NEG)
        else:
            sel = sel_ref[...]
            sel = jnp.concatenate(
                [jnp.broadcast_to(sel[r:r + 1, :], (SLC_BLOCK, tq)) for r in range(tk // SLC_BLOCK)], axis=0)
            bias = jnp.where(dist >= 0, sel, NEG)
        k = k_ref[...]
        vt = vt_ref[...]
        for g in range(NSA_GROUP):
            s = _dot_nt(k, q_ref[:, g * HEAD_DIM:(g + 1) * HEAD_DIM])
            s = s - slope_ref[kvh * NSA_GROUP + g] * distf + bias
            m_prev = m_ref[pl.ds(g, 1), :]
            m_new = jnp.maximum(m_prev, jnp.max(s, axis=0, keepdims=True))
            alpha = jnp.exp(m_prev - m_new)
            e = jnp.exp(s - m_new)
            l_ref[pl.ds(g, 1), :] = alpha * l_ref[pl.ds(g, 1), :] + jnp.sum(e, axis=0, keepdims=True)
            acc_ref[g] = alpha * acc_ref[g] + _dot(vt, e.astype(BF16))
            m_ref[pl.ds(g, 1), :] = m_new

    @pl.when(kt == nkt - 1)
    def _():
        for g in range(NSA_GROUP):
            o = acc_ref[g] / jnp.maximum(l_ref[pl.ds(g, 1), :], 1e-30)
            o_ref[:, g * HEAD_DIM:(g + 1) * HEAD_DIM] = o.T


def flash_prompt(qn, kn, vt, slopes, b, l, sel=None):
    windowed = sel is None
    tq = _tile(l, 512)
    tk = tq
    nq = l // tq
    nkt = (min(WINDOW, l) // tk + 1) if windowed else l // tk
    if windowed:
        kmap = lambda qi, kt: jnp.maximum(qi - (nkt - 1) + kt, 0)
    else:
        kmap = lambda qi, kt: jnp.minimum(kt, (qi * tq + tq - 1) // tk)
    in_specs = [
        pl.BlockSpec(memory_space=pltpu.SMEM),
        pl.BlockSpec((tq, GQ_W), lambda bi, h, qi, kt: (bi * nq + qi, h)),
        pl.BlockSpec((tk, HEAD_DIM), lambda bi, h, qi, kt: (bi * nq + kmap(qi, kt), h)),
        pl.BlockSpec((None, None, HEAD_DIM, tk), lambda bi, h, qi, kt: (bi, h, 0, kmap(qi, kt))),
    ]
    args = [slopes, qn, kn, vt]
    if not windowed:
        in_specs.append(pl.BlockSpec((None, None, tk // SLC_BLOCK, tq),
                                     lambda bi, h, qi, kt: (bi, h, kmap(qi, kt), qi)))
        args.append(sel)
    return pl.pallas_call(
        functools.partial(_flash_kernel, tq=tq, tk=tk, nkt=nkt, windowed=windowed),
        grid=(b, NSA_KV_HEADS, nq, nkt),
        in_specs=in_specs,
        out_specs=pl.BlockSpec((tq, GQ_W), lambda bi, h, qi, kt: (bi * nq + qi, h)),
        out_shape=jax.ShapeDtypeStruct((b * l, NSA_HEADS * HEAD_DIM), F32),
        scratch_shapes=[pltpu.VMEM((SUBLANE, tq), F32), pltpu.VMEM((SUBLANE, tq), F32),
                        pltpu.VMEM((NSA_GROUP, HEAD_DIM, tq), F32)],
        compiler_params=_cparams(("parallel", "parallel", "parallel", "arbitrary")),
        name="window_attention" if windowed else "selected_attention",
    )(*args)


def _combine_kernel(g_ref, c_ref, s_ref, w_ref, o_ref):
    gates = _sigmoid(g_ref[...])
    for h in range(NSA_HEADS):
        cols = slice(h * HEAD_DIM, (h + 1) * HEAD_DIM)
        o = (gates[:, h:h + 1] * c_ref[:, cols]
             + gates[:, NSA_HEADS + h:NSA_HEADS + h + 1] * s_ref[:, cols]
             + gates[:, 2 * NSA_HEADS + h:2 * NSA_HEADS + h + 1] * w_ref[:, cols])
        o_ref[:, cols] = o.astype(o_ref.dtype)


def combine_branches(gates, o_cmp, o_slc, o_win):
    t, w = o_cmp.shape
    tr = _tile(t, 256)
    spec = pl.BlockSpec((tr, w), lambda i: (i, 0))
    return pl.pallas_call(
        _combine_kernel,
        grid=(t // tr,),
        in_specs=[pl.BlockSpec((tr, gates.shape[1]), lambda i: (i, 0)), spec, spec, spec],
        out_specs=spec,
        out_shape=jax.ShapeDtypeStruct((t, w), BF16),
        compiler_params=_cparams(("parallel",)),
        name="combine",
    )(gates, o_cmp, o_slc, o_win)


def _group_norm_gate(o, rg, gn):
    mu = jnp.mean(o, axis=-1, keepdims=True)
    var = jnp.mean(jnp.square(o - mu), axis=-1, keepdims=True)
    return _silu(rg) * ((o - mu) * lax.rsqrt(var + EPS) * gn)


def _retention_kernel(q_ref, k_ref, v_ref, rg_ref, intra_ref, qd_ref, kd_ref, cd_ref, gn_ref,
                      o_ref, s_ref, st_ref, *, tl, hb):
    li = pl.program_id(2)
    nl = pl.num_programs(2)

    @pl.when(li == 0)
    def _():
        st_ref[...] = jnp.zeros(st_ref.shape, F32)

    def chunk(c, carry):
        rows = pl.ds(pl.multiple_of(c * RET_CHUNK, RET_CHUNK), RET_CHUNK)
        for h in range(hb):
            cols = slice(h * HEAD_DIM, (h + 1) * HEAD_DIM)
            qc = q_ref[rows, cols]
            kc = k_ref[rows, cols] * (HEAD_DIM ** -0.5)
            vc = v_ref[rows, cols].astype(BF16)
            state = st_ref[h]
            att = _dot_nt(qc.astype(BF16), kc.astype(BF16)) * intra_ref[h]
            o = _dot(att.astype(BF16), vc) + _dot((qc * qd_ref[h]).astype(BF16), state.astype(BF16))
            st_ref[h] = state * cd_ref[h] + _dot((kc * kd_ref[h]).T.astype(BF16), vc)
            o_ref[rows, cols] = _group_norm_gate(o, rg_ref[rows, cols], gn_ref[:, cols]).astype(o_ref.dtype)
        return carry

    lax.fori_loop(0, tl // RET_CHUNK, chunk, 0)

    @pl.when(li == nl - 1)
    def _():
        s_ref[...] = st_ref[...]


def _retention_tables(chunk):
    h = jnp.arange(RET_HEADS, dtype=F32)
    log_g = jnp.log1p(-jnp.exp2(-5.0 - h))
    i = jnp.arange(chunk, dtype=F32)
    diff = i[:, None] - i[None, :]
    intra = jnp.where(diff >= 0, jnp.exp(jnp.maximum(diff, 0.0)[None] * log_g[:, None, None]), 0.0)
    q_dec = jnp.exp((i[None, :] + 1.0) * log_g[:, None])
    k_dec = jnp.exp((chunk - 1.0 - i)[None, :] * log_g[:, None])
    c_dec = jnp.exp(chunk * log_g)
    return intra, q_dec, k_dec, c_dec


def retention_prompt(r, gn, b, l):
    intra, q_dec, k_dec, c_dec = _retention_tables(RET_CHUNK)
    bc = lambda t: jnp.broadcast_to(t[:, :, None], (RET_HEADS, RET_CHUNK, HEAD_DIM))
    qd, kd = bc(q_dec), bc(k_dec)
    cd = jnp.broadcast_to(c_dec[:, None, None], (RET_HEADS, HEAD_DIM, HEAD_DIM))
    hb = 4
    ng = RET_HEADS // hb
    tl = _tile(l, 1024)
    nl = l // tl
    w = hb * HEAD_DIM
    rspec = lambda part: pl.BlockSpec((tl, w), lambda bi, g, li: (bi * nl + li, part * ng + g))
    tspec = pl.BlockSpec((hb, RET_CHUNK, HEAD_DIM), lambda bi, g, li: (g, 0, 0))
    return pl.pallas_call(
        functools.partial(_retention_kernel, tl=tl, hb=hb),
        grid=(b, ng, nl),
        in_specs=[rspec(0), rspec(1), rspec(2), rspec(3), tspec, tspec, tspec, tspec,
                  pl.BlockSpec((1, w), lambda bi, g, li: (0, g))],
        out_specs=[pl.BlockSpec((tl, w), lambda bi, g, li: (bi * nl + li, g)),
                   pl.BlockSpec((None, hb, HEAD_DIM, HEAD_DIM), lambda bi, g, li: (bi, g, 0, 0))],
        out_shape=[jax.ShapeDtypeStruct((b * l, RET_HEADS * HEAD_DIM), BF16),
                   jax.ShapeDtypeStruct((b, RET_HEADS, HEAD_DIM, HEAD_DIM), F32)],
        scratch_shapes=[pltpu.VMEM((hb, HEAD_DIM, HEAD_DIM), F32)],
        compiler_params=_cparams(("parallel", "parallel", "arbitrary")),
        name="retention_prompt",
    )(r, r, r, r, intra, qd, kd, cd, gn.reshape(1, -1))


def _group_rows(kvh):
    row = lax.broadcasted_iota(jnp.int32, (NSA_HEADS, 1), 0)
    return (row >= kvh * NSA_GROUP) & (row < (kvh + 1) * NSA_GROUP)


def _sample_cmp_kernel(pt_ref, *refs, pp, npg, pos0):
    pages = refs[:pp]
    (q_ref, slope_ref, pea_ref, peb_ref, w1a_ref, w1b_ref, b1_ref, w2_ref, b2_ref, kn_ref, m_ref,
     o_ref, idx_ref, x_ref, f_ref, s_ref) = refs[pp:]
    jp = pl.program_id(1)
    cpp = PAGE_ROWS // CMP_STRIDE
    nck = npg * cpp

    @pl.when(jp == 0)
    def _():
        s_ref[...] = jnp.zeros(s_ref.shape, F32)

    for t in range(pp):
        for w in range(2):
            for kvh in range(NSA_KV_HEADS):
                for c in range(CMP_STRIDE):
                    x_ref[w, pl.ds((t * NSA_KV_HEADS + kvh) * cpp, cpp), pl.ds(c * HEAD_DIM, HEAD_DIM)] = (
                        pages[t][pl.ds(c, cpp, stride=CMP_STRIDE), w * NSA_KV_HEADS + kvh, :])
    for w in range(2):
        x = x_ref[w]
        fa = _dot((x + pea_ref[w]).astype(BF16), w1a_ref[w])
        sb = _dot((x + peb_ref[w]).astype(BF16), w1b_ref[w])
        for t in range(pp):
            for kvh in range(NSA_KV_HEADS):
                src = slice((t * NSA_KV_HEADS + kvh) * cpp, (t * NSA_KV_HEADS + kvh + 1) * cpp)
                dst = pl.ds(pl.multiple_of((jp * pp + t) * cpp, cpp), cpp)
                f_ref[w, kvh, dst, :] = fa[src]
                s_ref[w, kvh, dst, :] = sb[src]

    @pl.when(jp == pl.num_programs(1) - 1)
    def _():
        q = q_ref[...]
        n_io = lax.broadcasted_iota(jnp.int32, (NSA_HEADS, nck), 1)
        dist = pos0 - (n_io * CMP_STRIDE + (CMP_BLOCK - 1))
        mask = dist >= 0
        bias = slope_ref[:, :nck] * dist.astype(F32)
        o_acc = jnp.zeros((NSA_HEADS, HEAD_DIM), F32)
        imp_rows = []
        for kvh in range(NSA_KV_HEADS):
            def cblock(w):
                hid = _gelu_tanh(f_ref[w, kvh, pl.ds(0, nck), :] + s_ref[w, kvh, pl.ds(1, nck), :] + b1_ref[w])
                return _dot(hid.astype(BF16), w2_ref[w]) + b2_ref[w]
            kc = _row_rms(cblock(0), kn_ref[...]).astype(BF16)
            vc = cblock(1).astype(BF16)
            s = jnp.where(mask, _dot_nt(q, kc) - bias, -jnp.inf)
            m = jnp.max(s, axis=1, keepdims=True)
            m = jnp.where(m == -jnp.inf, 0.0, m)
            e = jnp.where(mask, jnp.exp(s - m), 0.0)
            p = e / jnp.maximum(jnp.sum(e, axis=1, keepdims=True), 1e-30)
            grp = _group_rows(kvh)
            o_acc = o_acc + jnp.where(grp, _dot(p.astype(BF16), vc), 0.0)
            imp_rows.append(jnp.sum(jnp.where(grp, p, 0.0), axis=0, keepdims=True))
        o_ref[...] = o_acc
        imp = jnp.concatenate(imp_rows + [jnp.zeros((SUBLANE - NSA_KV_HEADS, nck), F32)], axis=0)
        score = jnp.dot(imp, m_ref[...], preferred_element_type=F32, precision=lax.Precision.HIGHEST)
        nsp = score.shape[1]
        j = lax.broadcasted_iota(jnp.int32, (SUBLANE, nsp), 1)
        cur = pos0 // SLC_BLOCK
        forced = (j == 0) | (j == cur) | (j == cur - 1)
        score = jnp.where(j <= cur, jnp.where(forced, BIG, score), -BIG)
        lane = lax.broadcasted_iota(jnp.int32, (SUBLANE, LANE), 1)
        out = jnp.zeros((SUBLANE, LANE), jnp.int32)
        for t in range(SLC_TOPK):
            mx = jnp.max(score, axis=1, keepdims=True)
            idx = jnp.min(jnp.where(score == mx, j, nsp), axis=1, keepdims=True)
            out = jnp.where(lane == t, idx, out)
            score = jnp.where(j == idx, -jnp.inf, score)
        idx_ref[...] = out


PAGE_ROWS = 128


def sample_cmp_select(cache4, page_table, qn3, slopes_b, pe, w1, b1, w2, b2, kn, pos0):
    db, npg = page_table.shape
    pp = _tile(npg, 8)
    cpp = PAGE_ROWS // CMP_STRIDE
    nck = npg * cpp
    ng = 2 * NSA_KV_HEADS
    n_slc = (pos0 + 1 + SLC_BLOCK - 1) // SLC_BLOCK
    nsp = -(-n_slc // LANE) * LANE
    m = np.zeros((nck, nsp), np.float32)
    m[:, :n_slc] = _overlap_matrix(nck, n_slc)
    flat = lambda a: a.reshape(2, 1, CMP_STRIDE * HEAD_DIM)
    pea, peb = flat(pe[:, :CMP_STRIDE]), flat(pe[:, CMP_STRIDE:])
    w1a = w1[:, :CMP_STRIDE].reshape(2, CMP_STRIDE * HEAD_DIM, CMP_HIDDEN)
    w1b = w1[:, CMP_STRIDE:].reshape(2, CMP_STRIDE * HEAD_DIM, CMP_HIDDEN)
    const = lambda shape: pl.BlockSpec(shape, lambda bi, jp, pt: (0,) * len(shape))
    page_specs = [
        pl.BlockSpec((None, PAGE_ROWS, ng, HEAD_DIM),
                     functools.partial(lambda bi, jp, pt, t: (pt[bi, jp * pp + t], 0, 0, 0), t=t))
        for t in range(pp)]
    grid_spec = pltpu.PrefetchScalarGridSpec(
        num_scalar_prefetch=1,
        grid=(db, npg // pp),
        in_specs=page_specs + [
            pl.BlockSpec((None, NSA_HEADS, HEAD_DIM), lambda bi, jp, pt: (bi, 0, 0)),
            const((NSA_HEADS, slopes_b.shape[1])),
            const((2, 1, CMP_STRIDE * HEAD_DIM)), const((2, 1, CMP_STRIDE * HEAD_DIM)),
            const((2, CMP_STRIDE * HEAD_DIM, CMP_HIDDEN)), const((2, CMP_STRIDE * HEAD_DIM, CMP_HIDDEN)),
            const((2, 1, CMP_HIDDEN)), const((2, CMP_HIDDEN, HEAD_DIM)), const((2, 1, HEAD_DIM)),
            const((1, HEAD_DIM)), const((nck, nsp)),
        ],
        out_specs=[pl.BlockSpec((None, NSA_HEADS, HEAD_DIM), lambda bi, jp, pt: (bi, 0, 0)),
                   pl.BlockSpec((None, SUBLANE, LANE), lambda bi, jp, pt: (bi, 0, 0))],
        scratch_shapes=[
            pltpu.VMEM((2, pp * NSA_KV_HEADS * cpp, CMP_STRIDE * HEAD_DIM), F32),
            pltpu.VMEM((2, NSA_KV_HEADS, nck, CMP_HIDDEN), F32),
            pltpu.VMEM((2, NSA_KV_HEADS, nck + SUBLANE, CMP_HIDDEN), F32),
        ],
    )
    return pl.pallas_call(
        functools.partial(_sample_cmp_kernel, pp=pp, npg=npg, pos0=pos0),
        grid_spec=grid_spec,
        out_shape=[jax.ShapeDtypeStruct((db, NSA_HEADS, HEAD_DIM), F32),
                   jax.ShapeDtypeStruct((db, SUBLANE, LANE), jnp.int32)],
        compiler_params=_cparams(("parallel", "arbitrary")),
        name="sample_cmp_select",
    )(page_table, *([cache4] * pp), qn3, slopes_b, pea, peb, w1a, w1b, b1.reshape(2, 1, -1), w2,
      b2.reshape(2, 1, -1), kn.reshape(1, -1), jnp.asarray(m))


def _softmax_with_new_key(s, s_new):
    m = jnp.maximum(jnp.max(s, axis=1, keepdims=True), s_new)
    e = jnp.exp(s - m)
    e_new = jnp.exp(s_new - m)
    return e, e_new, jnp.maximum(jnp.sum(e, axis=1, keepdims=True) + e_new, 1e-30)


def _bf16_round(x):
    return x.astype(BF16).astype(F32)


def _sample_slc_kernel(row_ref, cache_ref, q_ref, kpos_ref, slope_ref, new_ref, kn_ref, o_ref, kbuf, vbuf, sem, *, pos0):
    bi = pl.program_id(0)
    nb = pl.num_programs(0)
    nsel = SLC_TOPK * SLC_BLOCK
    kcol = 2 * NSA_KV_HEADS
    vcol = 3 * NSA_KV_HEADS

    def gather(b, slot, start):
        for kvh in range(NSA_KV_HEADS):
            for t in range(SLC_TOPK):
                src = pl.ds(row_ref[(b * NSA_KV_HEADS + kvh) * SLC_TOPK + t], SLC_BLOCK)
                dst = pl.ds(t * SLC_BLOCK, SLC_BLOCK)
                for col, buf in ((kcol, kbuf), (vcol, vbuf)):
                    cp = pltpu.make_async_copy(cache_ref.at[src, col + kvh, :], buf.at[slot, kvh, dst, :], sem.at[slot])
                    if start:
                        cp.start()
                    else:
                        cp.wait()

    slot = bi % 2

    @pl.when(bi == 0)
    def _():
        gather(bi, 0, True)

    @pl.when(bi + 1 < nb)
    def _():
        gather(bi + 1, 1 - slot, True)

    gather(bi, slot, False)

    q = q_ref[...]
    o_acc = jnp.zeros((NSA_HEADS, HEAD_DIM), F32)
    for kvh in range(NSA_KV_HEADS):
        kn = _row_rms(kbuf[slot, kvh], kn_ref[...]).astype(BF16)
        v = vbuf[slot, kvh]
        dist = pos0 - kpos_ref[pl.ds(kvh, 1), :]
        s = _dot_nt(q, kn) - slope_ref[:, :nsel] * dist.astype(F32)
        s = jnp.where(dist >= 0, s, NEG)
        knew = _row_rms(new_ref[pl.ds(kvh, 1), :], kn_ref[...])
        s_new = jnp.sum(q.astype(F32) * _bf16_round(knew), axis=1, keepdims=True)
        e, e_new, l = _softmax_with_new_key(s, s_new)
        vnew = new_ref[pl.ds(NSA_KV_HEADS + kvh, 1), :]
        o = _dot(e.astype(BF16), v.astype(BF16)) + _bf16_round(e_new) * _bf16_round(vnew)
        o_acc = o_acc + jnp.where(_group_rows(kvh), o / l, 0.0)
    o_ref[...] = o_acc


def sample_slc_attention(cache3, row0, qn3, kpos, slopes_b, new_kv, kn, pos0):
    db = qn3.shape[0]
    nsel = SLC_TOPK * SLC_BLOCK
    ng = 2 * NSA_KV_HEADS
    grid_spec = pltpu.PrefetchScalarGridSpec(
        num_scalar_prefetch=1,
        grid=(db,),
        in_specs=[
            pl.BlockSpec(memory_space=pl.ANY),
            pl.BlockSpec((None, NSA_HEADS, HEAD_DIM), lambda bi, r: (bi, 0, 0)),
            pl.BlockSpec((None, NSA_KV_HEADS, nsel), lambda bi, r: (bi, 0, 0)),
            pl.BlockSpec((NSA_HEADS, slopes_b.shape[1]), lambda bi, r: (0, 0)),
            pl.BlockSpec((None, ng, HEAD_DIM), lambda bi, r: (bi, 0, 0)),
            pl.BlockSpec((1, HEAD_DIM), lambda bi, r: (0, 0)),
        ],
        out_specs=pl.BlockSpec((None, NSA_HEADS, HEAD_DIM), lambda bi, r: (bi, 0, 0)),
        scratch_shapes=[pltpu.VMEM((2, NSA_KV_HEADS, nsel, HEAD_DIM), F32),
                        pltpu.VMEM((2, NSA_KV_HEADS, nsel, HEAD_DIM), F32),
                        pltpu.SemaphoreType.DMA((2,))],
    )
    return pl.pallas_call(
        functools.partial(_sample_slc_kernel, pos0=pos0),
        grid_spec=grid_spec,
        out_shape=jax.ShapeDtypeStruct((db, NSA_HEADS, HEAD_DIM), F32),
        compiler_params=_cparams(("arbitrary",)),
        name="sample_slc_attention",
    )(row0, cache3, qn3, kpos, slopes_b, new_kv, kn.reshape(1, -1))


def _sample_win_kernel(sw_ref, q_ref, slope_ref, new_ref, kn_ref, g_ref, oc_ref, os_ref, o_ref, nw_ref, *, wb):
    q = q_ref[...]
    row = lax.broadcasted_iota(jnp.int32, (NSA_HEADS, wb), 1)
    dist = wb - row
    valid = dist < WINDOW
    bias = slope_ref[:, :wb] * dist.astype(F32)
    o_acc = jnp.zeros((NSA_HEADS, HEAD_DIM), F32)
    for kvh in range(NSA_KV_HEADS):
        k = sw_ref[:, kvh, :]
        v = sw_ref[:, NSA_KV_HEADS + kvh, :]
        kn = _row_rms(k, kn_ref[...]).astype(BF16)
        s = jnp.where(valid, _dot_nt(q, kn) - bias, NEG)
        knew = _row_rms(new_ref[pl.ds(kvh, 1), :], kn_ref[...])
        s_new = jnp.sum(q.astype(F32) * _bf16_round(knew), axis=1, keepdims=True)
        e, e_new, l = _softmax_with_new_key(s, s_new)
        vnew = new_ref[pl.ds(NSA_KV_HEADS + kvh, 1), :]
        o = _dot(e.astype(BF16), v.astype(BF16)) + _bf16_round(e_new) * _bf16_round(vnew)
        o_acc = o_acc + jnp.where(_group_rows(kvh), o / l, 0.0)
    g = _sigmoid(g_ref[...])
    o_ref[...] = (g[0] * oc_ref[...] + g[1] * os_ref[...] + g[2] * o_acc).astype(o_ref.dtype)
    nw_ref[pl.ds(0, wb - 1)] = sw_ref[pl.ds(1, wb - 1)]
    nw_ref[wb - 1] = new_ref[...]


def sample_win_attention(sw, qn3, slopes_b, win_new, kn, gates_b, o_cmp, o_slc):
    db, wb, ng, _ = sw.shape
    hspec = pl.BlockSpec((None, NSA_HEADS, HEAD_DIM), lambda bi: (bi, 0, 0))
    wspec = pl.BlockSpec((None, wb, ng, HEAD_DIM), lambda bi: (bi, 0, 0, 0))
    return pl.pallas_call(
        functools.partial(_sample_win_kernel, wb=wb),
        grid=(db,),
        in_specs=[
            wspec,
            hspec,
            pl.BlockSpec((NSA_HEADS, slopes_b.shape[1]), lambda bi: (0, 0)),
            pl.BlockSpec((None, ng, HEAD_DIM), lambda bi: (bi, 0, 0)),
            pl.BlockSpec((1, HEAD_DIM), lambda bi: (0, 0)),
            pl.BlockSpec((None, 3, NSA_HEADS, HEAD_DIM), lambda bi: (bi, 0, 0, 0)),
            hspec, hspec,
        ],
        out_specs=[hspec, wspec],
        out_shape=[jax.ShapeDtypeStruct((db, NSA_HEADS, HEAD_DIM), BF16),
                   jax.ShapeDtypeStruct((db, wb, ng, HEAD_DIM), F32)],
        compiler_params=_cparams(("parallel",)),
        name="sample_win_attention",
    )(sw, qn3, slopes_b, win_new, kn.reshape(1, -1), gates_b, o_cmp, o_slc)


def _sample_ret_kernel(st_ref, r_ref, gam_ref, gn_ref, o_ref, s_ref):
    q = r_ref[0]
    k = r_ref[1] * (HEAD_DIM ** -0.5)
    v = r_ref[2]
    rg = r_ref[3]
    gam = gam_ref[...]
    qb, kb, vb = _bf16_round(q), _bf16_round(k), _bf16_round(v)
    att = jnp.sum(qb * kb, axis=1, keepdims=True)
    kt = kb.T
    qd = (q * gam).astype(BF16)
    rows = []
    for h in range(RET_HEADS):
        state = st_ref[h]
        rows.append(_dot(qd[h:h + 1, :], state.astype(BF16)))
        s_ref[h] = state * gam[h:h + 1, :] + kt[:, h:h + 1] * vb[h:h + 1, :]
    o = _bf16_round(att) * vb + jnp.concatenate(rows, axis=0)
    o_ref[...] = _group_norm_gate(o, rg, gn_ref[...]).astype(o_ref.dtype)


def sample_retention(state, r4, gn):
    db = state.shape[0]
    _, q_dec, _, _ = _retention_tables(1)
    gam = jnp.broadcast_to(q_dec, (RET_HEADS, HEAD_DIM))
    return pl.pallas_call(
        _sample_ret_kernel,
        grid=(db,),
        in_specs=[
            pl.BlockSpec((None, RET_HEADS, HEAD_DIM, HEAD_DIM), lambda bi: (bi, 0, 0, 0)),
            pl.BlockSpec((None, 4, RET_HEADS, HEAD_DIM), lambda bi: (bi, 0, 0, 0)),
            pl.BlockSpec((RET_HEADS, HEAD_DIM), lambda bi: (0, 0)),
            pl.BlockSpec((RET_HEADS, HEAD_DIM), lambda bi: (0, 0)),
        ],
        out_specs=[pl.BlockSpec((None, RET_HEADS, HEAD_DIM), lambda bi: (bi, 0, 0)),
                   pl.BlockSpec((None, RET_HEADS, HEAD_DIM, HEAD_DIM), lambda bi: (bi, 0, 0, 0))],
        out_shape=[jax.ShapeDtypeStruct((db, RET_HEADS, HEAD_DIM), BF16),
                   jax.ShapeDtypeStruct((db, RET_HEADS, HEAD_DIM, HEAD_DIM), F32)],
        compiler_params=_cparams(("parallel",)),
        name="sample_retention",
    )(state, r4, gam, gn.reshape(RET_HEADS, HEAD_DIM))


def _alibi_slopes():
    i = np.arange(NSA_HEADS, dtype=np.float32)
    return np.exp2(-8.0 * (i + 1.0) / NSA_HEADS).astype(np.float32)


def _in_proj(n, w_in, w_ret):
    nq = NSA_HEADS * HEAD_DIM
    nkv = 4 * NSA_KV_HEADS * HEAD_DIM
    nwin = 2 * NSA_KV_HEADS * HEAD_DIM
    ng = 3 * NSA_HEADS
    nr = 4 * RET_HEADS * HEAD_DIM
    assert w_in.shape[2] == nq + nkv + nwin + ng + nr and ng <= LANE
    o_g = nq + nkv + nwin
    return [matmul(n, w_in, 0, nq), matmul(n, w_in, nq, nkv), matmul(n, w_in, nq + nkv, nwin),
            matmul(n, w_in, o_g, LANE), matmul(n, w_ret, 0, nr)]


def _ret_weights(w_in):
    return w_in[:, :, w_in.shape[2] - 4 * RET_HEADS * HEAD_DIM:]


def kernel(x_prompt, x_sample, cache_nsa_kv, state_win_kv, state_ret, page_table, norm_g, ffn_w1, ffn_w3, ffn_w2,
           w_in, nsa_q_norm, nsa_k_norm, cmp_pe, cmp_w1, cmp_b1, cmp_w2, cmp_b2, ret_gn_g, w_out):
    assert cache_nsa_kv.shape[0] == 1 and x_sample.shape[1] == 1
    b, l, d = x_prompt.shape
    db = x_sample.shape[0]
    npg = page_table.shape[1]
    n_pool = cache_nsa_kv.shape[1]
    assert cache_nsa_kv.shape[2] == PAGE_ROWS
    pos0 = npg * PAGE_ROWS
    half = NSA_KV_HEADS * HEAD_DIM

    w_ret = _ret_weights(w_in)
    ng = norm_g[0]
    qg, kg = nsa_q_norm[0], nsa_k_norm[0]
    pe = cmp_pe[0]
    cw1 = cmp_w1[0].astype(BF16)
    cb1 = cmp_b1[0]
    cw2 = cmp_w2[0].astype(BF16)
    cb2 = cmp_b2[0]
    gn = ret_gn_g[0]
    slopes = jnp.asarray(_alibi_slopes())
    q_scale = HEAD_DIM ** -0.5

    def front(x):
        h = ffn_half_step(x, ng[0], ffn_w1, ffn_w3, ffn_w2, 0)
        return h, _in_proj(rmsnorm_bf16(h, ng[1]), w_in, w_ret)

    def back(h, mixed):
        h = matmul(mixed, w_out, 0, w_out.shape[2], res=h)
        return ffn_half_step(h, ng[2], ffn_w1, ffn_w3, ffn_w2, 1)

    hp, (q, kv, win, gates, r) = front(x_prompt.reshape(b * l, d))
    qn = headnorm_bf16(q, 0, NSA_HEADS, qg, q_scale)
    kc = compress_prompt(kv, 0, b, l, pe[0], cw1[0], cb1[0], cw2[0], cb2[0], kg[0])
    vct = compress_prompt(kv, 1, b, l, pe[1], cw1[1], cb1[1], cw2[1], cb2[1], kg[0])
    o_cmp, sel = cmp_attention_prompt(qn, kc, vct, slopes, b, l)
    k_slc = headnorm_bf16(kv, 2 * half, NSA_KV_HEADS, kg[1])
    v_slc = values_transposed(kv, 3 * half, b, l)
    o_slc = flash_prompt(qn, k_slc, v_slc, slopes, b, l, sel=sel)
    k_win = headnorm_bf16(win, 0, NSA_KV_HEADS, kg[2])
    v_win = values_transposed(win, half, b, l)
    o_win = flash_prompt(qn, k_win, v_win, slopes, b, l)
    o_nsa = combine_branches(gates, o_cmp, o_slc, o_win)
    o_ret, ret_p = retention_prompt(r, gn, b, l)
    y_prompt = back(hp, jnp.concatenate([o_nsa, o_ret], axis=1)).reshape(b, l, d)
    kv_prompt = kv.reshape(1, b, l, 4, NSA_KV_HEADS, HEAD_DIM)
    wl = min(WINDOW, l)
    win_prompt = win.reshape(b, l, 2, NSA_KV_HEADS, HEAD_DIM)[:, l - wl:][None]

    hs, (q, kv, win, gates, r) = front(x_sample.reshape(db, d))
    qn3 = headnorm_bf16(q, 0, NSA_HEADS, qg, q_scale).reshape(db, NSA_HEADS, HEAD_DIM)
    slopes_b = jnp.broadcast_to(slopes[:, None], (NSA_HEADS, max(npg * (PAGE_ROWS // CMP_STRIDE),
                                                                 SLC_TOPK * SLC_BLOCK, state_win_kv.shape[2])))
    o_cmp, idx = sample_cmp_select(cache_nsa_kv.reshape(n_pool, PAGE_ROWS, 4 * NSA_KV_HEADS, HEAD_DIM), page_table,
                                   qn3, slopes_b, pe, cw1, cb1, cw2, cb2, kg[0], pos0)
    idx = idx[:, :NSA_KV_HEADS, :SLC_TOPK]
    n_past = pos0 // SLC_BLOCK
    in_cache = idx < n_past
    pidx = jnp.minimum(idx, n_past - 1)
    per_page = PAGE_ROWS // SLC_BLOCK
    page = jnp.take_along_axis(page_table, (pidx // per_page).reshape(db, -1), axis=1).reshape(pidx.shape)
    row0 = (page * PAGE_ROWS + (pidx % per_page) * SLC_BLOCK).astype(jnp.int32).reshape(-1)
    offs = jnp.arange(SLC_BLOCK, dtype=jnp.int32)
    kpos = jnp.where(in_cache[..., None], idx[..., None] * SLC_BLOCK + offs, pos0 + 1).reshape(db, NSA_KV_HEADS, -1)
    o_slc = sample_slc_attention(cache_nsa_kv.reshape(n_pool * PAGE_ROWS, 4 * NSA_KV_HEADS, HEAD_DIM), row0, qn3, kpos,
                                 slopes_b, kv[:, 2 * half:].reshape(db, 2 * NSA_KV_HEADS, HEAD_DIM), kg[1], pos0)
    gates_b = jnp.broadcast_to(gates[:, :3 * NSA_HEADS].reshape(db, 3, NSA_HEADS, 1), (db, 3, NSA_HEADS, HEAD_DIM))
    wb = state_win_kv.shape[2]
    o_nsa, win_s = sample_win_attention(state_win_kv.reshape(db, wb, 2 * NSA_KV_HEADS, HEAD_DIM), qn3, slopes_b,
                                        win.reshape(db, 2 * NSA_KV_HEADS, HEAD_DIM), kg[2], gates_b, o_cmp, o_slc)
    o_ret, ret_s = sample_retention(state_ret[0], r.reshape(db, 4, RET_HEADS, HEAD_DIM), gn)
    mixed = jnp.concatenate([o_nsa.reshape(db, -1), o_ret.reshape(db, -1)], axis=1)
    y_sample = back(hs, mixed).reshape(db, 1, d)
    kv_sample = kv.reshape(1, db, 1, 4, NSA_KV_HEADS, HEAD_DIM)
    win_sample = win_s.reshape(1, db, wb, 2, NSA_KV_HEADS, HEAD_DIM)

    return (y_prompt, y_sample, kv_prompt, kv_sample, win_prompt, win_sample, ret_p[None], ret_s[None])
```

```python
import functools
import math

import numpy as np
import jax
import jax.numpy as jnp
from jax import lax
from jax.experimental import pallas as pl
from jax.experimental.pallas import tpu as pltpu

F32 = jnp.float32
BF16 = jnp.bfloat16

HEAD_DIM = 128
NSA_HEADS = 16
NSA_KV_HEADS = 4
NSA_GROUP = NSA_HEADS // NSA_KV_HEADS
RET_HEADS = 16
CMP_BLOCK = 32
CMP_STRIDE = 16
CMP_HIDDEN = 2 * HEAD_DIM
SLC_BLOCK = 64
SLC_TOPK = 16
WINDOW = 512
RET_CHUNK = 128
EPS = 1e-6
BIG = 1e30
NEG = -1e30
LANE = 128
SUBLANE = 8
VMEM_LIMIT = 56 * 1024 * 1024

GQ_W = NSA_GROUP * HEAD_DIM


def _cparams(sem):
    return pltpu.CompilerParams(dimension_semantics=sem, vmem_limit_bytes=VMEM_LIMIT)


def _tile(n, pref):
    if n <= pref:
        return n
    t = pref
    while n % t:
        t //= 2
    return t


def _dot(a, b):
    return jnp.dot(a, b, preferred_element_type=F32)


def _dot_nt(a, b):
    return lax.dot_general(a, b, (((1,), (1,)), ((), ())), preferred_element_type=F32)


def _gelu_tanh(x):
    return 0.5 * x * (1.0 + jnp.tanh(math.sqrt(2.0 / math.pi) * (x + 0.044715 * (x * x * x))))


def _silu(x):
    return x * (1.0 / (1.0 + jnp.exp(-x)))


def _sigmoid(x):
    return 1.0 / (1.0 + jnp.exp(-x))


def _row_rms(x, g):
    return x * lax.rsqrt(jnp.mean(x * x, axis=-1, keepdims=True) + EPS) * g


def _rmsnorm_kernel(x_ref, g_ref, o_ref):
    o_ref[...] = _row_rms(x_ref[...], g_ref[...]).astype(o_ref.dtype)


def rmsnorm_bf16(x, g):
    t, d = x.shape
    tr = _tile(t, 256)
    return pl.pallas_call(
        _rmsnorm_kernel,
        grid=(t // tr,),
        in_specs=[pl.BlockSpec((tr, d), lambda i: (i, 0)), pl.BlockSpec((1, d), lambda i: (0, 0))],
        out_specs=pl.BlockSpec((tr, d), lambda i: (i, 0)),
        out_shape=jax.ShapeDtypeStruct((t, d), BF16),
        compiler_params=_cparams(("parallel",)),
        name="rmsnorm",
    )(x, g.reshape(1, d))


def _mm_nt_kernel(a_ref, w_ref, o_ref):
    o_ref[...] = _dot_nt(a_ref[...], w_ref[...].astype(BF16))


def matmul_nt(a, wt, row0, n):
    m, k = a.shape
    tm = _tile(m, 1024)
    tn = _tile(n, 512)
    assert wt.shape[0] == 1 and wt.shape[2] == k and row0 % tn == 0
    r0 = row0 // tn
    return pl.pallas_call(
        _mm_nt_kernel,
        grid=(m // tm, n // tn),
        in_specs=[pl.BlockSpec((tm, k), lambda i, j: (i, 0)), pl.BlockSpec((None, tn, k), lambda i, j: (0, r0 + j, 0))],
        out_specs=pl.BlockSpec((tm, tn), lambda i, j: (i, j)),
        out_shape=jax.ShapeDtypeStruct((m, n), F32),
        compiler_params=_cparams(("parallel", "arbitrary")),
        name="in_proj",
    )(a, wt)


def _out_proj_kernel(a1_ref, a2_ref, w_ref, r_ref, o_ref, *, k1):
    acc = _dot(a1_ref[...], w_ref[pl.ds(0, k1), :].astype(BF16))
    acc = acc + _dot(a2_ref[...], w_ref[pl.ds(k1, w_ref.shape[0] - k1), :].astype(BF16))
    o_ref[...] = r_ref[...] + acc


def out_proj(a1, a2, w, res):
    m, k1 = a1.shape
    k = k1 + a2.shape[1]
    n = w.shape[2]
    tm = _tile(m, 1024)
    tn = _tile(n, 512)
    assert w.shape[:2] == (1, k)
    return pl.pallas_call(
        functools.partial(_out_proj_kernel, k1=k1),
        grid=(m // tm, n // tn),
        in_specs=[pl.BlockSpec((tm, k1), lambda i, j: (i, 0)), pl.BlockSpec((tm, k - k1), lambda i, j: (i, 0)),
                  pl.BlockSpec((None, k, tn), lambda i, j: (0, 0, j)), pl.BlockSpec((tm, tn), lambda i, j: (i, j))],
        out_specs=pl.BlockSpec((tm, tn), lambda i, j: (i, j)),
        out_shape=jax.ShapeDtypeStruct((m, n), F32),
        compiler_params=_cparams(("parallel", "arbitrary")),
        name="out_proj",
    )(a1, a2, w, res)


def _ffn_kernel(x_hbm, g_ref, w1_ref, w3_ref, w2_ref, o_ref, n_ref, sem, *, tm, rb):
    i = pl.program_id(0)
    f = pl.program_id(1)

    @pl.when(f == 0)
    def _():
        cp = pltpu.make_async_copy(x_hbm.at[pl.ds(pl.multiple_of(i * tm, tm), tm), :], o_ref, sem.at[0])
        cp.start()
        cp.wait()

        def norm_rows(r, carry):
            rows = pl.ds(pl.multiple_of(r * rb, rb), rb)
            n_ref[rows, :] = _row_rms(o_ref[rows, :], g_ref[...]).astype(BF16)
            return carry

        lax.fori_loop(0, tm // rb, norm_rows, 0)

    n = n_ref[...]
    h = (_silu(_dot(n, w1_ref[...].astype(BF16))) * _dot(n, w3_ref[...].astype(BF16)) * 0.5).astype(BF16)
    o_ref[...] += _dot(h, w2_ref[...].astype(BF16))


def ffn_half_step(x, g, w1, w3, w2, which):
    t, d = x.shape
    dff = w1.shape[3]
    tm = _tile(t, 1024)
    tf = _tile(dff, 256)
    return pl.pallas_call(
        functools.partial(_ffn_kernel, tm=tm, rb=_tile(tm, 64)),
        grid=(t // tm, dff // tf),
        in_specs=[
            pl.BlockSpec(memory_space=pl.ANY),
            pl.BlockSpec((1, d), lambda i, f: (0, 0)),
            pl.BlockSpec((None, None, d, tf), lambda i, f: (0, which, 0, f)),
            pl.BlockSpec((None, None, d, tf), lambda i, f: (0, which, 0, f)),
            pl.BlockSpec((None, None, tf, d), lambda i, f: (0, which, f, 0)),
        ],
        out_specs=pl.BlockSpec((tm, d), lambda i, f: (i, 0), pipeline_mode=pl.Buffered(1)),
        out_shape=jax.ShapeDtypeStruct((t, d), F32),
        scratch_shapes=[pltpu.VMEM((tm, d), BF16), pltpu.SemaphoreType.DMA((1,))],
        compiler_params=_cparams(("parallel", "arbitrary")),
        name="ffn",
    )(x, g.reshape(1, d), w1, w3, w2)


def _headnorm_kernel(x_ref, g_ref, o_ref, *, scale):
    o_ref[...] = (_row_rms(x_ref[...], g_ref[...]) * scale).astype(o_ref.dtype)


def headnorm_bf16(x, col0, nheads, g, scale=1.0):
    t = x.shape[0]
    tr = _tile(t, 1024)
    c0 = col0 // HEAD_DIM
    return pl.pallas_call(
        functools.partial(_headnorm_kernel, scale=scale),
        grid=(t // tr, nheads),
        in_specs=[pl.BlockSpec((tr, HEAD_DIM), lambda i, j: (i, c0 + j)),
                  pl.BlockSpec((1, HEAD_DIM), lambda i, j: (0, 0))],
        out_specs=pl.BlockSpec((tr, HEAD_DIM), lambda i, j: (i, j)),
        out_shape=jax.ShapeDtypeStruct((t, nheads * HEAD_DIM), BF16),
        compiler_params=_cparams(("parallel", "parallel")),
        name="headnorm",
    )(x, g.reshape(1, HEAD_DIM))


def _vt_kernel(x_ref, o_ref):
    o_ref[...] = x_ref[...].T.astype(o_ref.dtype)


def values_transposed(x, col0, b, l):
    tr = _tile(l, 512)
    nl = l // tr
    c0 = col0 // HEAD_DIM
    return pl.pallas_call(
        _vt_kernel,
        grid=(b, NSA_KV_HEADS, nl),
        in_specs=[pl.BlockSpec((tr, HEAD_DIM), lambda bi, h, i: (bi * nl + i, c0 + h))],
        out_specs=pl.BlockSpec((None, None, HEAD_DIM, tr), lambda bi, h, i: (bi, h, 0, i)),
        out_shape=jax.ShapeDtypeStruct((b, NSA_KV_HEADS, HEAD_DIM, l), BF16),
        compiler_params=_cparams(("parallel", "parallel", "parallel")),
        name="values_t",
    )(x)


def _compress_kernel(x_ref, pe_ref, w1_ref, b1_ref, w2_ref, b2_ref, kn_ref, o_ref, xs_ref, *, l, is_k):
    nc = l // CMP_STRIDE
    xs_ref[pl.ds(0, l), :] = x_ref[...]
    xs_ref[pl.ds(l, CMP_BLOCK), :] = jnp.zeros((CMP_BLOCK, HEAD_DIM), F32)
    acc = jnp.zeros((nc, CMP_HIDDEN), F32)
    for r in range(CMP_BLOCK):
        xr = xs_ref[pl.ds(r, nc, stride=CMP_STRIDE), :] + pe_ref[pl.ds(r, 1), :]
        acc = acc + _dot(xr.astype(BF16), w1_ref[r])
    hid = _gelu_tanh(acc + b1_ref[...])
    c = _dot(hid.astype(BF16), w2_ref[...]) + b2_ref[...]
    if is_k:
        o_ref[...] = _row_rms(c, kn_ref[...]).astype(o_ref.dtype)
    else:
        o_ref[...] = c.T.astype(o_ref.dtype)


def compress_prompt(kv, which, b, l, pe, w1, b1, w2, b2, kn):
    nc = l // CMP_STRIDE
    is_k = which == 0
    oshape = (b, NSA_KV_HEADS, nc, HEAD_DIM) if is_k else (b, NSA_KV_HEADS, HEAD_DIM, nc)
    oblock = (None, None, nc, HEAD_DIM) if is_k else (None, None, HEAD_DIM, nc)
    return pl.pallas_call(
        functools.partial(_compress_kernel, l=l, is_k=is_k),
        grid=(b, NSA_KV_HEADS),
        in_specs=[
            pl.BlockSpec((l, HEAD_DIM), lambda bi, h: (bi, which * NSA_KV_HEADS + h)),
            pl.BlockSpec((CMP_BLOCK, HEAD_DIM), lambda bi, h: (0, 0)),
            pl.BlockSpec((CMP_BLOCK, HEAD_DIM, CMP_HIDDEN), lambda bi, h: (0, 0, 0)),
            pl.BlockSpec((1, CMP_HIDDEN), lambda bi, h: (0, 0)),
            pl.BlockSpec((CMP_HIDDEN, HEAD_DIM), lambda bi, h: (0, 0)),
            pl.BlockSpec((1, HEAD_DIM), lambda bi, h: (0, 0)),
            pl.BlockSpec((1, HEAD_DIM), lambda bi, h: (0, 0)),
        ],
        out_specs=pl.BlockSpec(oblock, lambda bi, h: (bi, h, 0, 0)),
        out_shape=jax.ShapeDtypeStruct(oshape, BF16),
        scratch_shapes=[pltpu.VMEM((l + CMP_BLOCK, HEAD_DIM), F32)],
        compiler_params=_cparams(("parallel", "parallel")),
        name="compress_prompt",
    )(kv, pe, w1, b1.reshape(1, -1), w2, b2.reshape(1, -1), kn.reshape(1, -1))


def _select_topk(score, j, ntop):
    ns = score.shape[0]
    bias = jnp.full(score.shape, NEG, F32)
    for _ in range(ntop):
        mx = jnp.max(score, axis=0, keepdims=True)
        idx = jnp.min(jnp.where(score == mx, j, ns), axis=0, keepdims=True)
        hit = j == idx
        bias = jnp.where(hit, 0.0, bias)
        score = jnp.where(hit, -jnp.inf, score)
    return bias


def _cmp_attn_kernel(slope_ref, q_ref, kc_ref, vct_ref, mt_ref, o_ref, sel_ref, *, tq, nc, ns):
    kvh = pl.program_id(1)
    qi = pl.program_id(2)
    qp = qi * tq + lax.broadcasted_iota(jnp.int32, (nc, tq), 1)
    blk_end = lax.broadcasted_iota(jnp.int32, (nc, tq), 0) * CMP_STRIDE + (CMP_BLOCK - 1)
    dist = qp - blk_end
    mask = dist >= 0
    distf = dist.astype(F32)
    kc = kc_ref[...]
    vct = vct_ref[...]
    imp = jnp.zeros((nc, tq), F32)
    for g in range(NSA_GROUP):
        s = _dot_nt(kc, q_ref[:, g * HEAD_DIM:(g + 1) * HEAD_DIM])
        s = jnp.where(mask, s - slope_ref[kvh * NSA_GROUP + g] * distf, -jnp.inf)
        m = jnp.max(s, axis=0, keepdims=True)
        m = jnp.where(m == -jnp.inf, 0.0, m)
        e = jnp.where(mask, jnp.exp(s - m), 0.0)
        p = e / jnp.maximum(jnp.sum(e, axis=0, keepdims=True), 1e-30)
        imp = imp + p
        o_ref[g] = _dot(vct, p.astype(BF16))
    score = jnp.dot(mt_ref[...], imp, preferred_element_type=F32, precision=lax.Precision.HIGHEST)
    j = lax.broadcasted_iota(jnp.int32, (ns, tq), 0)
    cur = (qi * tq + lax.broadcasted_iota(jnp.int32, (ns, tq), 1)) // SLC_BLOCK
    forced = (j == 0) | (j == cur) | (j == cur - 1)
    score = jnp.where(j <= cur, jnp.where(forced, BIG, score), -BIG)
    sel_ref[...] = _select_topk(score, j, min(SLC_TOPK, ns))


def _overlap_matrix(n_cmp, n_slc):
    cs = np.arange(n_cmp)[:, None] * CMP_STRIDE
    ss = np.arange(n_slc)[None, :] * SLC_BLOCK
    ov = np.minimum(cs + CMP_BLOCK, ss + SLC_BLOCK) - np.maximum(cs, ss)
    return (np.maximum(ov, 0).astype(np.float32) / CMP_BLOCK)


def cmp_attention_prompt(qn, kc, vct, slopes, b, l):
    nc = l // CMP_STRIDE
    ns = l // SLC_BLOCK
    tq = _tile(l, 512)
    nq = l // tq
    mt = jnp.asarray(_overlap_matrix(nc, ns).T)
    return pl.pallas_call(
        functools.partial(_cmp_attn_kernel, tq=tq, nc=nc, ns=ns),
        grid=(b, NSA_KV_HEADS, nq),
        in_specs=[
            pl.BlockSpec(memory_space=pltpu.SMEM),
            pl.BlockSpec((tq, GQ_W), lambda bi, h, i: (bi * nq + i, h)),
            pl.BlockSpec((None, None, nc, HEAD_DIM), lambda bi, h, i: (bi, h, 0, 0)),
            pl.BlockSpec((None, None, HEAD_DIM, nc), lambda bi, h, i: (bi, h, 0, 0)),
            pl.BlockSpec((ns, nc), lambda bi, h, i: (0, 0)),
        ],
        out_specs=[
            pl.BlockSpec((None, NSA_GROUP, HEAD_DIM, tq), lambda bi, h, i: (bi, h, 0, i)),
            pl.BlockSpec((None, None, ns, tq), lambda bi, h, i: (bi, h, 0, i)),
        ],
        out_shape=[jax.ShapeDtypeStruct((b, NSA_HEADS, HEAD_DIM, l), F32),
                   jax.ShapeDtypeStruct((b, NSA_KV_HEADS, ns, l), F32)],
        compiler_params=_cparams(("parallel", "parallel", "parallel")),
        name="cmp_attention",
    )(slopes, qn, kc, vct, mt)


def _flash_kernel(slope_ref, q_ref, k_ref, vt_ref, *rest, tq, tk, nkt, windowed):
    if windowed:
        g_ref, oc_ref, os_ref, o_ref, m_ref, l_ref, acc_ref = rest
    else:
        sel_ref, o_ref, m_ref, l_ref, acc_ref = rest
    kvh = pl.program_id(1)
    qi = pl.program_id(2)
    kt = pl.program_id(3)
    if windowed:
        ktile = qi - (nkt - 1) + kt
        valid = ktile >= 0
    else:
        ktile = kt
        valid = kt * tk <= qi * tq + (tq - 1)

    @pl.when(kt == 0)
    def _():
        m_ref[...] = jnp.full(m_ref.shape, NEG, F32)
        l_ref[...] = jnp.zeros(l_ref.shape, F32)
        acc_ref[...] = jnp.zeros(acc_ref.shape, F32)

    @pl.when(valid)
    def _():
        kpos = ktile * tk + lax.broadcasted_iota(jnp.int32, (tk, tq), 0)
        qpos = qi * tq + lax.broadcasted_iota(jnp.int32, (tk, tq), 1)
        dist = qpos - kpos
        distf = dist.astype(F32)
        if windowed:
            bias = jnp.where((dist >= 0) & (dist < WINDOW), 0.0, NEG)
        else:
            sel = sel_ref[...]
            sel = jnp.concatenate(
                [jnp.broadcast_to(sel[r:r + 1, :], (SLC_BLOCK, tq)) for r in range(tk // SLC_BLOCK)], axis=0)
            bias = jnp.where(dist >= 0, sel, NEG)
        k = k_ref[...]
        vt = vt_ref[...]
        for g in range(NSA_GROUP):
            s = _dot_nt(k, q_ref[:, g * HEAD_DIM:(g + 1) * HEAD_DIM])
            s = s - slope_ref[kvh * NSA_GROUP + g] * distf + bias
            m_prev = m_ref[pl.ds(g, 1), :]
            m_new = jnp.maximum(m_prev, jnp.max(s, axis=0, keepdims=True))
            alpha = jnp.exp(m_prev - m_new)
            e = jnp.exp(s - m_new)
            l_ref[pl.ds(g, 1), :] = alpha * l_ref[pl.ds(g, 1), :] + jnp.sum(e, axis=0, keepdims=True)
            acc_ref[g] = alpha * acc_ref[g] + _dot(vt, e.astype(BF16))
            m_ref[pl.ds(g, 1), :] = m_new

    @pl.when(kt == nkt - 1)
    def _():
        if windowed:
            gates = _sigmoid(g_ref[...])
        for g in range(NSA_GROUP):
            o = acc_ref[g] / jnp.maximum(l_ref[pl.ds(g, 1), :], 1e-30)
            if windowed:
                o = (gates[g:g + 1, :] * oc_ref[g] + gates[NSA_GROUP + g:NSA_GROUP + g + 1, :] * os_ref[g]
                     + gates[2 * NSA_GROUP + g:2 * NSA_GROUP + g + 1, :] * o)
                o_ref[:, g * HEAD_DIM:(g + 1) * HEAD_DIM] = o.T.astype(o_ref.dtype)
            else:
                o_ref[g] = o


def flash_prompt(qn, kn, vt, slopes, b, l, sel=None, gated=None):
    windowed = sel is None
    tq = _tile(l, 512)
    tk = tq
    nq = l // tq
    nkt = (min(WINDOW, l) // tk + 1) if windowed else l // tk
    if windowed:
        kmap = lambda qi, kt: jnp.maximum(qi - (nkt - 1) + kt, 0)
    else:
        kmap = lambda qi, kt: jnp.minimum(kt, (qi * tq + tq - 1) // tk)
    in_specs = [
        pl.BlockSpec(memory_space=pltpu.SMEM),
        pl.BlockSpec((tq, GQ_W), lambda bi, h, qi, kt: (bi * nq + qi, h)),
        pl.BlockSpec((tk, HEAD_DIM), lambda bi, h, qi, kt: (bi * nq + kmap(qi, kt), h)),
        pl.BlockSpec((None, None, HEAD_DIM, tk), lambda bi, h, qi, kt: (bi, h, 0, kmap(qi, kt))),
    ]
    args = [slopes, qn, kn, vt]
    tspec = pl.BlockSpec((None, NSA_GROUP, HEAD_DIM, tq), lambda bi, h, qi, kt: (bi, h, 0, qi))
    if windowed:
        gates, o_cmp, o_slc = gated
        ngr = 3 * NSA_GROUP
        gk = gates[:, :3 * NSA_HEADS].reshape(b, l, 3, NSA_KV_HEADS, NSA_GROUP).transpose(0, 3, 2, 4, 1)
        gk = jnp.pad(gk.reshape(b, NSA_KV_HEADS, ngr, l), ((0, 0), (0, 0), (0, -ngr % SUBLANE), (0, 0)))
        in_specs += [pl.BlockSpec((None, None, gk.shape[2], tq), lambda bi, h, qi, kt: (bi, h, 0, qi)), tspec, tspec]
        args += [gk, o_cmp, o_slc]
        out_spec = pl.BlockSpec((tq, GQ_W), lambda bi, h, qi, kt: (bi * nq + qi, h))
        out_shape = jax.ShapeDtypeStruct((b * l, NSA_HEADS * HEAD_DIM), BF16)
    else:
        in_specs.append(pl.BlockSpec((None, None, tk // SLC_BLOCK, tq),
                                     lambda bi, h, qi, kt: (bi, h, kmap(qi, kt), qi)))
        args.append(sel)
        out_spec = tspec
        out_shape = jax.ShapeDtypeStruct((b, NSA_HEADS, HEAD_DIM, l), F32)
    return pl.pallas_call(
        functools.partial(_flash_kernel, tq=tq, tk=tk, nkt=nkt, windowed=windowed),
        grid=(b, NSA_KV_HEADS, nq, nkt),
        in_specs=in_specs,
        out_specs=out_spec,
        out_shape=out_shape,
        scratch_shapes=[pltpu.VMEM((SUBLANE, tq), F32), pltpu.VMEM((SUBLANE, tq), F32),
                        pltpu.VMEM((NSA_GROUP, HEAD_DIM, tq), F32)],
        compiler_params=_cparams(("parallel", "parallel", "parallel", "arbitrary")),
        name="window_attention" if windowed else "selected_attention",
    )(*args)


def _group_norm_gate(o, rg, gn):
    mu = jnp.mean(o, axis=-1, keepdims=True)
    var = jnp.mean(jnp.square(o - mu), axis=-1, keepdims=True)
    return _silu(rg) * ((o - mu) * lax.rsqrt(var + EPS) * gn)


def _retention_kernel(q_ref, k_ref, v_ref, rg_ref, intra_ref, qd_ref, kd_ref, cd_ref, gn_ref,
                      o_ref, s_ref, st_ref, *, tl, hb):
    li = pl.program_id(2)
    nl = pl.num_programs(2)

    @pl.when(li == 0)
    def _():
        st_ref[...] = jnp.zeros(st_ref.shape, F32)

    def chunk(c, carry):
        rows = pl.ds(pl.multiple_of(c * RET_CHUNK, RET_CHUNK), RET_CHUNK)
        for h in range(hb):
            cols = slice(h * HEAD_DIM, (h + 1) * HEAD_DIM)
            qc = q_ref[rows, cols]
            kc = k_ref[rows, cols] * (HEAD_DIM ** -0.5)
            vc = v_ref[rows, cols].astype(BF16)
            state = st_ref[h]
            att = _dot_nt(qc.astype(BF16), kc.astype(BF16)) * intra_ref[h]
            o = _dot(att.astype(BF16), vc) + _dot((qc * qd_ref[h]).astype(BF16), state.astype(BF16))
            st_ref[h] = state * cd_ref[h] + _dot((kc * kd_ref[h]).T.astype(BF16), vc)
            o_ref[rows, cols] = _group_norm_gate(o, rg_ref[rows, cols], gn_ref[:, cols]).astype(o_ref.dtype)
        return carry

    lax.fori_loop(0, tl // RET_CHUNK, chunk, 0)

    @pl.when(li == nl - 1)
    def _():
        s_ref[...] = st_ref[...]


def _retention_tables(chunk):
    h = jnp.arange(RET_HEADS, dtype=F32)
    log_g = jnp.log1p(-jnp.exp2(-5.0 - h))
    i = jnp.arange(chunk, dtype=F32)
    diff = i[:, None] - i[None, :]
    intra = jnp.where(diff >= 0, jnp.exp(jnp.maximum(diff, 0.0)[None] * log_g[:, None, None]), 0.0)
    q_dec = jnp.exp((i[None, :] + 1.0) * log_g[:, None])
    k_dec = jnp.exp((chunk - 1.0 - i)[None, :] * log_g[:, None])
    c_dec = jnp.exp(chunk * log_g)
    return intra, q_dec, k_dec, c_dec


def retention_prompt(r, gn, b, l):
    intra, q_dec, k_dec, c_dec = _retention_tables(RET_CHUNK)
    bc = lambda t: jnp.broadcast_to(t[:, :, None], (RET_HEADS, RET_CHUNK, HEAD_DIM))
    qd, kd = bc(q_dec), bc(k_dec)
    cd = jnp.broadcast_to(c_dec[:, None, None], (RET_HEADS, HEAD_DIM, HEAD_DIM))
    hb = 4
    ng = RET_HEADS // hb
    tl = _tile(l, 1024)
    nl = l // tl
    w = hb * HEAD_DIM
    rspec = lambda part: pl.BlockSpec((tl, w), lambda bi, g, li: (bi * nl + li, part * ng + g))
    tspec = pl.BlockSpec((hb, RET_CHUNK, HEAD_DIM), lambda bi, g, li: (g, 0, 0))
    return pl.pallas_call(
        functools.partial(_retention_kernel, tl=tl, hb=hb),
        grid=(b, ng, nl),
        in_specs=[rspec(0), rspec(1), rspec(2), rspec(3), tspec, tspec, tspec, tspec,
                  pl.BlockSpec((1, w), lambda bi, g, li: (0, g))],
        out_specs=[pl.BlockSpec((tl, w), lambda bi, g, li: (bi * nl + li, g)),
                   pl.BlockSpec((None, hb, HEAD_DIM, HEAD_DIM), lambda bi, g, li: (bi, g, 0, 0))],
        out_shape=[jax.ShapeDtypeStruct((b * l, RET_HEADS * HEAD_DIM), BF16),
                   jax.ShapeDtypeStruct((b, RET_HEADS, HEAD_DIM, HEAD_DIM), F32)],
        scratch_shapes=[pltpu.VMEM((hb, HEAD_DIM, HEAD_DIM), F32)],
        compiler_params=_cparams(("parallel", "parallel", "arbitrary")),
        name="retention_prompt",
    )(r, r, r, r, intra, qd, kd, cd, gn.reshape(1, -1))


def _group_rows(kvh):
    row = lax.broadcasted_iota(jnp.int32, (NSA_HEADS, 1), 0)
    return (row >= kvh * NSA_GROUP) & (row < (kvh + 1) * NSA_GROUP)


def _sample_cmp_kernel(pt_ref, *refs, pp, npg, pos0):
    pages = refs[:pp]
    (q_ref, slope_ref, pea_ref, peb_ref, wa_ref, wb_ref, b1_ref, w2_ref, b2_ref, kn_ref, m_ref,
     o_ref, idx_ref, xa_ref, xb_ref, f_ref, s_ref, c_ref) = refs[pp:]
    jp = pl.program_id(1)
    cpp = PAGE_ROWS // CMP_STRIDE
    nck = npg * cpp
    ng = 2 * NSA_KV_HEADS
    rows = pp * cpp * ng
    nrow = nck * ng

    def is_k(shape):
        return (lax.broadcasted_iota(jnp.int32, shape, 0) & NSA_KV_HEADS) == 0

    @pl.when(jp == 0)
    def _():
        s_ref[pl.ds(nrow, ng), :] = jnp.zeros((ng, CMP_HIDDEN), F32)

    for t in range(pp):
        for i in range(cpp):
            for c in range(CMP_STRIDE):
                dst = (pl.ds((t * cpp + i) * ng, ng), pl.ds(c * HEAD_DIM, HEAD_DIM))
                row = pages[t][i * CMP_STRIDE + c]
                xa_ref[dst] = row + pea_ref[:, c * HEAD_DIM:(c + 1) * HEAD_DIM]
                xb_ref[dst] = row + peb_ref[:, c * HEAD_DIM:(c + 1) * HEAD_DIM]
    fa = _dot(xa_ref[...].astype(BF16), wa_ref[...])
    sb = _dot(xb_ref[...].astype(BF16), wb_ref[...])
    kmask = is_k((rows, CMP_HIDDEN))
    dst = pl.ds(pl.multiple_of(jp * rows, rows), rows)
    f_ref[dst, :] = jnp.where(kmask, fa[:, :CMP_HIDDEN], fa[:, CMP_HIDDEN:])
    s_ref[dst, :] = jnp.where(kmask, sb[:, :CMP_HIDDEN], sb[:, CMP_HIDDEN:])

    @pl.when(jp == pl.num_programs(1) - 1)
    def _():
        eb = min(nrow, 1024)

        def cblock(r, carry):
            lo = pl.multiple_of(r * eb, eb)
            b1 = jnp.where(is_k((eb, CMP_HIDDEN)), b1_ref[0:1, :], b1_ref[1:2, :])
            hid = _gelu_tanh(f_ref[pl.ds(lo, eb), :] + s_ref[pl.ds(lo + ng, eb), :] + b1)
            c2 = _dot(hid.astype(BF16), w2_ref[...])
            c_ref[pl.ds(lo, eb), :] = jnp.where(is_k((eb, HEAD_DIM)), c2[:, :HEAD_DIM] + b2_ref[0:1, :],
                                                c2[:, HEAD_DIM:] + b2_ref[1:2, :])
            return carry

        lax.fori_loop(0, nrow // eb, cblock, 0)

        q = q_ref[...]
        n_io = lax.broadcasted_iota(jnp.int32, (NSA_HEADS, nck), 1)
        dist = pos0 - (n_io * CMP_STRIDE + (CMP_BLOCK - 1))
        mask = dist >= 0
        bias = slope_ref[:, :nck] * dist.astype(F32)
        o_acc = jnp.zeros((NSA_HEADS, HEAD_DIM), F32)
        imp_rows = []
        for kvh in range(NSA_KV_HEADS):
            kc = _row_rms(c_ref[pl.ds(kvh, nck, stride=ng), :], kn_ref[...]).astype(BF16)
            vc = c_ref[pl.ds(NSA_KV_HEADS + kvh, nck, stride=ng), :].astype(BF16)
            s = jnp.where(mask, _dot_nt(q, kc) - bias, -jnp.inf)
            m = jnp.max(s, axis=1, keepdims=True)
            m = jnp.where(m == -jnp.inf, 0.0, m)
            e = jnp.where(mask, jnp.exp(s - m), 0.0)
            p = e / jnp.maximum(jnp.sum(e, axis=1, keepdims=True), 1e-30)
            grp = _group_rows(kvh)
            o_acc = o_acc + jnp.where(grp, _dot(p.astype(BF16), vc), 0.0)
            imp_rows.append(jnp.sum(jnp.where(grp, p, 0.0), axis=0, keepdims=True))
        o_ref[...] = o_acc
        imp = jnp.concatenate(imp_rows + [jnp.zeros((SUBLANE - NSA_KV_HEADS, nck), F32)], axis=0)
        score = jnp.dot(imp, m_ref[...], preferred_element_type=F32, precision=lax.Precision.HIGHEST)
        nsp = score.shape[1]
        j = lax.broadcasted_iota(jnp.int32, (SUBLANE, nsp), 1)
        cur = pos0 // SLC_BLOCK
        forced = (j == 0) | (j == cur) | (j == cur - 1)
        score = jnp.where(j <= cur, jnp.where(forced, BIG, score), -BIG)
        lane = lax.broadcasted_iota(jnp.int32, (SUBLANE, LANE), 1)
        out = jnp.zeros((SUBLANE, LANE), jnp.int32)
        for t in range(SLC_TOPK):
            mx = jnp.max(score, axis=1, keepdims=True)
            idx = jnp.min(jnp.where(score == mx, j, nsp), axis=1, keepdims=True)
            out = jnp.where(lane == t, idx, out)
            score = jnp.where(j == idx, -jnp.inf, score)
        idx_ref[...] = out


PAGE_ROWS = 128


def sample_cmp_select(cache4, page_table, qn3, slopes_b, pe, w1, b1, w2, b2, kn, pos0):
    db, npg = page_table.shape
    pp = _tile(npg, 8)
    cpp = PAGE_ROWS // CMP_STRIDE
    nck = npg * cpp
    ng = 2 * NSA_KV_HEADS
    n_slc = (pos0 + 1 + SLC_BLOCK - 1) // SLC_BLOCK
    nsp = -(-n_slc // LANE) * LANE
    m = np.zeros((nck, nsp), np.float32)
    m[:, :n_slc] = _overlap_matrix(nck, n_slc)
    kdim = CMP_STRIDE * HEAD_DIM
    pe_rows = lambda a: jnp.repeat(a.reshape(2, 1, kdim), NSA_KV_HEADS, axis=1).reshape(ng, kdim)
    pea, peb = pe_rows(pe[:, :CMP_STRIDE]), pe_rows(pe[:, CMP_STRIDE:])
    side = lambda a: jnp.concatenate([a[0], a[1]], axis=-1)
    wa = side(w1[:, :CMP_STRIDE].reshape(2, kdim, CMP_HIDDEN))
    wb = side(w1[:, CMP_STRIDE:].reshape(2, kdim, CMP_HIDDEN))
    const = lambda shape: pl.BlockSpec(shape, lambda bi, jp, pt: (0,) * len(shape))
    page_specs = [
        pl.BlockSpec((None, PAGE_ROWS, ng, HEAD_DIM),
                     functools.partial(lambda bi, jp, pt, t: (pt[bi, jp * pp + t], 0, 0, 0), t=t))
        for t in range(pp)]
    grid_spec = pltpu.PrefetchScalarGridSpec(
        num_scalar_prefetch=1,
        grid=(db, npg // pp),
        in_specs=page_specs + [
            pl.BlockSpec((None, NSA_HEADS, HEAD_DIM), lambda bi, jp, pt: (bi, 0, 0)),
            const((NSA_HEADS, slopes_b.shape[1])),
            const((ng, kdim)), const((ng, kdim)),
            const((kdim, 2 * CMP_HIDDEN)), const((kdim, 2 * CMP_HIDDEN)),
            const((2, CMP_HIDDEN)), const((CMP_HIDDEN, 2 * HEAD_DIM)), const((2, HEAD_DIM)),
            const((1, HEAD_DIM)), const((nck, nsp)),
        ],
        out_specs=[pl.BlockSpec((None, NSA_HEADS, HEAD_DIM), lambda bi, jp, pt: (bi, 0, 0)),
                   pl.BlockSpec((None, SUBLANE, LANE), lambda bi, jp, pt: (bi, 0, 0))],
        scratch_shapes=[
            pltpu.VMEM((pp * cpp * ng, kdim), F32),
            pltpu.VMEM((pp * cpp * ng, kdim), F32),
            pltpu.VMEM((nck * ng, CMP_HIDDEN), F32),
            pltpu.VMEM((nck * ng + ng, CMP_HIDDEN), F32),
            pltpu.VMEM((nck * ng, HEAD_DIM), F32),
        ],
    )
    return pl.pallas_call(
        functools.partial(_sample_cmp_kernel, pp=pp, npg=npg, pos0=pos0),
        grid_spec=grid_spec,
        out_shape=[jax.ShapeDtypeStruct((db, NSA_HEADS, HEAD_DIM), F32),
                   jax.ShapeDtypeStruct((db, SUBLANE, LANE), jnp.int32)],
        compiler_params=_cparams(("parallel", "arbitrary")),
        name="sample_cmp_select",
    )(page_table, *([cache4] * pp), qn3, slopes_b, pea, peb, wa, wb, b1, side(w2), b2, kn.reshape(1, -1),
      jnp.asarray(m))


def _softmax_with_new_key(s, s_new):
    m = jnp.maximum(jnp.max(s, axis=1, keepdims=True), s_new)
    e = jnp.exp(s - m)
    e_new = jnp.exp(s_new - m)
    return e, e_new, jnp.maximum(jnp.sum(e, axis=1, keepdims=True) + e_new, 1e-30)


def _bf16_round(x):
    return x.astype(BF16).astype(F32)


def _sample_slc_kernel(row_ref, cache_ref, q_ref, kpos_ref, slope_ref, new_ref, kn_ref, o_ref, kbuf, vbuf, sem, *, pos0):
    bi = pl.program_id(0)
    nb = pl.num_programs(0)
    nsel = SLC_TOPK * SLC_BLOCK
    kcol = 2 * NSA_KV_HEADS
    vcol = 3 * NSA_KV_HEADS

    def gather(b, slot, start):
        for kvh in range(NSA_KV_HEADS):
            for t in range(SLC_TOPK):
                src = pl.ds(row_ref[(b * NSA_KV_HEADS + kvh) * SLC_TOPK + t], SLC_BLOCK)
                dst = pl.ds(t * SLC_BLOCK, SLC_BLOCK)
                for col, buf in ((kcol, kbuf), (vcol, vbuf)):
                    cp = pltpu.make_async_copy(cache_ref.at[src, col + kvh, :], buf.at[slot, kvh, dst, :], sem.at[slot])
                    if start:
                        cp.start()
                    else:
                        cp.wait()

    slot = bi % 2

    @pl.when(bi == 0)
    def _():
        gather(bi, 0, True)

    @pl.when(bi + 1 < nb)
    def _():
        gather(bi + 1, 1 - slot, True)

    gather(bi, slot, False)

    q = q_ref[...]
    o_acc = jnp.zeros((NSA_HEADS, HEAD_DIM), F32)
    for kvh in range(NSA_KV_HEADS):
        kn = _row_rms(kbuf[slot, kvh], kn_ref[...]).astype(BF16)
        v = vbuf[slot, kvh]
        dist = pos0 - kpos_ref[pl.ds(kvh, 1), :]
        s = _dot_nt(q, kn) - slope_ref[:, :nsel] * dist.astype(F32)
        s = jnp.where(dist >= 0, s, NEG)
        knew = _row_rms(new_ref[pl.ds(kvh, 1), :], kn_ref[...])
        s_new = jnp.sum(q.astype(F32) * _bf16_round(knew), axis=1, keepdims=True)
        e, e_new, l = _softmax_with_new_key(s, s_new)
        vnew = new_ref[pl.ds(NSA_KV_HEADS + kvh, 1), :]
        o = _dot(e.astype(BF16), v.astype(BF16)) + _bf16_round(e_new) * _bf16_round(vnew)
        o_acc = o_acc + jnp.where(_group_rows(kvh), o / l, 0.0)
    o_ref[...] = o_acc


def sample_slc_attention(cache3, row0, qn3, kpos, slopes_b, new_kv, kn, pos0):
    db = qn3.shape[0]
    nsel = SLC_TOPK * SLC_BLOCK
    ng = 2 * NSA_KV_HEADS
    grid_spec = pltpu.PrefetchScalarGridSpec(
        num_scalar_prefetch=1,
        grid=(db,),
        in_specs=[
            pl.BlockSpec(memory_space=pl.ANY),
            pl.BlockSpec((None, NSA_HEADS, HEAD_DIM), lambda bi, r: (bi, 0, 0)),
            pl.BlockSpec((None, NSA_KV_HEADS, nsel), lambda bi, r: (bi, 0, 0)),
            pl.BlockSpec((NSA_HEADS, slopes_b.shape[1]), lambda bi, r: (0, 0)),
            pl.BlockSpec((None, ng, HEAD_DIM), lambda bi, r: (bi, 0, 0)),
            pl.BlockSpec((1, HEAD_DIM), lambda bi, r: (0, 0)),
        ],
        out_specs=pl.BlockSpec((None, NSA_HEADS, HEAD_DIM), lambda bi, r: (bi, 0, 0)),
        scratch_shapes=[pltpu.VMEM((2, NSA_KV_HEADS, nsel, HEAD_DIM), F32),
                        pltpu.VMEM((2, NSA_KV_HEADS, nsel, HEAD_DIM), F32),
                        pltpu.SemaphoreType.DMA((2,))],
    )
    return pl.pallas_call(
        functools.partial(_sample_slc_kernel, pos0=pos0),
        grid_spec=grid_spec,
        out_shape=jax.ShapeDtypeStruct((db, NSA_HEADS, HEAD_DIM), F32),
        compiler_params=_cparams(("arbitrary",)),
        name="sample_slc_attention",
    )(row0, cache3, qn3, kpos, slopes_b, new_kv, kn.reshape(1, -1))


def _sample_win_kernel(sw_ref, q_ref, slope_ref, new_ref, kn_ref, g_ref, oc_ref, os_ref, o_ref, nw_ref, *, wb):
    q = q_ref[...]
    row = lax.broadcasted_iota(jnp.int32, (NSA_HEADS, wb), 1)
    dist = wb - row
    valid = dist < WINDOW
    bias = slope_ref[:, :wb] * dist.astype(F32)
    o_acc = jnp.zeros((NSA_HEADS, HEAD_DIM), F32)
    for kvh in range(NSA_KV_HEADS):
        k = sw_ref[:, kvh, :]
        v = sw_ref[:, NSA_KV_HEADS + kvh, :]
        kn = _row_rms(k, kn_ref[...]).astype(BF16)
        s = jnp.where(valid, _dot_nt(q, kn) - bias, NEG)
        knew = _row_rms(new_ref[pl.ds(kvh, 1), :], kn_ref[...])
        s_new = jnp.sum(q.astype(F32) * _bf16_round(knew), axis=1, keepdims=True)
        e, e_new, l = _softmax_with_new_key(s, s_new)
        vnew = new_ref[pl.ds(NSA_KV_HEADS + kvh, 1), :]
        o = _dot(e.astype(BF16), v.astype(BF16)) + _bf16_round(e_new) * _bf16_round(vnew)
        o_acc = o_acc + jnp.where(_group_rows(kvh), o / l, 0.0)
    g = _sigmoid(g_ref[...])
    o_ref[...] = (g[0] * oc_ref[...] + g[1] * os_ref[...] + g[2] * o_acc).astype(o_ref.dtype)
    nw_ref[pl.ds(0, wb - 1)] = sw_ref[pl.ds(1, wb - 1)]
    nw_ref[wb - 1] = new_ref[...]


def sample_win_attention(sw, qn3, slopes_b, win_new, kn, gates_b, o_cmp, o_slc):
    db, wb, ng, _ = sw.shape
    hspec = pl.BlockSpec((None, NSA_HEADS, HEAD_DIM), lambda bi: (bi, 0, 0))
    wspec = pl.BlockSpec((None, wb, ng, HEAD_DIM), lambda bi: (bi, 0, 0, 0))
    return pl.pallas_call(
        functools.partial(_sample_win_kernel, wb=wb),
        grid=(db,),
        in_specs=[
            wspec,
            hspec,
            pl.BlockSpec((NSA_HEADS, slopes_b.shape[1]), lambda bi: (0, 0)),
            pl.BlockSpec((None, ng, HEAD_DIM), lambda bi: (bi, 0, 0)),
            pl.BlockSpec((1, HEAD_DIM), lambda bi: (0, 0)),
            pl.BlockSpec((None, 3, NSA_HEADS, HEAD_DIM), lambda bi: (bi, 0, 0, 0)),
            hspec, hspec,
        ],
        out_specs=[hspec, wspec],
        out_shape=[jax.ShapeDtypeStruct((db, NSA_HEADS, HEAD_DIM), BF16),
                   jax.ShapeDtypeStruct((db, wb, ng, HEAD_DIM), F32)],
        compiler_params=_cparams(("parallel",)),
        name="sample_win_attention",
    )(sw, qn3, slopes_b, win_new, kn.reshape(1, -1), gates_b, o_cmp, o_slc)


def _sample_ret_kernel(st_ref, r_ref, gam_ref, gn_ref, o_ref, s_ref):
    q = r_ref[0]
    k = r_ref[1] * (HEAD_DIM ** -0.5)
    v = r_ref[2]
    rg = r_ref[3]
    gam = gam_ref[...]
    qb, kb, vb = _bf16_round(q), _bf16_round(k), _bf16_round(v)
    att = jnp.sum(qb * kb, axis=1, keepdims=True)
    kt = kb.T
    qd = (q * gam).astype(BF16)
    rows = []
    for h in range(RET_HEADS):
        state = st_ref[h]
        rows.append(_dot(qd[h:h + 1, :], state.astype(BF16)))
        s_ref[h] = state * gam[h:h + 1, :] + kt[:, h:h + 1] * vb[h:h + 1, :]
    o = _bf16_round(att) * vb + jnp.concatenate(rows, axis=0)
    o_ref[...] = _group_norm_gate(o, rg, gn_ref[...]).astype(o_ref.dtype)


def sample_retention(state, r4, gn):
    db = state.shape[0]
    _, q_dec, _, _ = _retention_tables(1)
    gam = jnp.broadcast_to(q_dec, (RET_HEADS, HEAD_DIM))
    return pl.pallas_call(
        _sample_ret_kernel,
        grid=(db,),
        in_specs=[
            pl.BlockSpec((None, RET_HEADS, HEAD_DIM, HEAD_DIM), lambda bi: (bi, 0, 0, 0)),
            pl.BlockSpec((None, 4, RET_HEADS, HEAD_DIM), lambda bi: (bi, 0, 0, 0)),
            pl.BlockSpec((RET_HEADS, HEAD_DIM), lambda bi: (0, 0)),
            pl.BlockSpec((RET_HEADS, HEAD_DIM), lambda bi: (0, 0)),
        ],
        out_specs=[pl.BlockSpec((None, RET_HEADS, HEAD_DIM), lambda bi: (bi, 0, 0)),
                   pl.BlockSpec((None, RET_HEADS, HEAD_DIM, HEAD_DIM), lambda bi: (bi, 0, 0, 0))],
        out_shape=[jax.ShapeDtypeStruct((db, RET_HEADS, HEAD_DIM), BF16),
                   jax.ShapeDtypeStruct((db, RET_HEADS, HEAD_DIM, HEAD_DIM), F32)],
        compiler_params=_cparams(("parallel",)),
        name="sample_retention",
    )(state, r4, gam, gn.reshape(RET_HEADS, HEAD_DIM))


def _alibi_slopes():
    i = np.arange(NSA_HEADS, dtype=np.float32)
    return np.exp2(-8.0 * (i + 1.0) / NSA_HEADS).astype(np.float32)


def _in_proj(n, w_t, w_ret_t):
    nq = NSA_HEADS * HEAD_DIM
    nkv = 4 * NSA_KV_HEADS * HEAD_DIM
    nwin = 2 * NSA_KV_HEADS * HEAD_DIM
    ng = 3 * NSA_HEADS
    nr = 4 * RET_HEADS * HEAD_DIM
    assert w_t.shape[1] == nq + nkv + nwin + ng + nr and ng <= LANE
    o_g = nq + nkv + nwin
    return [matmul_nt(n, w_t, 0, nq), matmul_nt(n, w_t, nq, nkv), matmul_nt(n, w_t, nq + nkv, nwin),
            matmul_nt(n, w_t, o_g, LANE), matmul_nt(n, w_ret_t, 0, nr)]


def kernel(x_prompt, x_sample, cache_nsa_kv, state_win_kv, state_ret, page_table, norm_g, ffn_w1, ffn_w3, ffn_w2,
           w_in, nsa_q_norm, nsa_k_norm, cmp_pe, cmp_w1, cmp_b1, cmp_w2, cmp_b2, ret_gn_g, w_out):
    assert cache_nsa_kv.shape[0] == 1 and x_sample.shape[1] == 1
    b, l, d = x_prompt.shape
    db = x_sample.shape[0]
    npg = page_table.shape[1]
    n_pool = cache_nsa_kv.shape[1]
    assert cache_nsa_kv.shape[2] == PAGE_ROWS
    pos0 = npg * PAGE_ROWS
    half = NSA_KV_HEADS * HEAD_DIM

    w_t = jnp.swapaxes(w_in, 1, 2)
    w_ret_t = w_t[:, w_t.shape[1] - 4 * RET_HEADS * HEAD_DIM:]
    ng = norm_g[0]
    qg, kg = nsa_q_norm[0], nsa_k_norm[0]
    pe = cmp_pe[0]
    cw1 = cmp_w1[0].astype(BF16)
    cb1 = cmp_b1[0]
    cw2 = cmp_w2[0].astype(BF16)
    cb2 = cmp_b2[0]
    gn = ret_gn_g[0]
    slopes = jnp.asarray(_alibi_slopes())
    q_scale = HEAD_DIM ** -0.5

    def front(x):
        h = ffn_half_step(x, ng[0], ffn_w1, ffn_w3, ffn_w2, 0)
        return h, _in_proj(rmsnorm_bf16(h, ng[1]), w_t, w_ret_t)

    def back(h, o_nsa, o_ret):
        h = out_proj(o_nsa, o_ret, w_out, h)
        return ffn_half_step(h, ng[2], ffn_w1, ffn_w3, ffn_w2, 1)

    hp, (q, kv, win, gates, r) = front(x_prompt.reshape(b * l, d))
    qn = headnorm_bf16(q, 0, NSA_HEADS, qg, q_scale)
    kc = compress_prompt(kv, 0, b, l, pe[0], cw1[0], cb1[0], cw2[0], cb2[0], kg[0])
    vct = compress_prompt(kv, 1, b, l, pe[1], cw1[1], cb1[1], cw2[1], cb2[1], kg[0])
    o_cmp, sel = cmp_attention_prompt(qn, kc, vct, slopes, b, l)
    k_slc = headnorm_bf16(kv, 2 * half, NSA_KV_HEADS, kg[1])
    v_slc = values_transposed(kv, 3 * half, b, l)
    o_slc = flash_prompt(qn, k_slc, v_slc, slopes, b, l, sel=sel)
    k_win = headnorm_bf16(win, 0, NSA_KV_HEADS, kg[2])
    v_win = values_transposed(win, half, b, l)
    o_nsa = flash_prompt(qn, k_win, v_win, slopes, b, l, gated=(gates, o_cmp, o_slc))
    o_ret, ret_p = retention_prompt(r, gn, b, l)
    y_prompt = back(hp, o_nsa, o_ret).reshape(b, l, d)
    kv_prompt = kv.reshape(1, b, l, 4, NSA_KV_HEADS, HEAD_DIM)
    wl = min(WINDOW, l)
    win_prompt = win.reshape(b, l, 2, NSA_KV_HEADS, HEAD_DIM)[:, l - wl:][None]

    hs, (q, kv, win, gates, r) = front(x_sample.reshape(db, d))
    qn3 = headnorm_bf16(q, 0, NSA_HEADS, qg, q_scale).reshape(db, NSA_HEADS, HEAD_DIM)
    slopes_b = jnp.broadcast_to(slopes[:, None], (NSA_HEADS, max(npg * (PAGE_ROWS // CMP_STRIDE),
                                                                 SLC_TOPK * SLC_BLOCK, state_win_kv.shape[2])))
    o_cmp, idx = sample_cmp_select(cache_nsa_kv.reshape(n_pool, PAGE_ROWS, 4 * NSA_KV_HEADS, HEAD_DIM), page_table,
                                   qn3, slopes_b, pe, cw1, cb1, cw2, cb2, kg[0], pos0)
    idx = idx[:, :NSA_KV_HEADS, :SLC_TOPK]
    n_past = pos0 // SLC_BLOCK
    in_cache = idx < n_past
    pidx = jnp.minimum(idx, n_past - 1)
    per_page = PAGE_ROWS // SLC_BLOCK
    page = jnp.take_along_axis(page_table, (pidx // per_page).reshape(db, -1), axis=1).reshape(pidx.shape)
    row0 = (page * PAGE_ROWS + (pidx % per_page) * SLC_BLOCK).astype(jnp.int32).reshape(-1)
    offs = jnp.arange(SLC_BLOCK, dtype=jnp.int32)
    kpos = jnp.where(in_cache[..., None], idx[..., None] * SLC_BLOCK + offs, pos0 + 1).reshape(db, NSA_KV_HEADS, -1)
    o_slc = sample_slc_attention(cache_nsa_kv.reshape(n_pool * PAGE_ROWS, 4 * NSA_KV_HEADS, HEAD_DIM), row0, qn3, kpos,
                                 slopes_b, kv[:, 2 * half:].reshape(db, 2 * NSA_KV_HEADS, HEAD_DIM), kg[1], pos0)
    gates_b = jnp.broadcast_to(gates[:, :3 * NSA_HEADS].reshape(db, 3, NSA_HEADS, 1), (db, 3, NSA_HEADS, HEAD_DIM))
    wb = state_win_kv.shape[2]
    o_nsa, win_s = sample_win_attention(state_win_kv.reshape(db, wb, 2 * NSA_KV_HEADS, HEAD_DIM), qn3, slopes_b,
                                        win.reshape(db, 2 * NSA_KV_HEADS, HEAD_DIM), kg[2], gates_b, o_cmp, o_slc)
    o_ret, ret_s = sample_retention(state_ret[0], r.reshape(db, 4, RET_HEADS, HEAD_DIM), gn)
    y_sample = back(hs, o_nsa.reshape(db, -1), o_ret.reshape(db, -1)).reshape(db, 1, d)
    kv_sample = kv.reshape(1, db, 1, 4, NSA_KV_HEADS, HEAD_DIM)
    win_sample = win_s.reshape(1, db, wb, 2, NSA_KV_HEADS, HEAD_DIM)

    return (y_prompt, y_sample, kv_prompt, kv_sample, win_prompt, win_sample, ret_p[None], ret_s[None])
```

```python
import functools
import math

import numpy as np
import jax
import jax.numpy as jnp
from jax import lax
from jax.experimental import pallas as pl
from jax.experimental.pallas import tpu as pltpu

F32 = jnp.float32
BF16 = jnp.bfloat16

HEAD_DIM = 128
NSA_HEADS = 16
NSA_KV_HEADS = 4
NSA_GROUP = NSA_HEADS // NSA_KV_HEADS
RET_HEADS = 16
CMP_BLOCK = 32
CMP_STRIDE = 16
CMP_HIDDEN = 2 * HEAD_DIM
SLC_BLOCK = 64
SLC_TOPK = 16
WINDOW = 512
RET_CHUNK = 128
EPS = 1e-6
BIG = 1e30
NEG = -1e30
LANE = 128
SUBLANE = 8
VMEM_LIMIT = 56 * 1024 * 1024

GQ_W = NSA_GROUP * HEAD_DIM


def _cparams(sem):
    return pltpu.CompilerParams(dimension_semantics=sem, vmem_limit_bytes=VMEM_LIMIT)


def _tile(n, pref):
    if n <= pref:
        return n
    t = pref
    while n % t:
        t //= 2
    return t


def _dot(a, b):
    return jnp.dot(a, b, preferred_element_type=F32)


def _dot_nt(a, b):
    return lax.dot_general(a, b, (((1,), (1,)), ((), ())), preferred_element_type=F32)


def _gelu_tanh(x):
    return 0.5 * x * (1.0 + jnp.tanh(math.sqrt(2.0 / math.pi) * (x + 0.044715 * (x * x * x))))


def _silu(x):
    return x * (1.0 / (1.0 + jnp.exp(-x)))


def _sigmoid(x):
    return 1.0 / (1.0 + jnp.exp(-x))


def _row_rms(x, g):
    return x * lax.rsqrt(jnp.mean(x * x, axis=-1, keepdims=True) + EPS) * g


def _rmsnorm_kernel(x_ref, g_ref, o_ref):
    o_ref[...] = _row_rms(x_ref[...], g_ref[...]).astype(o_ref.dtype)


def rmsnorm_bf16(x, g):
    t, d = x.shape
    tr = _tile(t, 256)
    return pl.pallas_call(
        _rmsnorm_kernel,
        grid=(t // tr,),
        in_specs=[pl.BlockSpec((tr, d), lambda i: (i, 0)), pl.BlockSpec((1, d), lambda i: (0, 0))],
        out_specs=pl.BlockSpec((tr, d), lambda i: (i, 0)),
        out_shape=jax.ShapeDtypeStruct((t, d), BF16),
        compiler_params=_cparams(("parallel",)),
        name="rmsnorm",
    )(x, g.reshape(1, d))


def _mm_nt_kernel(a_ref, w_ref, o_ref):
    o_ref[...] = _dot_nt(a_ref[...], w_ref[...].astype(BF16))


def matmul_nt(a, wt, row0, n):
    m, k = a.shape
    tm = _tile(m, 1024)
    tn = _tile(n, 512)
    assert wt.shape[1] == k and row0 % SUBLANE == 0 and tn % SUBLANE == 0
    wspec = pl.BlockSpec((pl.Element(tn), pl.Element(k)),
                         lambda i, j: ((row0 // SUBLANE + j * (tn // SUBLANE)) * SUBLANE, 0))
    return pl.pallas_call(
        _mm_nt_kernel,
        grid=(m // tm, n // tn),
        in_specs=[pl.BlockSpec((tm, k), lambda i, j: (i, 0)), wspec],
        out_specs=pl.BlockSpec((tm, tn), lambda i, j: (i, j)),
        out_shape=jax.ShapeDtypeStruct((m, n), F32),
        compiler_params=_cparams(("parallel", "arbitrary")),
        name="in_proj",
    )(a, wt)


def _out_proj_kernel(a1_ref, a2_ref, w_ref, r_ref, o_ref, *, k1):
    acc = _dot(a1_ref[...], w_ref[pl.ds(0, k1), :].astype(BF16))
    acc = acc + _dot(a2_ref[...], w_ref[pl.ds(k1, w_ref.shape[0] - k1), :].astype(BF16))
    o_ref[...] = r_ref[...] + acc


def out_proj(a1, a2, w, res):
    m, k1 = a1.shape
    k = k1 + a2.shape[1]
    n = w.shape[2]
    tm = _tile(m, 1024)
    tn = _tile(n, 512)
    assert w.shape[:2] == (1, k)
    return pl.pallas_call(
        functools.partial(_out_proj_kernel, k1=k1),
        grid=(m // tm, n // tn),
        in_specs=[pl.BlockSpec((tm, k1), lambda i, j: (i, 0)), pl.BlockSpec((tm, k - k1), lambda i, j: (i, 0)),
                  pl.BlockSpec((None, k, tn), lambda i, j: (0, 0, j)), pl.BlockSpec((tm, tn), lambda i, j: (i, j))],
        out_specs=pl.BlockSpec((tm, tn), lambda i, j: (i, j)),
        out_shape=jax.ShapeDtypeStruct((m, n), F32),
        compiler_params=_cparams(("parallel", "arbitrary")),
        name="out_proj",
    )(a1, a2, w, res)


def _ffn_kernel(x_hbm, g_ref, w1_ref, w3_ref, w2_ref, o_ref, n_ref, sem, *, tm, rb):
    i = pl.program_id(0)
    f = pl.program_id(1)

    @pl.when(f == 0)
    def _():
        cp = pltpu.make_async_copy(x_hbm.at[pl.ds(pl.multiple_of(i * tm, tm), tm), :], o_ref, sem.at[0])
        cp.start()
        cp.wait()

        def norm_rows(r, carry):
            rows = pl.ds(pl.multiple_of(r * rb, rb), rb)
            n_ref[rows, :] = _row_rms(o_ref[rows, :], g_ref[...]).astype(BF16)
            return carry

        lax.fori_loop(0, tm // rb, norm_rows, 0)

    n = n_ref[...]
    h = (_silu(_dot(n, w1_ref[...].astype(BF16))) * _dot(n, w3_ref[...].astype(BF16)) * 0.5).astype(BF16)
    o_ref[...] += _dot(h, w2_ref[...].astype(BF16))


def ffn_half_step(x, g, w1, w3, w2, which):
    t, d = x.shape
    dff = w1.shape[3]
    tm = _tile(t, 1024)
    tf = _tile(dff, 256)
    return pl.pallas_call(
        functools.partial(_ffn_kernel, tm=tm, rb=_tile(tm, 64)),
        grid=(t // tm, dff // tf),
        in_specs=[
            pl.BlockSpec(memory_space=pl.ANY),
            pl.BlockSpec((1, d), lambda i, f: (0, 0)),
            pl.BlockSpec((None, None, d, tf), lambda i, f: (0, which, 0, f)),
            pl.BlockSpec((None, None, d, tf), lambda i, f: (0, which, 0, f)),
            pl.BlockSpec((None, None, tf, d), lambda i, f: (0, which, f, 0)),
        ],
        out_specs=pl.BlockSpec((tm, d), lambda i, f: (i, 0), pipeline_mode=pl.Buffered(1)),
        out_shape=jax.ShapeDtypeStruct((t, d), F32),
        scratch_shapes=[pltpu.VMEM((tm, d), BF16), pltpu.SemaphoreType.DMA((1,))],
        compiler_params=_cparams(("parallel", "arbitrary")),
        name="ffn",
    )(x, g.reshape(1, d), w1, w3, w2)


def _headnorm_kernel(x_ref, g_ref, o_ref, *, scale):
    o_ref[...] = (_row_rms(x_ref[...], g_ref[...]) * scale).astype(o_ref.dtype)


def headnorm_bf16(x, col0, nheads, g, scale=1.0):
    t = x.shape[0]
    tr = _tile(t, 1024)
    c0 = col0 // HEAD_DIM
    return pl.pallas_call(
        functools.partial(_headnorm_kernel, scale=scale),
        grid=(t // tr, nheads),
        in_specs=[pl.BlockSpec((tr, HEAD_DIM), lambda i, j: (i, c0 + j)),
                  pl.BlockSpec((1, HEAD_DIM), lambda i, j: (0, 0))],
        out_specs=pl.BlockSpec((tr, HEAD_DIM), lambda i, j: (i, j)),
        out_shape=jax.ShapeDtypeStruct((t, nheads * HEAD_DIM), BF16),
        compiler_params=_cparams(("parallel", "parallel")),
        name="headnorm",
    )(x, g.reshape(1, HEAD_DIM))


def _vt_kernel(x_ref, o_ref):
    o_ref[...] = x_ref[...].T.astype(o_ref.dtype)


def values_transposed(x, col0, b, l):
    tr = _tile(l, 512)
    nl = l // tr
    c0 = col0 // HEAD_DIM
    return pl.pallas_call(
        _vt_kernel,
        grid=(b, NSA_KV_HEADS, nl),
        in_specs=[pl.BlockSpec((tr, HEAD_DIM), lambda bi, h, i: (bi * nl + i, c0 + h))],
        out_specs=pl.BlockSpec((None, None, HEAD_DIM, tr), lambda bi, h, i: (bi, h, 0, i)),
        out_shape=jax.ShapeDtypeStruct((b, NSA_KV_HEADS, HEAD_DIM, l), BF16),
        compiler_params=_cparams(("parallel", "parallel", "parallel")),
        name="values_t",
    )(x)


def _compress_kernel(x_ref, pe_ref, w1_ref, b1_ref, w2_ref, b2_ref, kn_ref, o_ref, xs_ref, *, l, is_k):
    nc = l // CMP_STRIDE
    xs_ref[pl.ds(0, l), :] = x_ref[...]
    xs_ref[pl.ds(l, CMP_BLOCK), :] = jnp.zeros((CMP_BLOCK, HEAD_DIM), F32)
    acc = jnp.zeros((nc, CMP_HIDDEN), F32)
    for r in range(CMP_BLOCK):
        xr = xs_ref[pl.ds(r, nc, stride=CMP_STRIDE), :] + pe_ref[pl.ds(r, 1), :]
        acc = acc + _dot(xr.astype(BF16), w1_ref[r])
    hid = _gelu_tanh(acc + b1_ref[...])
    c = _dot(hid.astype(BF16), w2_ref[...]) + b2_ref[...]
    if is_k:
        o_ref[...] = _row_rms(c, kn_ref[...]).astype(o_ref.dtype)
    else:
        o_ref[...] = c.T.astype(o_ref.dtype)


def compress_prompt(kv, which, b, l, pe, w1, b1, w2, b2, kn):
    nc = l // CMP_STRIDE
    is_k = which == 0
    oshape = (b, NSA_KV_HEADS, nc, HEAD_DIM) if is_k else (b, NSA_KV_HEADS, HEAD_DIM, nc)
    oblock = (None, None, nc, HEAD_DIM) if is_k else (None, None, HEAD_DIM, nc)
    return pl.pallas_call(
        functools.partial(_compress_kernel, l=l, is_k=is_k),
        grid=(b, NSA_KV_HEADS),
        in_specs=[
            pl.BlockSpec((l, HEAD_DIM), lambda bi, h: (bi, which * NSA_KV_HEADS + h)),
            pl.BlockSpec((CMP_BLOCK, HEAD_DIM), lambda bi, h: (0, 0)),
            pl.BlockSpec((CMP_BLOCK, HEAD_DIM, CMP_HIDDEN), lambda bi, h: (0, 0, 0)),
            pl.BlockSpec((1, CMP_HIDDEN), lambda bi, h: (0, 0)),
            pl.BlockSpec((CMP_HIDDEN, HEAD_DIM), lambda bi, h: (0, 0)),
            pl.BlockSpec((1, HEAD_DIM), lambda bi, h: (0, 0)),
            pl.BlockSpec((1, HEAD_DIM), lambda bi, h: (0, 0)),
        ],
        out_specs=pl.BlockSpec(oblock, lambda bi, h: (bi, h, 0, 0)),
        out_shape=jax.ShapeDtypeStruct(oshape, BF16),
        scratch_shapes=[pltpu.VMEM((l + CMP_BLOCK, HEAD_DIM), F32)],
        compiler_params=_cparams(("parallel", "parallel")),
        name="compress_prompt",
    )(kv, pe, w1, b1.reshape(1, -1), w2, b2.reshape(1, -1), kn.reshape(1, -1))


def _select_topk(score, j, ntop):
    ns = score.shape[0]
    bias = jnp.full(score.shape, NEG, F32)
    for _ in range(ntop):
        mx = jnp.max(score, axis=0, keepdims=True)
        idx = jnp.min(jnp.where(score == mx, j, ns), axis=0, keepdims=True)
        hit = j == idx
        bias = jnp.where(hit, 0.0, bias)
        score = jnp.where(hit, -jnp.inf, score)
    return bias


def _cmp_attn_kernel(slope_ref, q_ref, kc_ref, vct_ref, mt_ref, o_ref, sel_ref, *, tq, nc, ns):
    kvh = pl.program_id(1)
    qi = pl.program_id(2)
    qp = qi * tq + lax.broadcasted_iota(jnp.int32, (nc, tq), 1)
    blk_end = lax.broadcasted_iota(jnp.int32, (nc, tq), 0) * CMP_STRIDE + (CMP_BLOCK - 1)
    dist = qp - blk_end
    mask = dist >= 0
    distf = dist.astype(F32)
    kc = kc_ref[...]
    vct = vct_ref[...]
    imp = jnp.zeros((nc, tq), F32)
    for g in range(NSA_GROUP):
        s = _dot_nt(kc, q_ref[:, g * HEAD_DIM:(g + 1) * HEAD_DIM])
        s = jnp.where(mask, s - slope_ref[kvh * NSA_GROUP + g] * distf, -jnp.inf)
        m = jnp.max(s, axis=0, keepdims=True)
        m = jnp.where(m == -jnp.inf, 0.0, m)
        e = jnp.where(mask, jnp.exp(s - m), 0.0)
        p = e / jnp.maximum(jnp.sum(e, axis=0, keepdims=True), 1e-30)
        imp = imp + p
        o_ref[g] = _dot(vct, p.astype(BF16))
    score = jnp.dot(mt_ref[...], imp, preferred_element_type=F32, precision=lax.Precision.HIGHEST)
    j = lax.broadcasted_iota(jnp.int32, (ns, tq), 0)
    cur = (qi * tq + lax.broadcasted_iota(jnp.int32, (ns, tq), 1)) // SLC_BLOCK
    forced = (j == 0) | (j == cur) | (j == cur - 1)
    score = jnp.where(j <= cur, jnp.where(forced, BIG, score), -BIG)
    sel_ref[...] = _select_topk(score, j, min(SLC_TOPK, ns))


def _overlap_matrix(n_cmp, n_slc):
    cs = np.arange(n_cmp)[:, None] * CMP_STRIDE
    ss = np.arange(n_slc)[None, :] * SLC_BLOCK
    ov = np.minimum(cs + CMP_BLOCK, ss + SLC_BLOCK) - np.maximum(cs, ss)
    return (np.maximum(ov, 0).astype(np.float32) / CMP_BLOCK)


def cmp_attention_prompt(qn, kc, vct, slopes, b, l):
    nc = l // CMP_STRIDE
    ns = l // SLC_BLOCK
    tq = _tile(l, 512)
    nq = l // tq
    mt = jnp.asarray(_overlap_matrix(nc, ns).T)
    return pl.pallas_call(
        functools.partial(_cmp_attn_kernel, tq=tq, nc=nc, ns=ns),
        grid=(b, NSA_KV_HEADS, nq),
        in_specs=[
            pl.BlockSpec(memory_space=pltpu.SMEM),
            pl.BlockSpec((tq, GQ_W), lambda bi, h, i: (bi * nq + i, h)),
            pl.BlockSpec((None, None, nc, HEAD_DIM), lambda bi, h, i: (bi, h, 0, 0)),
            pl.BlockSpec((None, None, HEAD_DIM, nc), lambda bi, h, i: (bi, h, 0, 0)),
            pl.BlockSpec((ns, nc), lambda bi, h, i: (0, 0)),
        ],
        out_specs=[
            pl.BlockSpec((None, NSA_GROUP, HEAD_DIM, tq), lambda bi, h, i: (bi, h, 0, i)),
            pl.BlockSpec((None, None, ns, tq), lambda bi, h, i: (bi, h, 0, i)),
        ],
        out_shape=[jax.ShapeDtypeStruct((b, NSA_HEADS, HEAD_DIM, l), F32),
                   jax.ShapeDtypeStruct((b, NSA_KV_HEADS, ns, l), F32)],
        compiler_params=_cparams(("parallel", "parallel", "parallel")),
        name="cmp_attention",
    )(slopes, qn, kc, vct, mt)


def _flash_kernel(slope_ref, q_ref, k_ref, vt_ref, *rest, tq, tk, nkt, windowed):
    if windowed:
        g_ref, oc_ref, os_ref, o_ref, m_ref, l_ref, acc_ref = rest
    else:
        sel_ref, o_ref, m_ref, l_ref, acc_ref = rest
    kvh = pl.program_id(1)
    qi = pl.program_id(2)
    kt = pl.program_id(3)
    if windowed:
        ktile = qi - (nkt - 1) + kt
        valid = ktile >= 0
    else:
        ktile = kt
        valid = kt * tk <= qi * tq + (tq - 1)

    @pl.when(kt == 0)
    def _():
        m_ref[...] = jnp.full(m_ref.shape, NEG, F32)
        l_ref[...] = jnp.zeros(l_ref.shape, F32)
        acc_ref[...] = jnp.zeros(acc_ref.shape, F32)

    @pl.when(valid)
    def _():
        kpos = ktile * tk + lax.broadcasted_iota(jnp.int32, (tk, tq), 0)
        qpos = qi * tq + lax.broadcasted_iota(jnp.int32, (tk, tq), 1)
        dist = qpos - kpos
        distf = dist.astype(F32)
        if windowed:
            bias = jnp.where((dist >= 0) & (dist < WINDOW), 0.0, NEG)
        else:
            sel = sel_ref[...]
            sel = jnp.concatenate(
                [jnp.broadcast_to(sel[r:r + 1, :], (SLC_BLOCK, tq)) for r in range(tk // SLC_BLOCK)], axis=0)
            bias = jnp.where(dist >= 0, sel, NEG)
        k = k_ref[...]
        vt = vt_ref[...]
        for g in range(NSA_GROUP):
            s = _dot_nt(k, q_ref[:, g * HEAD_DIM:(g + 1) * HEAD_DIM])
            s = s - slope_ref[kvh * NSA_GROUP + g] * distf + bias
            m_prev = m_ref[pl.ds(g, 1), :]
            m_new = jnp.maximum(m_prev, jnp.max(s, axis=0, keepdims=True))
            alpha = jnp.exp(m_prev - m_new)
            e = jnp.exp(s - m_new)
            l_ref[pl.ds(g, 1), :] = alpha * l_ref[pl.ds(g, 1), :] + jnp.sum(e, axis=0, keepdims=True)
            acc_ref[g] = alpha * acc_ref[g] + _dot(vt, e.astype(BF16))
            m_ref[pl.ds(g, 1), :] = m_new

    @pl.when(kt == nkt - 1)
    def _():
        if windowed:
            gates = _sigmoid(g_ref[...])
        for g in range(NSA_GROUP):
            o = acc_ref[g] / jnp.maximum(l_ref[pl.ds(g, 1), :], 1e-30)
            if windowed:
                o = (gates[g:g + 1, :] * oc_ref[g] + gates[NSA_GROUP + g:NSA_GROUP + g + 1, :] * os_ref[g]
                     + gates[2 * NSA_GROUP + g:2 * NSA_GROUP + g + 1, :] * o)
                o_ref[:, g * HEAD_DIM:(g + 1) * HEAD_DIM] = o.T.astype(o_ref.dtype)
            else:
                o_ref[g] = o


def flash_prompt(qn, kn, vt, slopes, b, l, sel=None, gated=None):
    windowed = sel is None
    tq = _tile(l, 512)
    tk = tq
    nq = l // tq
    nkt = (min(WINDOW, l) // tk + 1) if windowed else l // tk
    if windowed:
        kmap = lambda qi, kt: jnp.maximum(qi - (nkt - 1) + kt, 0)
    else:
        kmap = lambda qi, kt: jnp.minimum(kt, (qi * tq + tq - 1) // tk)
    in_specs = [
        pl.BlockSpec(memory_space=pltpu.SMEM),
        pl.BlockSpec((tq, GQ_W), lambda bi, h, qi, kt: (bi * nq + qi, h)),
        pl.BlockSpec((tk, HEAD_DIM), lambda bi, h, qi, kt: (bi * nq + kmap(qi, kt), h)),
        pl.BlockSpec((None, None, HEAD_DIM, tk), lambda bi, h, qi, kt: (bi, h, 0, kmap(qi, kt))),
    ]
    args = [slopes, qn, kn, vt]
    tspec = pl.BlockSpec((None, NSA_GROUP, HEAD_DIM, tq), lambda bi, h, qi, kt: (bi, h, 0, qi))
    if windowed:
        gates, o_cmp, o_slc = gated
        ngr = 3 * NSA_GROUP
        gk = gates[:, :3 * NSA_HEADS].reshape(b, l, 3, NSA_KV_HEADS, NSA_GROUP).transpose(0, 3, 2, 4, 1)
        gk = jnp.pad(gk.reshape(b, NSA_KV_HEADS, ngr, l), ((0, 0), (0, 0), (0, -ngr % SUBLANE), (0, 0)))
        in_specs += [pl.BlockSpec((None, None, gk.shape[2], tq), lambda bi, h, qi, kt: (bi, h, 0, qi)), tspec, tspec]
        args += [gk, o_cmp, o_slc]
        out_spec = pl.BlockSpec((tq, GQ_W), lambda bi, h, qi, kt: (bi * nq + qi, h))
        out_shape = jax.ShapeDtypeStruct((b * l, NSA_HEADS * HEAD_DIM), BF16)
    else:
        in_specs.append(pl.BlockSpec((None, None, tk // SLC_BLOCK, tq),
                                     lambda bi, h, qi, kt: (bi, h, kmap(qi, kt), qi)))
        args.append(sel)
        out_spec = tspec
        out_shape = jax.ShapeDtypeStruct((b, NSA_HEADS, HEAD_DIM, l), F32)
    return pl.pallas_call(
        functools.partial(_flash_kernel, tq=tq, tk=tk, nkt=nkt, windowed=windowed),
        grid=(b, NSA_KV_HEADS, nq, nkt),
        in_specs=in_specs,
        out_specs=out_spec,
        out_shape=out_shape,
        scratch_shapes=[pltpu.VMEM((SUBLANE, tq), F32), pltpu.VMEM((SUBLANE, tq), F32),
                        pltpu.VMEM((NSA_GROUP, HEAD_DIM, tq), F32)],
        compiler_params=_cparams(("parallel", "parallel", "parallel", "arbitrary")),
        name="window_attention" if windowed else "selected_attention",
    )(*args)


def _group_norm_gate(o, rg, gn):
    mu = jnp.mean(o, axis=-1, keepdims=True)
    var = jnp.mean(jnp.square(o - mu), axis=-1, keepdims=True)
    return _silu(rg) * ((o - mu) * lax.rsqrt(var + EPS) * gn)


def _retention_kernel(q_ref, k_ref, v_ref, rg_ref, intra_ref, qd_ref, kd_ref, cd_ref, gn_ref,
                      o_ref, s_ref, st_ref, *, tl, hb):
    li = pl.program_id(2)
    nl = pl.num_programs(2)

    @pl.when(li == 0)
    def _():
        st_ref[...] = jnp.zeros(st_ref.shape, F32)

    def chunk(c, carry):
        rows = pl.ds(pl.multiple_of(c * RET_CHUNK, RET_CHUNK), RET_CHUNK)
        for h in range(hb):
            cols = slice(h * HEAD_DIM, (h + 1) * HEAD_DIM)
            qc = q_ref[rows, cols]
            kc = k_ref[rows, cols] * (HEAD_DIM ** -0.5)
            vc = v_ref[rows, cols].astype(BF16)
            state = st_ref[h]
            att = _dot_nt(qc.astype(BF16), kc.astype(BF16)) * intra_ref[h]
            o = _dot(att.astype(BF16), vc) + _dot((qc * qd_ref[h]).astype(BF16), state.astype(BF16))
            st_ref[h] = state * cd_ref[h] + _dot((kc * kd_ref[h]).T.astype(BF16), vc)
            o_ref[rows, cols] = _group_norm_gate(o, rg_ref[rows, cols], gn_ref[:, cols]).astype(o_ref.dtype)
        return carry

    lax.fori_loop(0, tl // RET_CHUNK, chunk, 0)

    @pl.when(li == nl - 1)
    def _():
        s_ref[...] = st_ref[...]


def _retention_tables(chunk):
    h = jnp.arange(RET_HEADS, dtype=F32)
    log_g = jnp.log1p(-jnp.exp2(-5.0 - h))
    i = jnp.arange(chunk, dtype=F32)
    diff = i[:, None] - i[None, :]
    intra = jnp.where(diff >= 0, jnp.exp(jnp.maximum(diff, 0.0)[None] * log_g[:, None, None]), 0.0)
    q_dec = jnp.exp((i[None, :] + 1.0) * log_g[:, None])
    k_dec = jnp.exp((chunk - 1.0 - i)[None, :] * log_g[:, None])
    c_dec = jnp.exp(chunk * log_g)
    return intra, q_dec, k_dec, c_dec


def retention_prompt(r, gn, b, l):
    intra, q_dec, k_dec, c_dec = _retention_tables(RET_CHUNK)
    bc = lambda t: jnp.broadcast_to(t[:, :, None], (RET_HEADS, RET_CHUNK, HEAD_DIM))
    qd, kd = bc(q_dec), bc(k_dec)
    cd = jnp.broadcast_to(c_dec[:, None, None], (RET_HEADS, HEAD_DIM, HEAD_DIM))
    hb = 8
    ng = RET_HEADS // hb
    tl = _tile(l, 1024)
    nl = l // tl
    w = hb * HEAD_DIM
    rspec = lambda part: pl.BlockSpec((tl, w), lambda bi, g, li: (bi * nl + li, part * ng + g))
    tspec = pl.BlockSpec((hb, RET_CHUNK, HEAD_DIM), lambda bi, g, li: (g, 0, 0))
    return pl.pallas_call(
        functools.partial(_retention_kernel, tl=tl, hb=hb),
        grid=(b, ng, nl),
        in_specs=[rspec(0), rspec(1), rspec(2), rspec(3), tspec, tspec, tspec, tspec,
                  pl.BlockSpec((1, w), lambda bi, g, li: (0, g))],
        out_specs=[pl.BlockSpec((tl, w), lambda bi, g, li: (bi * nl + li, g)),
                   pl.BlockSpec((None, hb, HEAD_DIM, HEAD_DIM), lambda bi, g, li: (bi, g, 0, 0))],
        out_shape=[jax.ShapeDtypeStruct((b * l, RET_HEADS * HEAD_DIM), BF16),
                   jax.ShapeDtypeStruct((b, RET_HEADS, HEAD_DIM, HEAD_DIM), F32)],
        scratch_shapes=[pltpu.VMEM((hb, HEAD_DIM, HEAD_DIM), F32)],
        compiler_params=_cparams(("parallel", "parallel", "arbitrary")),
        name="retention_prompt",
    )(r, r, r, r, intra, qd, kd, cd, gn.reshape(1, -1))


def _group_rows(kvh):
    row = lax.broadcasted_iota(jnp.int32, (NSA_HEADS, 1), 0)
    return (row >= kvh * NSA_GROUP) & (row < (kvh + 1) * NSA_GROUP)


def _sample_cmp_kernel(pt_ref, *refs, pp, npg, pos0):
    pages = refs[:pp]
    (q_ref, slope_ref, pea_ref, peb_ref, wa_ref, wb_ref, b1_ref, w2_ref, b2_ref, kn_ref, m_ref,
     o_ref, sc_ref, xa_ref, xb_ref, f_ref, s_ref, c_ref) = refs[pp:]
    jp = pl.program_id(1)
    cpp = PAGE_ROWS // CMP_STRIDE
    nck = npg * cpp
    ng = 2 * NSA_KV_HEADS
    rows = pp * cpp * ng
    nrow = nck * ng

    def is_k(shape):
        return (lax.broadcasted_iota(jnp.int32, shape, 0) & NSA_KV_HEADS) == 0

    @pl.when(jp == 0)
    def _():
        s_ref[pl.ds(nrow, ng), :] = jnp.zeros((ng, CMP_HIDDEN), F32)

    for t in range(pp):
        for i in range(cpp):
            for c in range(CMP_STRIDE):
                dst = (pl.ds((t * cpp + i) * ng, ng), pl.ds(c * HEAD_DIM, HEAD_DIM))
                row = pages[t][i * CMP_STRIDE + c]
                xa_ref[dst] = row + pea_ref[:, c * HEAD_DIM:(c + 1) * HEAD_DIM]
                xb_ref[dst] = row + peb_ref[:, c * HEAD_DIM:(c + 1) * HEAD_DIM]
    fa = _dot(xa_ref[...].astype(BF16), wa_ref[...])
    sb = _dot(xb_ref[...].astype(BF16), wb_ref[...])
    kmask = is_k((rows, CMP_HIDDEN))
    dst = pl.ds(pl.multiple_of(jp * rows, rows), rows)
    f_ref[dst, :] = jnp.where(kmask, fa[:, :CMP_HIDDEN], fa[:, CMP_HIDDEN:])
    s_ref[dst, :] = jnp.where(kmask, sb[:, :CMP_HIDDEN], sb[:, CMP_HIDDEN:])

    @pl.when(jp == pl.num_programs(1) - 1)
    def _():
        eb = min(nrow, 1024)

        def cblock(r, carry):
            lo = pl.multiple_of(r * eb, eb)
            b1 = jnp.where(is_k((eb, CMP_HIDDEN)), b1_ref[0:1, :], b1_ref[1:2, :])
            hid = _gelu_tanh(f_ref[pl.ds(lo, eb), :] + s_ref[pl.ds(lo + ng, eb), :] + b1)
            c2 = _dot(hid.astype(BF16), w2_ref[...])
            c_ref[pl.ds(lo, eb), :] = jnp.where(is_k((eb, HEAD_DIM)), c2[:, :HEAD_DIM] + b2_ref[0:1, :],
                                                c2[:, HEAD_DIM:] + b2_ref[1:2, :])
            return carry

        lax.fori_loop(0, nrow // eb, cblock, 0)

        q = q_ref[...]
        n_io = lax.broadcasted_iota(jnp.int32, (NSA_HEADS, nck), 1)
        dist = pos0 - (n_io * CMP_STRIDE + (CMP_BLOCK - 1))
        mask = dist >= 0
        bias = slope_ref[:, :nck] * dist.astype(F32)
        o_acc = jnp.zeros((NSA_HEADS, HEAD_DIM), F32)
        imp_rows = []
        for kvh in range(NSA_KV_HEADS):
            kc = _row_rms(c_ref[pl.ds(kvh, nck, stride=ng), :], kn_ref[...]).astype(BF16)
            vc = c_ref[pl.ds(NSA_KV_HEADS + kvh, nck, stride=ng), :].astype(BF16)
            s = jnp.where(mask, _dot_nt(q, kc) - bias, -jnp.inf)
            m = jnp.max(s, axis=1, keepdims=True)
            m = jnp.where(m == -jnp.inf, 0.0, m)
            e = jnp.where(mask, jnp.exp(s - m), 0.0)
            p = e / jnp.maximum(jnp.sum(e, axis=1, keepdims=True), 1e-30)
            grp = _group_rows(kvh)
            o_acc = o_acc + jnp.where(grp, _dot(p.astype(BF16), vc), 0.0)
            imp_rows.append(jnp.sum(jnp.where(grp, p, 0.0), axis=0, keepdims=True))
        o_ref[...] = o_acc
        imp = jnp.concatenate(imp_rows + [jnp.zeros((SUBLANE - NSA_KV_HEADS, nck), F32)], axis=0)
        sc_ref[...] = jnp.dot(imp, m_ref[...], preferred_element_type=F32, precision=lax.Precision.HIGHEST)


def _sample_topk_kernel(sc_ref, idx_ref, *, pos0):
    score = sc_ref[...]
    rows, nsp = score.shape
    j = lax.broadcasted_iota(jnp.int32, (rows, nsp), 1)
    cur = pos0 // SLC_BLOCK
    forced = (j == 0) | (j == cur) | (j == cur - 1)
    score = jnp.where(j <= cur, jnp.where(forced, BIG, score), -BIG)
    lane = lax.broadcasted_iota(jnp.int32, (rows, LANE), 1)
    out = jnp.zeros((rows, LANE), jnp.int32)
    for t in range(SLC_TOPK):
        mx = jnp.max(score, axis=1, keepdims=True)
        idx = jnp.min(jnp.where(score == mx, j, nsp), axis=1, keepdims=True)
        out = jnp.where(lane == t, idx, out)
        score = jnp.where(j == idx, -jnp.inf, score)
    idx_ref[...] = out


def sample_topk(score, pos0):
    rows = score.shape[0]
    return pl.pallas_call(
        functools.partial(_sample_topk_kernel, pos0=pos0),
        out_shape=jax.ShapeDtypeStruct((rows, LANE), jnp.int32),
        name="sample_topk",
    )(score)


PAGE_ROWS = 128


def sample_cmp_select(cache4, page_table, qn3, slopes_b, pe, w1, b1, w2, b2, kn, pos0):
    db, npg = page_table.shape
    pp = _tile(npg, 8)
    cpp = PAGE_ROWS // CMP_STRIDE
    nck = npg * cpp
    ng = 2 * NSA_KV_HEADS
    n_slc = (pos0 + 1 + SLC_BLOCK - 1) // SLC_BLOCK
    nsp = -(-n_slc // LANE) * LANE
    m = np.zeros((nck, nsp), np.float32)
    m[:, :n_slc] = _overlap_matrix(nck, n_slc)
    kdim = CMP_STRIDE * HEAD_DIM
    pe_rows = lambda a: jnp.repeat(a.reshape(2, 1, kdim), NSA_KV_HEADS, axis=1).reshape(ng, kdim)
    pea, peb = pe_rows(pe[:, :CMP_STRIDE]), pe_rows(pe[:, CMP_STRIDE:])
    side = lambda a: jnp.concatenate([a[0], a[1]], axis=-1)
    wa = side(w1[:, :CMP_STRIDE].reshape(2, kdim, CMP_HIDDEN))
    wb = side(w1[:, CMP_STRIDE:].reshape(2, kdim, CMP_HIDDEN))
    const = lambda shape: pl.BlockSpec(shape, lambda bi, jp, pt: (0,) * len(shape))
    page_specs = [
        pl.BlockSpec((None, PAGE_ROWS, ng, HEAD_DIM),
                     functools.partial(lambda bi, jp, pt, t: (pt[bi, jp * pp + t], 0, 0, 0), t=t))
        for t in range(pp)]
    grid_spec = pltpu.PrefetchScalarGridSpec(
        num_scalar_prefetch=1,
        grid=(db, npg // pp),
        in_specs=page_specs + [
            pl.BlockSpec((None, NSA_HEADS, HEAD_DIM), lambda bi, jp, pt: (bi, 0, 0)),
            const((NSA_HEADS, slopes_b.shape[1])),
            const((ng, kdim)), const((ng, kdim)),
            const((kdim, 2 * CMP_HIDDEN)), const((kdim, 2 * CMP_HIDDEN)),
            const((2, CMP_HIDDEN)), const((CMP_HIDDEN, 2 * HEAD_DIM)), const((2, HEAD_DIM)),
            const((1, HEAD_DIM)), const((nck, nsp)),
        ],
        out_specs=[pl.BlockSpec((None, NSA_HEADS, HEAD_DIM), lambda bi, jp, pt: (bi, 0, 0)),
                   pl.BlockSpec((None, SUBLANE, nsp), lambda bi, jp, pt: (bi, 0, 0))],
        scratch_shapes=[
            pltpu.VMEM((pp * cpp * ng, kdim), F32),
            pltpu.VMEM((pp * cpp * ng, kdim), F32),
            pltpu.VMEM((nck * ng, CMP_HIDDEN), F32),
            pltpu.VMEM((nck * ng + ng, CMP_HIDDEN), F32),
            pltpu.VMEM((nck * ng, HEAD_DIM), F32),
        ],
    )
    return pl.pallas_call(
        functools.partial(_sample_cmp_kernel, pp=pp, npg=npg, pos0=pos0),
        grid_spec=grid_spec,
        out_shape=[jax.ShapeDtypeStruct((db, NSA_HEADS, HEAD_DIM), F32),
                   jax.ShapeDtypeStruct((db, SUBLANE, nsp), F32)],
        compiler_params=_cparams(("parallel", "arbitrary")),
        name="sample_cmp_select",
    )(page_table, *([cache4] * pp), qn3, slopes_b, pea, peb, wa, wb, b1, side(w2), b2, kn.reshape(1, -1),
      jnp.asarray(m))


def _softmax_with_new_key(s, s_new):
    m = jnp.maximum(jnp.max(s, axis=1, keepdims=True), s_new)
    e = jnp.exp(s - m)
    e_new = jnp.exp(s_new - m)
    return e, e_new, jnp.maximum(jnp.sum(e, axis=1, keepdims=True) + e_new, 1e-30)


def _bf16_round(x):
    return x.astype(BF16).astype(F32)


def _sample_slc_kernel(row_ref, cache_ref, q_ref, kpos_ref, slope_ref, new_ref, kn_ref, o_ref, kbuf, vbuf, sem, *, pos0):
    bi = pl.program_id(0)
    nb = pl.num_programs(0)
    nsel = SLC_TOPK * SLC_BLOCK
    kcol = 2 * NSA_KV_HEADS
    vcol = 3 * NSA_KV_HEADS

    def gather(b, slot, start):
        for kvh in range(NSA_KV_HEADS):
            for t in range(SLC_TOPK):
                src = pl.ds(row_ref[(b * NSA_KV_HEADS + kvh) * SLC_TOPK + t], SLC_BLOCK)
                dst = pl.ds(t * SLC_BLOCK, SLC_BLOCK)
                for col, buf in ((kcol, kbuf), (vcol, vbuf)):
                    cp = pltpu.make_async_copy(cache_ref.at[src, col + kvh, :], buf.at[slot, kvh, dst, :], sem.at[slot])
                    if start:
                        cp.start()
                    else:
                        cp.wait()

    slot = bi % 2

    @pl.when(bi == 0)
    def _():
        gather(bi, 0, True)

    @pl.when(bi + 1 < nb)
    def _():
        gather(bi + 1, 1 - slot, True)

    gather(bi, slot, False)

    q = q_ref[...]
    o_acc = jnp.zeros((NSA_HEADS, HEAD_DIM), F32)
    for kvh in range(NSA_KV_HEADS):
        kn = _row_rms(kbuf[slot, kvh], kn_ref[...]).astype(BF16)
        v = vbuf[slot, kvh]
        dist = pos0 - kpos_ref[pl.ds(kvh, 1), :]
        s = _dot_nt(q, kn) - slope_ref[:, :nsel] * dist.astype(F32)
        s = jnp.where(dist >= 0, s, NEG)
        knew = _row_rms(new_ref[pl.ds(kvh, 1), :], kn_ref[...])
        s_new = jnp.sum(q.astype(F32) * _bf16_round(knew), axis=1, keepdims=True)
        e, e_new, l = _softmax_with_new_key(s, s_new)
        vnew = new_ref[pl.ds(NSA_KV_HEADS + kvh, 1), :]
        o = _dot(e.astype(BF16), v.astype(BF16)) + _bf16_round(e_new) * _bf16_round(vnew)
        o_acc = o_acc + jnp.where(_group_rows(kvh), o / l, 0.0)
    o_ref[...] = o_acc


def sample_slc_attention(cache3, row0, qn3, kpos, slopes_b, new_kv, kn, pos0):
    db = qn3.shape[0]
    nsel = SLC_TOPK * SLC_BLOCK
    ng = 2 * NSA_KV_HEADS
    grid_spec = pltpu.PrefetchScalarGridSpec(
        num_scalar_prefetch=1,
        grid=(db,),
        in_specs=[
            pl.BlockSpec(memory_space=pl.ANY),
            pl.BlockSpec((None, NSA_HEADS, HEAD_DIM), lambda bi, r: (bi, 0, 0)),
            pl.BlockSpec((None, NSA_KV_HEADS, nsel), lambda bi, r: (bi, 0, 0)),
            pl.BlockSpec((NSA_HEADS, slopes_b.shape[1]), lambda bi, r: (0, 0)),
            pl.BlockSpec((None, ng, HEAD_DIM), lambda bi, r: (bi, 0, 0)),
            pl.BlockSpec((1, HEAD_DIM), lambda bi, r: (0, 0)),
        ],
        out_specs=pl.BlockSpec((None, NSA_HEADS, HEAD_DIM), lambda bi, r: (bi, 0, 0)),
        scratch_shapes=[pltpu.VMEM((2, NSA_KV_HEADS, nsel, HEAD_DIM), F32),
                        pltpu.VMEM((2, NSA_KV_HEADS, nsel, HEAD_DIM), F32),
                        pltpu.SemaphoreType.DMA((2,))],
    )
    return pl.pallas_call(
        functools.partial(_sample_slc_kernel, pos0=pos0),
        grid_spec=grid_spec,
        out_shape=jax.ShapeDtypeStruct((db, NSA_HEADS, HEAD_DIM), F32),
        compiler_params=_cparams(("arbitrary",)),
        name="sample_slc_attention",
    )(row0, cache3, qn3, kpos, slopes_b, new_kv, kn.reshape(1, -1))


def _sample_win_kernel(sw_ref, q_ref, slope_ref, new_ref, kn_ref, g_ref, oc_ref, os_ref, o_ref, nw_ref, *, wb):
    q = q_ref[...]
    row = lax.broadcasted_iota(jnp.int32, (NSA_HEADS, wb), 1)
    dist = wb - row
    valid = dist < WINDOW
    bias = slope_ref[:, :wb] * dist.astype(F32)
    o_acc = jnp.zeros((NSA_HEADS, HEAD_DIM), F32)
    for kvh in range(NSA_KV_HEADS):
        k = sw_ref[:, kvh, :]
        v = sw_ref[:, NSA_KV_HEADS + kvh, :]
        kn = _row_rms(k, kn_ref[...]).astype(BF16)
        s = jnp.where(valid, _dot_nt(q, kn) - bias, NEG)
        knew = _row_rms(new_ref[pl.ds(kvh, 1), :], kn_ref[...])
        s_new = jnp.sum(q.astype(F32) * _bf16_round(knew), axis=1, keepdims=True)
        e, e_new, l = _softmax_with_new_key(s, s_new)
        vnew = new_ref[pl.ds(NSA_KV_HEADS + kvh, 1), :]
        o = _dot(e.astype(BF16), v.astype(BF16)) + _bf16_round(e_new) * _bf16_round(vnew)
        o_acc = o_acc + jnp.where(_group_rows(kvh), o / l, 0.0)
    g = _sigmoid(g_ref[...])
    o_ref[...] = (g[0] * oc_ref[...] + g[1] * os_ref[...] + g[2] * o_acc).astype(o_ref.dtype)
    nw_ref[pl.ds(0, wb - 1)] = sw_ref[pl.ds(1, wb - 1)]
    nw_ref[wb - 1] = new_ref[...]


def sample_win_attention(sw, qn3, slopes_b, win_new, kn, gates_b, o_cmp, o_slc):
    db, wb, ng, _ = sw.shape
    hspec = pl.BlockSpec((None, NSA_HEADS, HEAD_DIM), lambda bi: (bi, 0, 0))
    wspec = pl.BlockSpec((None, wb, ng, HEAD_DIM), lambda bi: (bi, 0, 0, 0))
    return pl.pallas_call(
        functools.partial(_sample_win_kernel, wb=wb),
        grid=(db,),
        in_specs=[
            wspec,
            hspec,
            pl.BlockSpec((NSA_HEADS, slopes_b.shape[1]), lambda bi: (0, 0)),
            pl.BlockSpec((None, ng, HEAD_DIM), lambda bi: (bi, 0, 0)),
            pl.BlockSpec((1, HEAD_DIM), lambda bi: (0, 0)),
            pl.BlockSpec((None, 3, NSA_HEADS, HEAD_DIM), lambda bi: (bi, 0, 0, 0)),
            hspec, hspec,
        ],
        out_specs=[hspec, wspec],
        out_shape=[jax.ShapeDtypeStruct((db, NSA_HEADS, HEAD_DIM), BF16),
                   jax.ShapeDtypeStruct((db, wb, ng, HEAD_DIM), F32)],
        compiler_params=_cparams(("parallel",)),
        name="sample_win_attention",
    )(sw, qn3, slopes_b, win_new, kn.reshape(1, -1), gates_b, o_cmp, o_slc)


def _sample_ret_kernel(st_ref, r_ref, gam_ref, gn_ref, o_ref, s_ref):
    q = r_ref[0]
    k = r_ref[1] * (HEAD_DIM ** -0.5)
    v = r_ref[2]
    rg = r_ref[3]
    gam = gam_ref[...]
    qb, kb, vb = _bf16_round(q), _bf16_round(k), _bf16_round(v)
    att = jnp.sum(qb * kb, axis=1, keepdims=True)
    kt = kb.T
    qd = (q * gam).astype(BF16)
    rows = []
    for h in range(RET_HEADS):
        state = st_ref[h]
        rows.append(_dot(qd[h:h + 1, :], state.astype(BF16)))
        s_ref[h] = state * gam[h:h + 1, :] + kt[:, h:h + 1] * vb[h:h + 1, :]
    o = _bf16_round(att) * vb + jnp.concatenate(rows, axis=0)
    o_ref[...] = _group_norm_gate(o, rg, gn_ref[...]).astype(o_ref.dtype)


def sample_retention(state, r4, gn):
    db = state.shape[0]
    _, q_dec, _, _ = _retention_tables(1)
    gam = jnp.broadcast_to(q_dec, (RET_HEADS, HEAD_DIM))
    return pl.pallas_call(
        _sample_ret_kernel,
        grid=(db,),
        in_specs=[
            pl.BlockSpec((None, RET_HEADS, HEAD_DIM, HEAD_DIM), lambda bi: (bi, 0, 0, 0)),
            pl.BlockSpec((None, 4, RET_HEADS, HEAD_DIM), lambda bi: (bi, 0, 0, 0)),
            pl.BlockSpec((RET_HEADS, HEAD_DIM), lambda bi: (0, 0)),
            pl.BlockSpec((RET_HEADS, HEAD_DIM), lambda bi: (0, 0)),
        ],
        out_specs=[pl.BlockSpec((None, RET_HEADS, HEAD_DIM), lambda bi: (bi, 0, 0)),
                   pl.BlockSpec((None, RET_HEADS, HEAD_DIM, HEAD_DIM), lambda bi: (bi, 0, 0, 0))],
        out_shape=[jax.ShapeDtypeStruct((db, RET_HEADS, HEAD_DIM), BF16),
                   jax.ShapeDtypeStruct((db, RET_HEADS, HEAD_DIM, HEAD_DIM), F32)],
        compiler_params=_cparams(("parallel",)),
        name="sample_retention",
    )(state, r4, gam, gn.reshape(RET_HEADS, HEAD_DIM))


def _alibi_slopes():
    i = np.arange(NSA_HEADS, dtype=np.float32)
    return np.exp2(-8.0 * (i + 1.0) / NSA_HEADS).astype(np.float32)


def _in_proj(n, w_t):
    nq = NSA_HEADS * HEAD_DIM
    nkv = 4 * NSA_KV_HEADS * HEAD_DIM
    nwin = 2 * NSA_KV_HEADS * HEAD_DIM
    ng = 3 * NSA_HEADS
    nr = 4 * RET_HEADS * HEAD_DIM
    assert w_t.shape[0] == nq + nkv + nwin + ng + nr and ng <= LANE
    o_g = nq + nkv + nwin
    return [matmul_nt(n, w_t, 0, nq), matmul_nt(n, w_t, nq, nkv), matmul_nt(n, w_t, nq + nkv, nwin),
            matmul_nt(n, w_t, o_g, LANE), matmul_nt(n, w_t, o_g + ng, nr)]


def kernel(x_prompt, x_sample, cache_nsa_kv, state_win_kv, state_ret, page_table, norm_g, ffn_w1, ffn_w3, ffn_w2,
           w_in, nsa_q_norm, nsa_k_norm, cmp_pe, cmp_w1, cmp_b1, cmp_w2, cmp_b2, ret_gn_g, w_out):
    assert cache_nsa_kv.shape[0] == 1 and x_sample.shape[1] == 1
    b, l, d = x_prompt.shape
    db = x_sample.shape[0]
    npg = page_table.shape[1]
    n_pool = cache_nsa_kv.shape[1]
    assert cache_nsa_kv.shape[2] == PAGE_ROWS
    pos0 = npg * PAGE_ROWS
    half = NSA_KV_HEADS * HEAD_DIM

    w_t = jnp.swapaxes(w_in, 1, 2)[0]
    ng = norm_g[0]
    qg, kg = nsa_q_norm[0], nsa_k_norm[0]
    pe = cmp_pe[0]
    cw1 = cmp_w1[0].astype(BF16)
    cb1 = cmp_b1[0]
    cw2 = cmp_w2[0].astype(BF16)
    cb2 = cmp_b2[0]
    gn = ret_gn_g[0]
    slopes = jnp.asarray(_alibi_slopes())
    q_scale = HEAD_DIM ** -0.5

    def front(x):
        h = ffn_half_step(x, ng[0], ffn_w1, ffn_w3, ffn_w2, 0)
        return h, _in_proj(rmsnorm_bf16(h, ng[1]), w_t)

    def back(h, o_nsa, o_ret):
        h = out_proj(o_nsa, o_ret, w_out, h)
        return ffn_half_step(h, ng[2], ffn_w1, ffn_w3, ffn_w2, 1)

    hp, (q, kv, win, gates, r) = front(x_prompt.reshape(b * l, d))
    qn = headnorm_bf16(q, 0, NSA_HEADS, qg, q_scale)
    kc = compress_prompt(kv, 0, b, l, pe[0], cw1[0], cb1[0], cw2[0], cb2[0], kg[0])
    vct = compress_prompt(kv, 1, b, l, pe[1], cw1[1], cb1[1], cw2[1], cb2[1], kg[0])
    o_cmp, sel = cmp_attention_prompt(qn, kc, vct, slopes, b, l)
    k_slc = headnorm_bf16(kv, 2 * half, NSA_KV_HEADS, kg[1])
    v_slc = values_transposed(kv, 3 * half, b, l)
    o_slc = flash_prompt(qn, k_slc, v_slc, slopes, b, l, sel=sel)
    k_win = headnorm_bf16(win, 0, NSA_KV_HEADS, kg[2])
    v_win = values_transposed(win, half, b, l)
    o_nsa = flash_prompt(qn, k_win, v_win, slopes, b, l, gated=(gates, o_cmp, o_slc))
    o_ret, ret_p = retention_prompt(r, gn, b, l)
    y_prompt = back(hp, o_nsa, o_ret).reshape(b, l, d)
    kv_prompt = kv.reshape(1, b, l, 4, NSA_KV_HEADS, HEAD_DIM)
    wl = min(WINDOW, l)
    win_prompt = win.reshape(b, l, 2, NSA_KV_HEADS, HEAD_DIM)[:, l - wl:][None]

    hs, (q, kv, win, gates, r) = front(x_sample.reshape(db, d))
    qn3 = headnorm_bf16(q, 0, NSA_HEADS, qg, q_scale).reshape(db, NSA_HEADS, HEAD_DIM)
    slopes_b = jnp.broadcast_to(slopes[:, None], (NSA_HEADS, max(npg * (PAGE_ROWS // CMP_STRIDE),
                                                                 SLC_TOPK * SLC_BLOCK, state_win_kv.shape[2])))
    o_cmp, score = sample_cmp_select(cache_nsa_kv.reshape(n_pool, PAGE_ROWS, 4 * NSA_KV_HEADS, HEAD_DIM), page_table,
                                     qn3, slopes_b, pe, cw1, cb1, cw2, cb2, kg[0], pos0)
    idx = sample_topk(score.reshape(db * SUBLANE, -1), pos0).reshape(db, SUBLANE, LANE)
    idx = idx[:, :NSA_KV_HEADS, :SLC_TOPK]
    n_past = pos0 // SLC_BLOCK
    in_cache = idx < n_past
    pidx = jnp.minimum(idx, n_past - 1)
    per_page = PAGE_ROWS // SLC_BLOCK
    page = jnp.take_along_axis(page_table, (pidx // per_page).reshape(db, -1), axis=1).reshape(pidx.shape)
    row0 = (page * PAGE_ROWS + (pidx % per_page) * SLC_BLOCK).astype(jnp.int32).reshape(-1)
    offs = jnp.arange(SLC_BLOCK, dtype=jnp.int32)
    kpos = jnp.where(in_cache[..., None], idx[..., None] * SLC_BLOCK + offs, pos0 + 1).reshape(db, NSA_KV_HEADS, -1)
    o_slc = sample_slc_attention(cache_nsa_kv.reshape(n_pool * PAGE_ROWS, 4 * NSA_KV_HEADS, HEAD_DIM), row0, qn3, kpos,
                                 slopes_b, kv[:, 2 * half:].reshape(db, 2 * NSA_KV_HEADS, HEAD_DIM), kg[1], pos0)
    gates_b = jnp.broadcast_to(gates[:, :3 * NSA_HEADS].reshape(db, 3, NSA_HEADS, 1), (db, 3, NSA_HEADS, HEAD_DIM))
    wb = state_win_kv.shape[2]
    o_nsa, win_s = sample_win_attention(state_win_kv.reshape(db, wb, 2 * NSA_KV_HEADS, HEAD_DIM), qn3, slopes_b,
                                        win.reshape(db, 2 * NSA_KV_HEADS, HEAD_DIM), kg[2], gates_b, o_cmp, o_slc)
    o_ret, ret_s = sample_retention(state_ret[0], r.reshape(db, 4, RET_HEADS, HEAD_DIM), gn)
    y_sample = back(hs, o_nsa.reshape(db, -1), o_ret.reshape(db, -1)).reshape(db, 1, d)
    kv_sample = kv.reshape(1, db, 1, 4, NSA_KV_HEADS, HEAD_DIM)
    win_sample = win_s.reshape(1, db, wb, 2, NSA_KV_HEADS, HEAD_DIM)

    return (y_prompt, y_sample, kv_prompt, kv_sample, win_prompt, win_sample, ret_p[None], ret_s[None])
```

```python
import functools
import math

import numpy as np
import jax
import jax.numpy as jnp
from jax import lax
from jax.experimental import pallas as pl
from jax.experimental.pallas import tpu as pltpu

F32 = jnp.float32
BF16 = jnp.bfloat16

HEAD_DIM = 128
NSA_HEADS = 16
NSA_KV_HEADS = 4
NSA_GROUP = NSA_HEADS // NSA_KV_HEADS
RET_HEADS = 16
CMP_BLOCK = 32
CMP_STRIDE = 16
CMP_HIDDEN = 2 * HEAD_DIM
SLC_BLOCK = 64
SLC_TOPK = 16
WINDOW = 512
RET_CHUNK = 128
EPS = 1e-6
BIG = 1e30
NEG = -1e30
LANE = 128
SUBLANE = 8
VMEM_LIMIT = 56 * 1024 * 1024

GQ_W = NSA_GROUP * HEAD_DIM


def _cparams(sem):
    return pltpu.CompilerParams(dimension_semantics=sem, vmem_limit_bytes=VMEM_LIMIT)


def _tile(n, pref):
    if n <= pref:
        return n
    t = pref
    while n % t:
        t //= 2
    return t


def _dot(a, b):
    return jnp.dot(a, b, preferred_element_type=F32)


def _dot_nt(a, b):
    return lax.dot_general(a, b, (((1,), (1,)), ((), ())), preferred_element_type=F32)


def _gelu_tanh(x):
    return 0.5 * x * (1.0 + jnp.tanh(math.sqrt(2.0 / math.pi) * (x + 0.044715 * (x * x * x))))


def _silu(x):
    return x * (1.0 / (1.0 + jnp.exp(-x)))


def _sigmoid(x):
    return 1.0 / (1.0 + jnp.exp(-x))


def _row_rms(x, g):
    return x * lax.rsqrt(jnp.mean(x * x, axis=-1, keepdims=True) + EPS) * g


def _rmsnorm_kernel(x_ref, g_ref, o_ref):
    o_ref[...] = _row_rms(x_ref[...], g_ref[...]).astype(o_ref.dtype)


def rmsnorm_bf16(x, g):
    t, d = x.shape
    tr = _tile(t, 256)
    return pl.pallas_call(
        _rmsnorm_kernel,
        grid=(t // tr,),
        in_specs=[pl.BlockSpec((tr, d), lambda i: (i, 0)), pl.BlockSpec((1, d), lambda i: (0, 0))],
        out_specs=pl.BlockSpec((tr, d), lambda i: (i, 0)),
        out_shape=jax.ShapeDtypeStruct((t, d), BF16),
        compiler_params=_cparams(("parallel",)),
        name="rmsnorm",
    )(x, g.reshape(1, d))


def _mm_nt_kernel(a_ref, w_ref, o_ref):
    o_ref[...] = _dot_nt(a_ref[...], w_ref[...].astype(BF16))


def matmul_nt(a, wt, row0, n):
    m, k = a.shape
    tm = _tile(m, 1024)
    tn = _tile(n, 512)
    assert wt.shape[1] == k and row0 % SUBLANE == 0 and tn % SUBLANE == 0
    wspec = pl.BlockSpec((pl.Element(tn), pl.Element(k)),
                         lambda i, j: ((row0 // SUBLANE + j * (tn // SUBLANE)) * SUBLANE, 0))
    return pl.pallas_call(
        _mm_nt_kernel,
        grid=(m // tm, n // tn),
        in_specs=[pl.BlockSpec((tm, k), lambda i, j: (i, 0)), wspec],
        out_specs=pl.BlockSpec((tm, tn), lambda i, j: (i, j)),
        out_shape=jax.ShapeDtypeStruct((m, n), F32),
        compiler_params=_cparams(("parallel", "arbitrary")),
        name="in_proj",
    )(a, wt)


def _out_proj_kernel(a1_ref, a2_ref, w_ref, r_ref, o_ref, *, k1):
    acc = _dot(a1_ref[...], w_ref[pl.ds(0, k1), :].astype(BF16))
    acc = acc + _dot(a2_ref[...], w_ref[pl.ds(k1, w_ref.shape[0] - k1), :].astype(BF16))
    o_ref[...] = r_ref[...] + acc


def out_proj(a1, a2, w, res):
    m, k1 = a1.shape
    k = k1 + a2.shape[1]
    n = w.shape[2]
    tm = _tile(m, 1024)
    tn = _tile(n, 512)
    assert w.shape[:2] == (1, k)
    return pl.pallas_call(
        functools.partial(_out_proj_kernel, k1=k1),
        grid=(m // tm, n // tn),
        in_specs=[pl.BlockSpec((tm, k1), lambda i, j: (i, 0)), pl.BlockSpec((tm, k - k1), lambda i, j: (i, 0)),
                  pl.BlockSpec((None, k, tn), lambda i, j: (0, 0, j)), pl.BlockSpec((tm, tn), lambda i, j: (i, j))],
        out_specs=pl.BlockSpec((tm, tn), lambda i, j: (i, j)),
        out_shape=jax.ShapeDtypeStruct((m, n), F32),
        compiler_params=_cparams(("parallel", "arbitrary")),
        name="out_proj",
    )(a1, a2, w, res)


def _ffn_kernel(x_hbm, xs_hbm, g_ref, w1_ref, w3_ref, w2_ref, o_ref, os_ref, n_ref, sem, *, tm, ts, rb):
    i = pl.program_id(0)
    f = pl.program_id(1)

    def load_and_norm(src, dst_ref, row0, nrows, chunk, sem_slot):
        cp = pltpu.make_async_copy(src, dst_ref, sem.at[sem_slot])
        cp.start()
        cp.wait()

        def norm_rows(r, carry):
            lo = pl.multiple_of(r * chunk, chunk)
            n_ref[pl.ds(row0 + lo, chunk), :] = _row_rms(dst_ref[pl.ds(lo, chunk), :], g_ref[...]).astype(BF16)
            return carry

        lax.fori_loop(0, nrows // chunk, norm_rows, 0)

    @pl.when(f == 0)
    def _():
        load_and_norm(x_hbm.at[pl.ds(pl.multiple_of(i * tm, tm), tm), :], o_ref, 0, tm, rb, 0)

    @pl.when((f == 0) & (i == 0))
    def _():
        load_and_norm(xs_hbm, os_ref, tm, ts, ts, 1)

    def hidden(n):
        return (_silu(_dot(n, w1_ref[...].astype(BF16))) * _dot(n, w3_ref[...].astype(BF16)) * 0.5).astype(BF16)

    @pl.when(i == 0)
    def _():
        h = hidden(n_ref[...])
        o_ref[...] += _dot(h[:tm], w2_ref[...].astype(BF16))
        os_ref[...] += _dot(h[tm:], w2_ref[...].astype(BF16))

    @pl.when(i != 0)
    def _():
        o_ref[...] += _dot(hidden(n_ref[pl.ds(0, tm), :]), w2_ref[...].astype(BF16))


def ffn_half_step(x, xs, g, w1, w3, w2, which):
    t, d = x.shape
    ts = xs.shape[0]
    dff = w1.shape[3]
    tm = _tile(t, 1024)
    tf = _tile(dff, 256)
    return pl.pallas_call(
        functools.partial(_ffn_kernel, tm=tm, ts=ts, rb=_tile(tm, 64)),
        grid=(t // tm, dff // tf),
        in_specs=[
            pl.BlockSpec(memory_space=pl.ANY),
            pl.BlockSpec(memory_space=pl.ANY),
            pl.BlockSpec((1, d), lambda i, f: (0, 0)),
            pl.BlockSpec((None, None, d, tf), lambda i, f: (0, which, 0, f)),
            pl.BlockSpec((None, None, d, tf), lambda i, f: (0, which, 0, f)),
            pl.BlockSpec((None, None, tf, d), lambda i, f: (0, which, f, 0)),
        ],
        out_specs=[pl.BlockSpec((tm, d), lambda i, f: (i, 0), pipeline_mode=pl.Buffered(1)),
                   pl.BlockSpec((ts, d), lambda i, f: (0, 0))],
        out_shape=[jax.ShapeDtypeStruct((t, d), F32), jax.ShapeDtypeStruct((ts, d), F32)],
        scratch_shapes=[pltpu.VMEM((tm + ts, d), BF16), pltpu.SemaphoreType.DMA((2,))],
        compiler_params=_cparams(("arbitrary", "arbitrary")),
        name="ffn",
    )(x, xs, g.reshape(1, d), w1, w3, w2)


def _headnorm_kernel(x_ref, g_ref, o_ref, *, scale):
    o_ref[...] = (_row_rms(x_ref[...], g_ref[...]) * scale).astype(o_ref.dtype)


def headnorm_bf16(x, col0, nheads, g, scale=1.0):
    t = x.shape[0]
    tr = _tile(t, 1024)
    c0 = col0 // HEAD_DIM
    return pl.pallas_call(
        functools.partial(_headnorm_kernel, scale=scale),
        grid=(t // tr, nheads),
        in_specs=[pl.BlockSpec((tr, HEAD_DIM), lambda i, j: (i, c0 + j)),
                  pl.BlockSpec((1, HEAD_DIM), lambda i, j: (0, 0))],
        out_specs=pl.BlockSpec((tr, HEAD_DIM), lambda i, j: (i, j)),
        out_shape=jax.ShapeDtypeStruct((t, nheads * HEAD_DIM), BF16),
        compiler_params=_cparams(("parallel", "parallel")),
        name="headnorm",
    )(x, g.reshape(1, HEAD_DIM))


def _vt_kernel(x_ref, o_ref):
    o_ref[...] = x_ref[...].T.astype(o_ref.dtype)


def values_transposed(x, col0, b, l):
    tr = _tile(l, 512)
    nl = l // tr
    c0 = col0 // HEAD_DIM
    return pl.pallas_call(
        _vt_kernel,
        grid=(b, NSA_KV_HEADS, nl),
        in_specs=[pl.BlockSpec((tr, HEAD_DIM), lambda bi, h, i: (bi * nl + i, c0 + h))],
        out_specs=pl.BlockSpec((None, None, HEAD_DIM, tr), lambda bi, h, i: (bi, h, 0, i)),
        out_shape=jax.ShapeDtypeStruct((b, NSA_KV_HEADS, HEAD_DIM, l), BF16),
        compiler_params=_cparams(("parallel", "parallel", "parallel")),
        name="values_t",
    )(x)


def _compress_kernel(x_ref, pe_ref, w1_ref, b1_ref, w2_ref, b2_ref, kn_ref, o_ref, xs_ref, *, l, is_k):
    nc = l // CMP_STRIDE
    xs_ref[pl.ds(0, l), :] = x_ref[...]
    xs_ref[pl.ds(l, CMP_BLOCK), :] = jnp.zeros((CMP_BLOCK, HEAD_DIM), F32)
    acc = jnp.zeros((nc, CMP_HIDDEN), F32)
    for r in range(CMP_BLOCK):
        xr = xs_ref[pl.ds(r, nc, stride=CMP_STRIDE), :] + pe_ref[pl.ds(r, 1), :]
        acc = acc + _dot(xr.astype(BF16), w1_ref[r])
    hid = _gelu_tanh(acc + b1_ref[...])
    c = _dot(hid.astype(BF16), w2_ref[...]) + b2_ref[...]
    if is_k:
        o_ref[...] = _row_rms(c, kn_ref[...]).astype(o_ref.dtype)
    else:
        o_ref[...] = c.T.astype(o_ref.dtype)


def compress_prompt(kv, which, b, l, pe, w1, b1, w2, b2, kn):
    nc = l // CMP_STRIDE
    is_k = which == 0
    oshape = (b, NSA_KV_HEADS, nc, HEAD_DIM) if is_k else (b, NSA_KV_HEADS, HEAD_DIM, nc)
    oblock = (None, None, nc, HEAD_DIM) if is_k else (None, None, HEAD_DIM, nc)
    return pl.pallas_call(
        functools.partial(_compress_kernel, l=l, is_k=is_k),
        grid=(b, NSA_KV_HEADS),
        in_specs=[
            pl.BlockSpec((l, HEAD_DIM), lambda bi, h: (bi, which * NSA_KV_HEADS + h)),
            pl.BlockSpec((CMP_BLOCK, HEAD_DIM), lambda bi, h: (0, 0)),
            pl.BlockSpec((CMP_BLOCK, HEAD_DIM, CMP_HIDDEN), lambda bi, h: (0, 0, 0)),
            pl.BlockSpec((1, CMP_HIDDEN), lambda bi, h: (0, 0)),
            pl.BlockSpec((CMP_HIDDEN, HEAD_DIM), lambda bi, h: (0, 0)),
            pl.BlockSpec((1, HEAD_DIM), lambda bi, h: (0, 0)),
            pl.BlockSpec((1, HEAD_DIM), lambda bi, h: (0, 0)),
        ],
        out_specs=pl.BlockSpec(oblock, lambda bi, h: (bi, h, 0, 0)),
        out_shape=jax.ShapeDtypeStruct(oshape, BF16),
        scratch_shapes=[pltpu.VMEM((l + CMP_BLOCK, HEAD_DIM), F32)],
        compiler_params=_cparams(("parallel", "parallel")),
        name="compress_prompt",
    )(kv, pe, w1, b1.reshape(1, -1), w2, b2.reshape(1, -1), kn.reshape(1, -1))


def _select_topk(score, j, ntop):
    ns = score.shape[0]
    bias = jnp.full(score.shape, NEG, F32)
    for _ in range(ntop):
        mx = jnp.max(score, axis=0, keepdims=True)
        idx = jnp.min(jnp.where(score == mx, j, ns), axis=0, keepdims=True)
        hit = j == idx
        bias = jnp.where(hit, 0.0, bias)
        score = jnp.where(hit, -jnp.inf, score)
    return bias


def _cmp_attn_kernel(slope_ref, q_ref, kc_ref, vct_ref, mt_ref, o_ref, sel_ref, *, tq, nc, ns):
    kvh = pl.program_id(1)
    qi = pl.program_id(2)
    qp = qi * tq + lax.broadcasted_iota(jnp.int32, (nc, tq), 1)
    blk_end = lax.broadcasted_iota(jnp.int32, (nc, tq), 0) * CMP_STRIDE + (CMP_BLOCK - 1)
    dist = qp - blk_end
    mask = dist >= 0
    distf = dist.astype(F32)
    kc = kc_ref[...]
    vct = vct_ref[...]
    imp = jnp.zeros((nc, tq), F32)
    for g in range(NSA_GROUP):
        s = _dot_nt(kc, q_ref[:, g * HEAD_DIM:(g + 1) * HEAD_DIM])
        s = jnp.where(mask, s - slope_ref[kvh * NSA_GROUP + g] * distf, -jnp.inf)
        m = jnp.max(s, axis=0, keepdims=True)
        m = jnp.where(m == -jnp.inf, 0.0, m)
        e = jnp.where(mask, jnp.exp(s - m), 0.0)
        p = e / jnp.maximum(jnp.sum(e, axis=0, keepdims=True), 1e-30)
        imp = imp + p
        o_ref[g] = _dot(vct, p.astype(BF16))
    score = jnp.dot(mt_ref[...], imp, preferred_element_type=F32, precision=lax.Precision.HIGHEST)
    j = lax.broadcasted_iota(jnp.int32, (ns, tq), 0)
    cur = (qi * tq + lax.broadcasted_iota(jnp.int32, (ns, tq), 1)) // SLC_BLOCK
    forced = (j == 0) | (j == cur) | (j == cur - 1)
    score = jnp.where(j <= cur, jnp.where(forced, BIG, score), -BIG)
    sel_ref[...] = _select_topk(score, j, min(SLC_TOPK, ns))


def _overlap_matrix(n_cmp, n_slc):
    cs = np.arange(n_cmp)[:, None] * CMP_STRIDE
    ss = np.arange(n_slc)[None, :] * SLC_BLOCK
    ov = np.minimum(cs + CMP_BLOCK, ss + SLC_BLOCK) - np.maximum(cs, ss)
    return (np.maximum(ov, 0).astype(np.float32) / CMP_BLOCK)


def cmp_attention_prompt(qn, kc, vct, slopes, b, l):
    nc = l // CMP_STRIDE
    ns = l // SLC_BLOCK
    tq = _tile(l, 512)
    nq = l // tq
    mt = jnp.asarray(_overlap_matrix(nc, ns).T)
    return pl.pallas_call(
        functools.partial(_cmp_attn_kernel, tq=tq, nc=nc, ns=ns),
        grid=(b, NSA_KV_HEADS, nq),
        in_specs=[
            pl.BlockSpec(memory_space=pltpu.SMEM),
            pl.BlockSpec((tq, GQ_W), lambda bi, h, i: (bi * nq + i, h)),
            pl.BlockSpec((None, None, nc, HEAD_DIM), lambda bi, h, i: (bi, h, 0, 0)),
            pl.BlockSpec((None, None, HEAD_DIM, nc), lambda bi, h, i: (bi, h, 0, 0)),
            pl.BlockSpec((ns, nc), lambda bi, h, i: (0, 0)),
        ],
        out_specs=[
            pl.BlockSpec((None, NSA_GROUP, HEAD_DIM, tq), lambda bi, h, i: (bi, h, 0, i)),
            pl.BlockSpec((None, None, ns, tq), lambda bi, h, i: (bi, h, 0, i)),
        ],
        out_shape=[jax.ShapeDtypeStruct((b, NSA_HEADS, HEAD_DIM, l), F32),
                   jax.ShapeDtypeStruct((b, NSA_KV_HEADS, ns, l), F32)],
        compiler_params=_cparams(("parallel", "parallel", "parallel")),
        name="cmp_attention",
    )(slopes, qn, kc, vct, mt)


def _flash_kernel(slope_ref, q_ref, k_ref, vt_ref, *rest, tq, tk, nkt, windowed):
    if windowed:
        g_ref, oc_ref, os_ref, o_ref, m_ref, l_ref, acc_ref = rest
    else:
        sel_ref, o_ref, m_ref, l_ref, acc_ref = rest
    kvh = pl.program_id(1)
    qi = pl.program_id(2)
    kt = pl.program_id(3)
    if windowed:
        ktile = qi - (nkt - 1) + kt
        valid = ktile >= 0
    else:
        ktile = kt
        valid = kt * tk <= qi * tq + (tq - 1)

    @pl.when(kt == 0)
    def _():
        m_ref[...] = jnp.full(m_ref.shape, NEG, F32)
        l_ref[...] = jnp.zeros(l_ref.shape, F32)
        acc_ref[...] = jnp.zeros(acc_ref.shape, F32)

    @pl.when(valid)
    def _():
        kpos = ktile * tk + lax.broadcasted_iota(jnp.int32, (tk, tq), 0)
        qpos = qi * tq + lax.broadcasted_iota(jnp.int32, (tk, tq), 1)
        dist = qpos - kpos
        distf = dist.astype(F32)
        if windowed:
            bias = jnp.where((dist >= 0) & (dist < WINDOW), 0.0, NEG)
        else:
            sel = sel_ref[...]
            sel = jnp.concatenate(
                [jnp.broadcast_to(sel[r:r + 1, :], (SLC_BLOCK, tq)) for r in range(tk // SLC_BLOCK)], axis=0)
            bias = jnp.where(dist >= 0, sel, NEG)
        k = k_ref[...]
        vt = vt_ref[...]
        for g in range(NSA_GROUP):
            s = _dot_nt(k, q_ref[:, g * HEAD_DIM:(g + 1) * HEAD_DIM])
            s = s - slope_ref[kvh * NSA_GROUP + g] * distf + bias
            m_prev = m_ref[pl.ds(g, 1), :]
            m_new = jnp.maximum(m_prev, jnp.max(s, axis=0, keepdims=True))
            alpha = jnp.exp(m_prev - m_new)
            e = jnp.exp(s - m_new)
            l_ref[pl.ds(g, 1), :] = alpha * l_ref[pl.ds(g, 1), :] + jnp.sum(e, axis=0, keepdims=True)
            acc_ref[g] = alpha * acc_ref[g] + _dot(vt, e.astype(BF16))
            m_ref[pl.ds(g, 1), :] = m_new

    @pl.when(kt == nkt - 1)
    def _():
        if windowed:
            gates = _sigmoid(g_ref[...])
        for g in range(NSA_GROUP):
            o = acc_ref[g] / jnp.maximum(l_ref[pl.ds(g, 1), :], 1e-30)
            if windowed:
                o = (gates[g:g + 1, :] * oc_ref[g] + gates[NSA_GROUP + g:NSA_GROUP + g + 1, :] * os_ref[g]
                     + gates[2 * NSA_GROUP + g:2 * NSA_GROUP + g + 1, :] * o)
                o_ref[:, g * HEAD_DIM:(g + 1) * HEAD_DIM] = o.T.astype(o_ref.dtype)
            else:
                o_ref[g] = o


def flash_prompt(qn, kn, vt, slopes, b, l, sel=None, gated=None):
    windowed = sel is None
    tq = _tile(l, 512)
    tk = tq
    nq = l // tq
    nkt = (min(WINDOW, l) // tk + 1) if windowed else l // tk
    if windowed:
        kmap = lambda qi, kt: jnp.maximum(qi - (nkt - 1) + kt, 0)
    else:
        kmap = lambda qi, kt: jnp.minimum(kt, (qi * tq + tq - 1) // tk)
    in_specs = [
        pl.BlockSpec(memory_space=pltpu.SMEM),
        pl.BlockSpec((tq, GQ_W), lambda bi, h, qi, kt: (bi * nq + qi, h)),
        pl.BlockSpec((tk, HEAD_DIM), lambda bi, h, qi, kt: (bi * nq + kmap(qi, kt), h)),
        pl.BlockSpec((None, None, HEAD_DIM, tk), lambda bi, h, qi, kt: (bi, h, 0, kmap(qi, kt))),
    ]
    args = [slopes, qn, kn, vt]
    tspec = pl.BlockSpec((None, NSA_GROUP, HEAD_DIM, tq), lambda bi, h, qi, kt: (bi, h, 0, qi))
    if windowed:
        gates, o_cmp, o_slc = gated
        ngr = 3 * NSA_GROUP
        gk = gates[:, :3 * NSA_HEADS].reshape(b, l, 3, NSA_KV_HEADS, NSA_GROUP).transpose(0, 3, 2, 4, 1)
        gk = jnp.pad(gk.reshape(b, NSA_KV_HEADS, ngr, l), ((0, 0), (0, 0), (0, -ngr % SUBLANE), (0, 0)))
        in_specs += [pl.BlockSpec((None, None, gk.shape[2], tq), lambda bi, h, qi, kt: (bi, h, 0, qi)), tspec, tspec]
        args += [gk, o_cmp, o_slc]
        out_spec = pl.BlockSpec((tq, GQ_W), lambda bi, h, qi, kt: (bi * nq + qi, h))
        out_shape = jax.ShapeDtypeStruct((b * l, NSA_HEADS * HEAD_DIM), BF16)
    else:
        in_specs.append(pl.BlockSpec((None, None, tk // SLC_BLOCK, tq),
                                     lambda bi, h, qi, kt: (bi, h, kmap(qi, kt), qi)))
        args.append(sel)
        out_spec = tspec
        out_shape = jax.ShapeDtypeStruct((b, NSA_HEADS, HEAD_DIM, l), F32)
    return pl.pallas_call(
        functools.partial(_flash_kernel, tq=tq, tk=tk, nkt=nkt, windowed=windowed),
        grid=(b, NSA_KV_HEADS, nq, nkt),
        in_specs=in_specs,
        out_specs=out_spec,
        out_shape=out_shape,
        scratch_shapes=[pltpu.VMEM((SUBLANE, tq), F32), pltpu.VMEM((SUBLANE, tq), F32),
                        pltpu.VMEM((NSA_GROUP, HEAD_DIM, tq), F32)],
        compiler_params=_cparams(("parallel", "parallel", "parallel", "arbitrary")),
        name="window_attention" if windowed else "selected_attention",
    )(*args)


def _group_norm_gate(o, rg, gn):
    mu = jnp.mean(o, axis=-1, keepdims=True)
    var = jnp.mean(jnp.square(o - mu), axis=-1, keepdims=True)
    return _silu(rg) * ((o - mu) * lax.rsqrt(var + EPS) * gn)


def _retention_kernel(q_ref, k_ref, v_ref, rg_ref, intra_ref, qd_ref, kd_ref, cd_ref, gn_ref,
                      o_ref, s_ref, st_ref, *, tl, hb):
    li = pl.program_id(2)
    nl = pl.num_programs(2)

    @pl.when(li == 0)
    def _():
        st_ref[...] = jnp.zeros(st_ref.shape, F32)

    def chunk(c, carry):
        rows = pl.ds(pl.multiple_of(c * RET_CHUNK, RET_CHUNK), RET_CHUNK)
        for h in range(hb):
            cols = slice(h * HEAD_DIM, (h + 1) * HEAD_DIM)
            qc = q_ref[rows, cols]
            kc = k_ref[rows, cols] * (HEAD_DIM ** -0.5)
            vc = v_ref[rows, cols].astype(BF16)
            state = st_ref[h]
            att = _dot_nt(qc.astype(BF16), kc.astype(BF16)) * intra_ref[h]
            o = _dot(att.astype(BF16), vc) + _dot((qc * qd_ref[h]).astype(BF16), state.astype(BF16))
            st_ref[h] = state * cd_ref[h] + _dot((kc * kd_ref[h]).T.astype(BF16), vc)
            o_ref[rows, cols] = _group_norm_gate(o, rg_ref[rows, cols], gn_ref[:, cols]).astype(o_ref.dtype)
        return carry

    lax.fori_loop(0, tl // RET_CHUNK, chunk, 0)

    @pl.when(li == nl - 1)
    def _():
        s_ref[...] = st_ref[...]


def _retention_tables(chunk):
    h = jnp.arange(RET_HEADS, dtype=F32)
    log_g = jnp.log1p(-jnp.exp2(-5.0 - h))
    i = jnp.arange(chunk, dtype=F32)
    diff = i[:, None] - i[None, :]
    intra = jnp.where(diff >= 0, jnp.exp(jnp.maximum(diff, 0.0)[None] * log_g[:, None, None]), 0.0)
    q_dec = jnp.exp((i[None, :] + 1.0) * log_g[:, None])
    k_dec = jnp.exp((chunk - 1.0 - i)[None, :] * log_g[:, None])
    c_dec = jnp.exp(chunk * log_g)
    return intra, q_dec, k_dec, c_dec


def retention_prompt(r, gn, b, l):
    intra, q_dec, k_dec, c_dec = _retention_tables(RET_CHUNK)
    bc = lambda t: jnp.broadcast_to(t[:, :, None], (RET_HEADS, RET_CHUNK, HEAD_DIM))
    qd, kd = bc(q_dec), bc(k_dec)
    cd = jnp.broadcast_to(c_dec[:, None, None], (RET_HEADS, HEAD_DIM, HEAD_DIM))
    hb = 8
    ng = RET_HEADS // hb
    tl = _tile(l, 1024)
    nl = l // tl
    w = hb * HEAD_DIM
    rspec = lambda part: pl.BlockSpec((tl, w), lambda bi, g, li: (bi * nl + li, part * ng + g))
    tspec = pl.BlockSpec((hb, RET_CHUNK, HEAD_DIM), lambda bi, g, li: (g, 0, 0))
    return pl.pallas_call(
        functools.partial(_retention_kernel, tl=tl, hb=hb),
        grid=(b, ng, nl),
        in_specs=[rspec(0), rspec(1), rspec(2), rspec(3), tspec, tspec, tspec, tspec,
                  pl.BlockSpec((1, w), lambda bi, g, li: (0, g))],
        out_specs=[pl.BlockSpec((tl, w), lambda bi, g, li: (bi * nl + li, g)),
                   pl.BlockSpec((None, hb, HEAD_DIM, HEAD_DIM), lambda bi, g, li: (bi, g, 0, 0))],
        out_shape=[jax.ShapeDtypeStruct((b * l, RET_HEADS * HEAD_DIM), BF16),
                   jax.ShapeDtypeStruct((b, RET_HEADS, HEAD_DIM, HEAD_DIM), F32)],
        scratch_shapes=[pltpu.VMEM((hb, HEAD_DIM, HEAD_DIM), F32)],
        compiler_params=_cparams(("parallel", "parallel", "arbitrary")),
        name="retention_prompt",
    )(r, r, r, r, intra, qd, kd, cd, gn.reshape(1, -1))


def _group_rows(kvh):
    row = lax.broadcasted_iota(jnp.int32, (NSA_HEADS, 1), 0)
    return (row >= kvh * NSA_GROUP) & (row < (kvh + 1) * NSA_GROUP)


def _sample_cmp_kernel(pt_ref, *refs, pp, npg, pos0):
    pages = refs[:pp]
    (q_ref, slope_ref, pea_ref, peb_ref, wa_ref, wb_ref, b1_ref, w2_ref, b2_ref, kn_ref, m_ref,
     o_ref, sc_ref, xa_ref, xb_ref, f_ref, s_ref, c_ref) = refs[pp:]
    jp = pl.program_id(1)
    cpp = PAGE_ROWS // CMP_STRIDE
    nck = npg * cpp
    ng = 2 * NSA_KV_HEADS
    rows = pp * cpp * ng
    nrow = nck * ng

    def is_k(shape):
        return (lax.broadcasted_iota(jnp.int32, shape, 0) & NSA_KV_HEADS) == 0

    @pl.when(jp == 0)
    def _():
        s_ref[pl.ds(nrow, ng), :] = jnp.zeros((ng, CMP_HIDDEN), F32)

    for t in range(pp):
        for i in range(cpp):
            for c in range(CMP_STRIDE):
                dst = (pl.ds((t * cpp + i) * ng, ng), pl.ds(c * HEAD_DIM, HEAD_DIM))
                row = pages[t][i * CMP_STRIDE + c]
                xa_ref[dst] = row + pea_ref[:, c * HEAD_DIM:(c + 1) * HEAD_DIM]
                xb_ref[dst] = row + peb_ref[:, c * HEAD_DIM:(c + 1) * HEAD_DIM]
    fa = _dot(xa_ref[...].astype(BF16), wa_ref[...])
    sb = _dot(xb_ref[...].astype(BF16), wb_ref[...])
    kmask = is_k((rows, CMP_HIDDEN))
    dst = pl.ds(pl.multiple_of(jp * rows, rows), rows)
    f_ref[dst, :] = jnp.where(kmask, fa[:, :CMP_HIDDEN], fa[:, CMP_HIDDEN:])
    s_ref[dst, :] = jnp.where(kmask, sb[:, :CMP_HIDDEN], sb[:, CMP_HIDDEN:])

    @pl.when(jp == pl.num_programs(1) - 1)
    def _():
        eb = min(nrow, 1024)

        def cblock(r, carry):
            lo = pl.multiple_of(r * eb, eb)
            b1 = jnp.where(is_k((eb, CMP_HIDDEN)), b1_ref[0:1, :], b1_ref[1:2, :])
            hid = _gelu_tanh(f_ref[pl.ds(lo, eb), :] + s_ref[pl.ds(lo + ng, eb), :] + b1)
            c2 = _dot(hid.astype(BF16), w2_ref[...])
            c_ref[pl.ds(lo, eb), :] = jnp.where(is_k((eb, HEAD_DIM)), c2[:, :HEAD_DIM] + b2_ref[0:1, :],
                                                c2[:, HEAD_DIM:] + b2_ref[1:2, :])
            return carry

        lax.fori_loop(0, nrow // eb, cblock, 0)

        q = q_ref[...]
        n_io = lax.broadcasted_iota(jnp.int32, (NSA_HEADS, nck), 1)
        dist = pos0 - (n_io * CMP_STRIDE + (CMP_BLOCK - 1))
        mask = dist >= 0
        bias = slope_ref[:, :nck] * dist.astype(F32)
        o_acc = jnp.zeros((NSA_HEADS, HEAD_DIM), F32)
        imp_rows = []
        for kvh in range(NSA_KV_HEADS):
            kc = _row_rms(c_ref[pl.ds(kvh, nck, stride=ng), :], kn_ref[...]).astype(BF16)
            vc = c_ref[pl.ds(NSA_KV_HEADS + kvh, nck, stride=ng), :].astype(BF16)
            s = jnp.where(mask, _dot_nt(q, kc) - bias, -jnp.inf)
            m = jnp.max(s, axis=1, keepdims=True)
            m = jnp.where(m == -jnp.inf, 0.0, m)
            e = jnp.where(mask, jnp.exp(s - m), 0.0)
            p = e / jnp.maximum(jnp.sum(e, axis=1, keepdims=True), 1e-30)
            grp = _group_rows(kvh)
            o_acc = o_acc + jnp.where(grp, _dot(p.astype(BF16), vc), 0.0)
            imp_rows.append(jnp.sum(jnp.where(grp, p, 0.0), axis=0, keepdims=True))
        o_ref[...] = o_acc
        imp = jnp.concatenate(imp_rows + [jnp.zeros((SUBLANE - NSA_KV_HEADS, nck), F32)], axis=0)
        sc_ref[...] = jnp.dot(imp, m_ref[...], preferred_element_type=F32, precision=lax.Precision.HIGHEST)


def _sample_topk_kernel(sc_ref, idx_ref, *, pos0):
    score = sc_ref[...]
    rows, nsp = score.shape
    j = lax.broadcasted_iota(jnp.int32, (rows, nsp), 1)
    cur = pos0 // SLC_BLOCK
    forced = (j == 0) | (j == cur) | (j == cur - 1)
    score = jnp.where(j <= cur, jnp.where(forced, BIG, score), -BIG)
    lane = lax.broadcasted_iota(jnp.int32, (rows, LANE), 1)
    out = jnp.zeros((rows, LANE), jnp.int32)
    for t in range(SLC_TOPK):
        mx = jnp.max(score, axis=1, keepdims=True)
        idx = jnp.min(jnp.where(score == mx, j, nsp), axis=1, keepdims=True)
        out = jnp.where(lane == t, idx, out)
        score = jnp.where(j == idx, -jnp.inf, score)
    idx_ref[...] = out


def sample_topk(score, pos0):
    rows = score.shape[0]
    return pl.pallas_call(
        functools.partial(_sample_topk_kernel, pos0=pos0),
        out_shape=jax.ShapeDtypeStruct((rows, LANE), jnp.int32),
        name="sample_topk",
    )(score)


PAGE_ROWS = 128


def sample_cmp_select(cache4, page_table, qn3, slopes_b, pe, w1, b1, w2, b2, kn, pos0):
    db, npg = page_table.shape
    pp = _tile(npg, 8)
    cpp = PAGE_ROWS // CMP_STRIDE
    nck = npg * cpp
    ng = 2 * NSA_KV_HEADS
    n_slc = (pos0 + 1 + SLC_BLOCK - 1) // SLC_BLOCK
    nsp = -(-n_slc // LANE) * LANE
    m = np.zeros((nck, nsp), np.float32)
    m[:, :n_slc] = _overlap_matrix(nck, n_slc)
    kdim = CMP_STRIDE * HEAD_DIM
    pe_rows = lambda a: jnp.repeat(a.reshape(2, 1, kdim), NSA_KV_HEADS, axis=1).reshape(ng, kdim)
    pea, peb = pe_rows(pe[:, :CMP_STRIDE]), pe_rows(pe[:, CMP_STRIDE:])
    side = lambda a: jnp.concatenate([a[0], a[1]], axis=-1)
    wa = side(w1[:, :CMP_STRIDE].reshape(2, kdim, CMP_HIDDEN))
    wb = side(w1[:, CMP_STRIDE:].reshape(2, kdim, CMP_HIDDEN))
    const = lambda shape: pl.BlockSpec(shape, lambda bi, jp, pt: (0,) * len(shape))
    page_specs = [
        pl.BlockSpec((None, PAGE_ROWS, ng, HEAD_DIM),
                     functools.partial(lambda bi, jp, pt, t: (pt[bi, jp * pp + t], 0, 0, 0), t=t))
        for t in range(pp)]
    grid_spec = pltpu.PrefetchScalarGridSpec(
        num_scalar_prefetch=1,
        grid=(db, npg // pp),
        in_specs=page_specs + [
            pl.BlockSpec((None, NSA_HEADS, HEAD_DIM), lambda bi, jp, pt: (bi, 0, 0)),
            const((NSA_HEADS, slopes_b.shape[1])),
            const((ng, kdim)), const((ng, kdim)),
            const((kdim, 2 * CMP_HIDDEN)), const((kdim, 2 * CMP_HIDDEN)),
            const((2, CMP_HIDDEN)), const((CMP_HIDDEN, 2 * HEAD_DIM)), const((2, HEAD_DIM)),
            const((1, HEAD_DIM)), const((nck, nsp)),
        ],
        out_specs=[pl.BlockSpec((None, NSA_HEADS, HEAD_DIM), lambda bi, jp, pt: (bi, 0, 0)),
                   pl.BlockSpec((None, SUBLANE, nsp), lambda bi, jp, pt: (bi, 0, 0))],
        scratch_shapes=[
            pltpu.VMEM((pp * cpp * ng, kdim), F32),
            pltpu.VMEM((pp * cpp * ng, kdim), F32),
            pltpu.VMEM((nck * ng, CMP_HIDDEN), F32),
            pltpu.VMEM((nck * ng + ng, CMP_HIDDEN), F32),
            pltpu.VMEM((nck * ng, HEAD_DIM), F32),
        ],
    )
    return pl.pallas_call(
        functools.partial(_sample_cmp_kernel, pp=pp, npg=npg, pos0=pos0),
        grid_spec=grid_spec,
        out_shape=[jax.ShapeDtypeStruct((db, NSA_HEADS, HEAD_DIM), F32),
                   jax.ShapeDtypeStruct((db, SUBLANE, nsp), F32)],
        compiler_params=_cparams(("parallel", "arbitrary")),
        name="sample_cmp_select",
    )(page_table, *([cache4] * pp), qn3, slopes_b, pea, peb, wa, wb, b1, side(w2), b2, kn.reshape(1, -1),
      jnp.asarray(m))


def _softmax_with_new_key(s, s_new):
    m = jnp.maximum(jnp.max(s, axis=1, keepdims=True), s_new)
    e = jnp.exp(s - m)
    e_new = jnp.exp(s_new - m)
    return e, e_new, jnp.maximum(jnp.sum(e, axis=1, keepdims=True) + e_new, 1e-30)


def _bf16_round(x):
    return x.astype(BF16).astype(F32)


def _sample_slc_kernel(row_ref, cache_ref, q_ref, kpos_ref, slope_ref, new_ref, kn_ref, o_ref, kbuf, vbuf, sem, *, pos0):
    bi = pl.program_id(0)
    nb = pl.num_programs(0)
    nsel = SLC_TOPK * SLC_BLOCK
    kcol = 2 * NSA_KV_HEADS
    vcol = 3 * NSA_KV_HEADS

    def gather(b, slot, start):
        for kvh in range(NSA_KV_HEADS):
            for t in range(SLC_TOPK):
                src = pl.ds(row_ref[(b * NSA_KV_HEADS + kvh) * SLC_TOPK + t], SLC_BLOCK)
                dst = pl.ds(t * SLC_BLOCK, SLC_BLOCK)
                for col, buf in ((kcol, kbuf), (vcol, vbuf)):
                    cp = pltpu.make_async_copy(cache_ref.at[src, col + kvh, :], buf.at[slot, kvh, dst, :], sem.at[slot])
                    if start:
                        cp.start()
                    else:
                        cp.wait()

    slot = bi % 2

    @pl.when(bi == 0)
    def _():
        gather(bi, 0, True)

    @pl.when(bi + 1 < nb)
    def _():
        gather(bi + 1, 1 - slot, True)

    gather(bi, slot, False)

    q = q_ref[...]
    o_acc = jnp.zeros((NSA_HEADS, HEAD_DIM), F32)
    for kvh in range(NSA_KV_HEADS):
        kn = _row_rms(kbuf[slot, kvh], kn_ref[...]).astype(BF16)
        v = vbuf[slot, kvh]
        dist = pos0 - kpos_ref[pl.ds(kvh, 1), :]
        s = _dot_nt(q, kn) - slope_ref[:, :nsel] * dist.astype(F32)
        s = jnp.where(dist >= 0, s, NEG)
        knew = _row_rms(new_ref[pl.ds(kvh, 1), :], kn_ref[...])
        s_new = jnp.sum(q.astype(F32) * _bf16_round(knew), axis=1, keepdims=True)
        e, e_new, l = _softmax_with_new_key(s, s_new)
        vnew = new_ref[pl.ds(NSA_KV_HEADS + kvh, 1), :]
        o = _dot(e.astype(BF16), v.astype(BF16)) + _bf16_round(e_new) * _bf16_round(vnew)
        o_acc = o_acc + jnp.where(_group_rows(kvh), o / l, 0.0)
    o_ref[...] = o_acc


def sample_slc_attention(cache3, row0, qn3, kpos, slopes_b, new_kv, kn, pos0):
    db = qn3.shape[0]
    nsel = SLC_TOPK * SLC_BLOCK
    ng = 2 * NSA_KV_HEADS
    grid_spec = pltpu.PrefetchScalarGridSpec(
        num_scalar_prefetch=1,
        grid=(db,),
        in_specs=[
            pl.BlockSpec(memory_space=pl.ANY),
            pl.BlockSpec((None, NSA_HEADS, HEAD_DIM), lambda bi, r: (bi, 0, 0)),
            pl.BlockSpec((None, NSA_KV_HEADS, nsel), lambda bi, r: (bi, 0, 0)),
            pl.BlockSpec((NSA_HEADS, slopes_b.shape[1]), lambda bi, r: (0, 0)),
            pl.BlockSpec((None, ng, HEAD_DIM), lambda bi, r: (bi, 0, 0)),
            pl.BlockSpec((1, HEAD_DIM), lambda bi, r: (0, 0)),
        ],
        out_specs=pl.BlockSpec((None, NSA_HEADS, HEAD_DIM), lambda bi, r: (bi, 0, 0)),
        scratch_shapes=[pltpu.VMEM((2, NSA_KV_HEADS, nsel, HEAD_DIM), F32),
                        pltpu.VMEM((2, NSA_KV_HEADS, nsel, HEAD_DIM), F32),
                        pltpu.SemaphoreType.DMA((2,))],
    )
    return pl.pallas_call(
        functools.partial(_sample_slc_kernel, pos0=pos0),
        grid_spec=grid_spec,
        out_shape=jax.ShapeDtypeStruct((db, NSA_HEADS, HEAD_DIM), F32),
        compiler_params=_cparams(("arbitrary",)),
        name="sample_slc_attention",
    )(row0, cache3, qn3, kpos, slopes_b, new_kv, kn.reshape(1, -1))


def _sample_win_kernel(sw_ref, q_ref, slope_ref, new_ref, kn_ref, g_ref, oc_ref, os_ref, o_ref, nw_ref, *, wb):
    q = q_ref[...]
    row = lax.broadcasted_iota(jnp.int32, (NSA_HEADS, wb), 1)
    dist = wb - row
    valid = dist < WINDOW
    bias = slope_ref[:, :wb] * dist.astype(F32)
    o_acc = jnp.zeros((NSA_HEADS, HEAD_DIM), F32)
    for kvh in range(NSA_KV_HEADS):
        k = sw_ref[:, kvh, :]
        v = sw_ref[:, NSA_KV_HEADS + kvh, :]
        kn = _row_rms(k, kn_ref[...]).astype(BF16)
        s = jnp.where(valid, _dot_nt(q, kn) - bias, NEG)
        knew = _row_rms(new_ref[pl.ds(kvh, 1), :], kn_ref[...])
        s_new = jnp.sum(q.astype(F32) * _bf16_round(knew), axis=1, keepdims=True)
        e, e_new, l = _softmax_with_new_key(s, s_new)
        vnew = new_ref[pl.ds(NSA_KV_HEADS + kvh, 1), :]
        o = _dot(e.astype(BF16), v.astype(BF16)) + _bf16_round(e_new) * _bf16_round(vnew)
        o_acc = o_acc + jnp.where(_group_rows(kvh), o / l, 0.0)
    g = _sigmoid(g_ref[...])
    o_ref[...] = (g[0] * oc_ref[...] + g[1] * os_ref[...] + g[2] * o_acc).astype(o_ref.dtype)
    nw_ref[pl.ds(0, wb - 1)] = sw_ref[pl.ds(1, wb - 1)]
    nw_ref[wb - 1] = new_ref[...]


def sample_win_attention(sw, qn3, slopes_b, win_new, kn, gates_b, o_cmp, o_slc):
    db, wb, ng, _ = sw.shape
    hspec = pl.BlockSpec((None, NSA_HEADS, HEAD_DIM), lambda bi: (bi, 0, 0))
    wspec = pl.BlockSpec((None, wb, ng, HEAD_DIM), lambda bi: (bi, 0, 0, 0))
    return pl.pallas_call(
        functools.partial(_sample_win_kernel, wb=wb),
        grid=(db,),
        in_specs=[
            wspec,
            hspec,
            pl.BlockSpec((NSA_HEADS, slopes_b.shape[1]), lambda bi: (0, 0)),
            pl.BlockSpec((None, ng, HEAD_DIM), lambda bi: (bi, 0, 0)),
            pl.BlockSpec((1, HEAD_DIM), lambda bi: (0, 0)),
            pl.BlockSpec((None, 3, NSA_HEADS, HEAD_DIM), lambda bi: (bi, 0, 0, 0)),
            hspec, hspec,
        ],
        out_specs=[hspec, wspec],
        out_shape=[jax.ShapeDtypeStruct((db, NSA_HEADS, HEAD_DIM), BF16),
                   jax.ShapeDtypeStruct((db, wb, ng, HEAD_DIM), F32)],
        compiler_params=_cparams(("parallel",)),
        name="sample_win_attention",
    )(sw, qn3, slopes_b, win_new, kn.reshape(1, -1), gates_b, o_cmp, o_slc)


def _sample_ret_kernel(st_ref, r_ref, gam_ref, gn_ref, o_ref, s_ref):
    q = r_ref[0]
    k = r_ref[1] * (HEAD_DIM ** -0.5)
    v = r_ref[2]
    rg = r_ref[3]
    gam = gam_ref[...]
    qb, kb, vb = _bf16_round(q), _bf16_round(k), _bf16_round(v)
    att = jnp.sum(qb * kb, axis=1, keepdims=True)
    kt = kb.T
    qd = (q * gam).astype(BF16)
    rows = []
    for h in range(RET_HEADS):
        state = st_ref[h]
        rows.append(_dot(qd[h:h + 1, :], state.astype(BF16)))
        s_ref[h] = state * gam[h:h + 1, :] + kt[:, h:h + 1] * vb[h:h + 1, :]
    o = _bf16_round(att) * vb + jnp.concatenate(rows, axis=0)
    o_ref[...] = _group_norm_gate(o, rg, gn_ref[...]).astype(o_ref.dtype)


def sample_retention(state, r4, gn):
    db = state.shape[0]
    _, q_dec, _, _ = _retention_tables(1)
    gam = jnp.broadcast_to(q_dec, (RET_HEADS, HEAD_DIM))
    return pl.pallas_call(
        _sample_ret_kernel,
        grid=(db,),
        in_specs=[
            pl.BlockSpec((None, RET_HEADS, HEAD_DIM, HEAD_DIM), lambda bi: (bi, 0, 0, 0)),
            pl.BlockSpec((None, 4, RET_HEADS, HEAD_DIM), lambda bi: (bi, 0, 0, 0)),
            pl.BlockSpec((RET_HEADS, HEAD_DIM), lambda bi: (0, 0)),
            pl.BlockSpec((RET_HEADS, HEAD_DIM), lambda bi: (0, 0)),
        ],
        out_specs=[pl.BlockSpec((None, RET_HEADS, HEAD_DIM), lambda bi: (bi, 0, 0)),
                   pl.BlockSpec((None, RET_HEADS, HEAD_DIM, HEAD_DIM), lambda bi: (bi, 0, 0, 0))],
        out_shape=[jax.ShapeDtypeStruct((db, RET_HEADS, HEAD_DIM), BF16),
                   jax.ShapeDtypeStruct((db, RET_HEADS, HEAD_DIM, HEAD_DIM), F32)],
        compiler_params=_cparams(("parallel",)),
        name="sample_retention",
    )(state, r4, gam, gn.reshape(RET_HEADS, HEAD_DIM))


def _alibi_slopes():
    i = np.arange(NSA_HEADS, dtype=np.float32)
    return np.exp2(-8.0 * (i + 1.0) / NSA_HEADS).astype(np.float32)


def _in_proj(n, w_t):
    nq = NSA_HEADS * HEAD_DIM
    nkv = 4 * NSA_KV_HEADS * HEAD_DIM
    nwin = 2 * NSA_KV_HEADS * HEAD_DIM
    ng = 3 * NSA_HEADS
    nr = 4 * RET_HEADS * HEAD_DIM
    assert w_t.shape[0] == nq + nkv + nwin + ng + nr and ng <= LANE
    o_g = nq + nkv + nwin
    return [matmul_nt(n, w_t, 0, nq), matmul_nt(n, w_t, nq, nkv), matmul_nt(n, w_t, nq + nkv, nwin),
            matmul_nt(n, w_t, o_g, LANE), matmul_nt(n, w_t, o_g + ng, nr)]


def kernel(x_prompt, x_sample, cache_nsa_kv, state_win_kv, state_ret, page_table, norm_g, ffn_w1, ffn_w3, ffn_w2,
           w_in, nsa_q_norm, nsa_k_norm, cmp_pe, cmp_w1, cmp_b1, cmp_w2, cmp_b2, ret_gn_g, w_out):
    assert cache_nsa_kv.shape[0] == 1 and x_sample.shape[1] == 1
    b, l, d = x_prompt.shape
    db = x_sample.shape[0]
    npg = page_table.shape[1]
    n_pool = cache_nsa_kv.shape[1]
    assert cache_nsa_kv.shape[2] == PAGE_ROWS
    pos0 = npg * PAGE_ROWS
    half = NSA_KV_HEADS * HEAD_DIM

    w_t = jnp.swapaxes(w_in, 1, 2)[0]
    ng = norm_g[0]
    qg, kg = nsa_q_norm[0], nsa_k_norm[0]
    pe = cmp_pe[0]
    cw1 = cmp_w1[0].astype(BF16)
    cb1 = cmp_b1[0]
    cw2 = cmp_w2[0].astype(BF16)
    cb2 = cmp_b2[0]
    gn = ret_gn_g[0]
    slopes = jnp.asarray(_alibi_slopes())
    q_scale = HEAD_DIM ** -0.5

    hp, hs = ffn_half_step(x_prompt.reshape(b * l, d), x_sample.reshape(db, d), ng[0], ffn_w1, ffn_w3, ffn_w2, 0)

    q, kv, win, gates, r = _in_proj(rmsnorm_bf16(hp, ng[1]), w_t)
    qn = headnorm_bf16(q, 0, NSA_HEADS, qg, q_scale)
    kc = compress_prompt(kv, 0, b, l, pe[0], cw1[0], cb1[0], cw2[0], cb2[0], kg[0])
    vct = compress_prompt(kv, 1, b, l, pe[1], cw1[1], cb1[1], cw2[1], cb2[1], kg[0])
    o_cmp, sel = cmp_attention_prompt(qn, kc, vct, slopes, b, l)
    k_slc = headnorm_bf16(kv, 2 * half, NSA_KV_HEADS, kg[1])
    v_slc = values_transposed(kv, 3 * half, b, l)
    o_slc = flash_prompt(qn, k_slc, v_slc, slopes, b, l, sel=sel)
    k_win = headnorm_bf16(win, 0, NSA_KV_HEADS, kg[2])
    v_win = values_transposed(win, half, b, l)
    o_nsa = flash_prompt(qn, k_win, v_win, slopes, b, l, gated=(gates, o_cmp, o_slc))
    o_ret, ret_p = retention_prompt(r, gn, b, l)
    hp = out_proj(o_nsa, o_ret, w_out, hp)
    kv_prompt = kv.reshape(1, b, l, 4, NSA_KV_HEADS, HEAD_DIM)
    wl = min(WINDOW, l)
    win_prompt = win.reshape(b, l, 2, NSA_KV_HEADS, HEAD_DIM)[:, l - wl:][None]

    q, kv, win, gates, r = _in_proj(rmsnorm_bf16(hs, ng[1]), w_t)
    qn3 = headnorm_bf16(q, 0, NSA_HEADS, qg, q_scale).reshape(db, NSA_HEADS, HEAD_DIM)
    slopes_b = jnp.broadcast_to(slopes[:, None], (NSA_HEADS, max(npg * (PAGE_ROWS // CMP_STRIDE),
                                                                 SLC_TOPK * SLC_BLOCK, state_win_kv.shape[2])))
    o_cmp, score = sample_cmp_select(cache_nsa_kv.reshape(n_pool, PAGE_ROWS, 4 * NSA_KV_HEADS, HEAD_DIM), page_table,
                                     qn3, slopes_b, pe, cw1, cb1, cw2, cb2, kg[0], pos0)
    idx = sample_topk(score.reshape(db * SUBLANE, -1), pos0).reshape(db, SUBLANE, LANE)
    idx = idx[:, :NSA_KV_HEADS, :SLC_TOPK]
    n_past = pos0 // SLC_BLOCK
    in_cache = idx < n_past
    pidx = jnp.minimum(idx, n_past - 1)
    per_page = PAGE_ROWS // SLC_BLOCK
    page = jnp.take_along_axis(page_table, (pidx // per_page).reshape(db, -1), axis=1).reshape(pidx.shape)
    row0 = (page * PAGE_ROWS + (pidx % per_page) * SLC_BLOCK).astype(jnp.int32).reshape(-1)
    offs = jnp.arange(SLC_BLOCK, dtype=jnp.int32)
    kpos = jnp.where(in_cache[..., None], idx[..., None] * SLC_BLOCK + offs, pos0 + 1).reshape(db, NSA_KV_HEADS, -1)
    o_slc = sample_slc_attention(cache_nsa_kv.reshape(n_pool * PAGE_ROWS, 4 * NSA_KV_HEADS, HEAD_DIM), row0, qn3, kpos,
                                 slopes_b, kv[:, 2 * half:].reshape(db, 2 * NSA_KV_HEADS, HEAD_DIM), kg[1], pos0)
    gates_b = jnp.broadcast_to(gates[:, :3 * NSA_HEADS].reshape(db, 3, NSA_HEADS, 1), (db, 3, NSA_HEADS, HEAD_DIM))
    wb = state_win_kv.shape[2]
    o_nsa, win_s = sample_win_attention(state_win_kv.reshape(db, wb, 2 * NSA_KV_HEADS, HEAD_DIM), qn3, slopes_b,
                                        win.reshape(db, 2 * NSA_KV_HEADS, HEAD_DIM), kg[2], gates_b, o_cmp, o_slc)
    o_ret, ret_s = sample_retention(state_ret[0], r.reshape(db, 4, RET_HEADS, HEAD_DIM), gn)
    hs = out_proj(o_nsa.reshape(db, -1), o_ret.reshape(db, -1), w_out, hs)
    y_prompt, y_sample = ffn_half_step(hp, hs, ng[2], ffn_w1, ffn_w3, ffn_w2, 1)
    y_prompt = y_prompt.reshape(b, l, d)
    y_sample = y_sample.reshape(db, 1, d)
    kv_sample = kv.reshape(1, db, 1, 4, NSA_KV_HEADS, HEAD_DIM)
    win_sample = win_s.reshape(1, db, wb, 2, NSA_KV_HEADS, HEAD_DIM)

    return (y_prompt, y_sample, kv_prompt, kv_sample, win_prompt, win_sample, ret_p[None], ret_s[None])
```

```python
import functools
import math

import numpy as np
import jax
import jax.numpy as jnp
from jax import lax
from jax.experimental import pallas as pl
from jax.experimental.pallas import tpu as pltpu

F32 = jnp.float32
BF16 = jnp.bfloat16

HEAD_DIM = 128
NSA_HEADS = 16
NSA_KV_HEADS = 4
NSA_GROUP = NSA_HEADS // NSA_KV_HEADS
RET_HEADS = 16
CMP_BLOCK = 32
CMP_STRIDE = 16
CMP_HIDDEN = 2 * HEAD_DIM
SLC_BLOCK = 64
SLC_TOPK = 16
WINDOW = 512
RET_CHUNK = 128
EPS = 1e-6
BIG = 1e30
NEG = -1e30
LANE = 128
SUBLANE = 8
VMEM_LIMIT = 56 * 1024 * 1024

GQ_W = NSA_GROUP * HEAD_DIM


def _cparams(sem):
    return pltpu.CompilerParams(dimension_semantics=sem, vmem_limit_bytes=VMEM_LIMIT)


def _tile(n, pref):
    if n <= pref:
        return n
    t = pref
    while n % t:
        t //= 2
    return t


def _dot(a, b):
    return jnp.dot(a, b, preferred_element_type=F32)


def _dot_nt(a, b):
    return lax.dot_general(a, b, (((1,), (1,)), ((), ())), preferred_element_type=F32)


def _gelu_tanh(x):
    return 0.5 * x * (1.0 + jnp.tanh(math.sqrt(2.0 / math.pi) * (x + 0.044715 * (x * x * x))))


def _silu(x):
    return x * (1.0 / (1.0 + jnp.exp(-x)))


def _sigmoid(x):
    return 1.0 / (1.0 + jnp.exp(-x))


def _row_rms(x, g):
    return x * lax.rsqrt(jnp.mean(x * x, axis=-1, keepdims=True) + EPS) * g


def _rmsnorm_kernel(x_ref, g_ref, o_ref):
    o_ref[...] = _row_rms(x_ref[...], g_ref[...]).astype(o_ref.dtype)


def rmsnorm_bf16(x, g):
    t, d = x.shape
    tr = _tile(t, 256)
    return pl.pallas_call(
        _rmsnorm_kernel,
        grid=(t // tr,),
        in_specs=[pl.BlockSpec((tr, d), lambda i: (i, 0)), pl.BlockSpec((1, d), lambda i: (0, 0))],
        out_specs=pl.BlockSpec((tr, d), lambda i: (i, 0)),
        out_shape=jax.ShapeDtypeStruct((t, d), BF16),
        compiler_params=_cparams(("parallel",)),
        name="rmsnorm",
    )(x, g.reshape(1, d))


def _mm_nt_kernel(a_ref, w_ref, o_ref):
    o_ref[...] = _dot_nt(a_ref[...], w_ref[...].astype(BF16))


def matmul_nt(a, wt, row0, n):
    m, k = a.shape
    tm = _tile(m, 1024)
    tn = _tile(n, 512)
    assert wt.shape[1] == k and row0 % SUBLANE == 0 and tn % SUBLANE == 0
    wspec = pl.BlockSpec((pl.Element(tn), pl.Element(k)),
                         lambda i, j: ((row0 // SUBLANE + j * (tn // SUBLANE)) * SUBLANE, 0))
    return pl.pallas_call(
        _mm_nt_kernel,
        grid=(m // tm, n // tn),
        in_specs=[pl.BlockSpec((tm, k), lambda i, j: (i, 0)), wspec],
        out_specs=pl.BlockSpec((tm, tn), lambda i, j: (i, j)),
        out_shape=jax.ShapeDtypeStruct((m, n), F32),
        compiler_params=_cparams(("parallel", "arbitrary")),
        name="in_proj",
    )(a, wt)


def _out_proj_kernel(a1_ref, a2_ref, w_ref, r_ref, o_ref, *, k1):
    acc = _dot(a1_ref[...], w_ref[pl.ds(0, k1), :].astype(BF16))
    acc = acc + _dot(a2_ref[...], w_ref[pl.ds(k1, w_ref.shape[0] - k1), :].astype(BF16))
    o_ref[...] = r_ref[...] + acc


def out_proj(a1, a2, w, res):
    m, k1 = a1.shape
    k = k1 + a2.shape[1]
    n = w.shape[2]
    tm = _tile(m, 1024)
    tn = _tile(n, 512)
    assert w.shape[:2] == (1, k)
    return pl.pallas_call(
        functools.partial(_out_proj_kernel, k1=k1),
        grid=(m // tm, n // tn),
        in_specs=[pl.BlockSpec((tm, k1), lambda i, j: (i, 0)), pl.BlockSpec((tm, k - k1), lambda i, j: (i, 0)),
                  pl.BlockSpec((None, k, tn), lambda i, j: (0, 0, j)), pl.BlockSpec((tm, tn), lambda i, j: (i, j))],
        out_specs=pl.BlockSpec((tm, tn), lambda i, j: (i, j)),
        out_shape=jax.ShapeDtypeStruct((m, n), F32),
        compiler_params=_cparams(("parallel", "arbitrary")),
        name="out_proj",
    )(a1, a2, w, res)


def _ffn_kernel(x_hbm, xs_hbm, g_ref, w1_ref, w3_ref, w2_ref, o_ref, os_ref, n_ref, sem, *, tm, ts, rb):
    i = pl.program_id(0)
    f = pl.program_id(1)

    def load_and_norm(src, dst_ref, row0, nrows, chunk, sem_slot):
        cp = pltpu.make_async_copy(src, dst_ref, sem.at[sem_slot])
        cp.start()
        cp.wait()

        def norm_rows(r, carry):
            lo = pl.multiple_of(r * chunk, chunk)
            n_ref[pl.ds(row0 + lo, chunk), :] = _row_rms(dst_ref[pl.ds(lo, chunk), :], g_ref[...]).astype(BF16)
            return carry

        lax.fori_loop(0, nrows // chunk, norm_rows, 0)

    @pl.when(f == 0)
    def _():
        load_and_norm(x_hbm.at[pl.ds(pl.multiple_of(i * tm, tm), tm), :], o_ref, 0, tm, rb, 0)

    @pl.when((f == 0) & (i == 0))
    def _():
        load_and_norm(xs_hbm, os_ref, tm, ts, ts, 1)

    def hidden(n):
        return (_silu(_dot(n, w1_ref[...].astype(BF16))) * _dot(n, w3_ref[...].astype(BF16)) * 0.5).astype(BF16)

    @pl.when(i == 0)
    def _():
        h = hidden(n_ref[...])
        o_ref[...] += _dot(h[:tm], w2_ref[...].astype(BF16))
        os_ref[...] += _dot(h[tm:], w2_ref[...].astype(BF16))

    @pl.when(i != 0)
    def _():
        o_ref[...] += _dot(hidden(n_ref[pl.ds(0, tm), :]), w2_ref[...].astype(BF16))


def ffn_half_step(x, xs, g, w1, w3, w2, which):
    t, d = x.shape
    ts = xs.shape[0]
    dff = w1.shape[3]
    tm = _tile(t, 1024)
    tf = _tile(dff, 256)
    return pl.pallas_call(
        functools.partial(_ffn_kernel, tm=tm, ts=ts, rb=_tile(tm, 64)),
        grid=(t // tm, dff // tf),
        in_specs=[
            pl.BlockSpec(memory_space=pl.ANY),
            pl.BlockSpec(memory_space=pl.ANY),
            pl.BlockSpec((1, d), lambda i, f: (0, 0)),
            pl.BlockSpec((None, None, d, tf), lambda i, f: (0, which, 0, f)),
            pl.BlockSpec((None, None, d, tf), lambda i, f: (0, which, 0, f)),
            pl.BlockSpec((None, None, tf, d), lambda i, f: (0, which, f, 0)),
        ],
        out_specs=[pl.BlockSpec((tm, d), lambda i, f: (i, 0), pipeline_mode=pl.Buffered(1)),
                   pl.BlockSpec((ts, d), lambda i, f: (0, 0))],
        out_shape=[jax.ShapeDtypeStruct((t, d), F32), jax.ShapeDtypeStruct((ts, d), F32)],
        scratch_shapes=[pltpu.VMEM((tm + ts, d), BF16), pltpu.SemaphoreType.DMA((2,))],
        compiler_params=_cparams(("arbitrary", "arbitrary")),
        name="ffn",
    )(x, xs, g.reshape(1, d), w1, w3, w2)


HEADS_PER_BLOCK = NSA_KV_HEADS


def _headnorm_kernel(x_ref, g_ref, o_ref, *, scale):
    for h in range(HEADS_PER_BLOCK):
        cols = slice(h * HEAD_DIM, (h + 1) * HEAD_DIM)
        o_ref[:, cols] = (_row_rms(x_ref[:, cols], g_ref[...]) * scale).astype(o_ref.dtype)


def headnorm_bf16(x, col0, nheads, g, scale=1.0):
    t = x.shape[0]
    tr = _tile(t, 512)
    w = HEADS_PER_BLOCK * HEAD_DIM
    assert nheads % HEADS_PER_BLOCK == 0 and col0 % w == 0
    c0 = col0 // w
    return pl.pallas_call(
        functools.partial(_headnorm_kernel, scale=scale),
        grid=(t // tr, nheads // HEADS_PER_BLOCK),
        in_specs=[pl.BlockSpec((tr, w), lambda i, j: (i, c0 + j)),
                  pl.BlockSpec((1, HEAD_DIM), lambda i, j: (0, 0))],
        out_specs=pl.BlockSpec((tr, w), lambda i, j: (i, j)),
        out_shape=jax.ShapeDtypeStruct((t, nheads * HEAD_DIM), BF16),
        compiler_params=_cparams(("parallel", "parallel")),
        name="headnorm",
    )(x, g.reshape(1, HEAD_DIM))


def _vt_kernel(x_ref, o_ref):
    for h in range(NSA_KV_HEADS):
        o_ref[h] = x_ref[:, h * HEAD_DIM:(h + 1) * HEAD_DIM].T.astype(o_ref.dtype)


def values_transposed(x, col0, b, l):
    tr = _tile(l, 512)
    nl = l // tr
    w = NSA_KV_HEADS * HEAD_DIM
    assert col0 % w == 0
    c0 = col0 // w
    return pl.pallas_call(
        _vt_kernel,
        grid=(b, nl),
        in_specs=[pl.BlockSpec((tr, w), lambda bi, i: (bi * nl + i, c0))],
        out_specs=pl.BlockSpec((None, NSA_KV_HEADS, HEAD_DIM, tr), lambda bi, i: (bi, 0, 0, i)),
        out_shape=jax.ShapeDtypeStruct((b, NSA_KV_HEADS, HEAD_DIM, l), BF16),
        compiler_params=_cparams(("parallel", "parallel")),
        name="values_t",
    )(x)


def _compress_kernel(x_ref, pe_ref, w1_ref, b1_ref, w2_ref, b2_ref, kn_ref, o_ref, xs_ref, *, l, is_k):
    nc = l // CMP_STRIDE
    xs_ref[pl.ds(0, l), :] = x_ref[...]
    xs_ref[pl.ds(l, CMP_BLOCK), :] = jnp.zeros((CMP_BLOCK, HEAD_DIM), F32)
    acc = jnp.zeros((nc, CMP_HIDDEN), F32)
    for r in range(CMP_BLOCK):
        xr = xs_ref[pl.ds(r, nc, stride=CMP_STRIDE), :] + pe_ref[pl.ds(r, 1), :]
        acc = acc + _dot(xr.astype(BF16), w1_ref[r])
    hid = _gelu_tanh(acc + b1_ref[...])
    c = _dot(hid.astype(BF16), w2_ref[...]) + b2_ref[...]
    if is_k:
        o_ref[...] = _row_rms(c, kn_ref[...]).astype(o_ref.dtype)
    else:
        o_ref[...] = c.T.astype(o_ref.dtype)


def compress_prompt(kv, which, b, l, pe, w1, b1, w2, b2, kn):
    nc = l // CMP_STRIDE
    is_k = which == 0
    oshape = (b, NSA_KV_HEADS, nc, HEAD_DIM) if is_k else (b, NSA_KV_HEADS, HEAD_DIM, nc)
    oblock = (None, None, nc, HEAD_DIM) if is_k else (None, None, HEAD_DIM, nc)
    return pl.pallas_call(
        functools.partial(_compress_kernel, l=l, is_k=is_k),
        grid=(b, NSA_KV_HEADS),
        in_specs=[
            pl.BlockSpec((l, HEAD_DIM), lambda bi, h: (bi, which * NSA_KV_HEADS + h)),
            pl.BlockSpec((CMP_BLOCK, HEAD_DIM), lambda bi, h: (0, 0)),
            pl.BlockSpec((CMP_BLOCK, HEAD_DIM, CMP_HIDDEN), lambda bi, h: (0, 0, 0)),
            pl.BlockSpec((1, CMP_HIDDEN), lambda bi, h: (0, 0)),
            pl.BlockSpec((CMP_HIDDEN, HEAD_DIM), lambda bi, h: (0, 0)),
            pl.BlockSpec((1, HEAD_DIM), lambda bi, h: (0, 0)),
            pl.BlockSpec((1, HEAD_DIM), lambda bi, h: (0, 0)),
        ],
        out_specs=pl.BlockSpec(oblock, lambda bi, h: (bi, h, 0, 0)),
        out_shape=jax.ShapeDtypeStruct(oshape, BF16),
        scratch_shapes=[pltpu.VMEM((l + CMP_BLOCK, HEAD_DIM), F32)],
        compiler_params=_cparams(("parallel", "parallel")),
        name="compress_prompt",
    )(kv, pe, w1, b1.reshape(1, -1), w2, b2.reshape(1, -1), kn.reshape(1, -1))


def _select_topk(score, j, ntop):
    ns = score.shape[0]
    bias = jnp.full(score.shape, NEG, F32)
    for _ in range(ntop):
        mx = jnp.max(score, axis=0, keepdims=True)
        idx = jnp.min(jnp.where(score == mx, j, ns), axis=0, keepdims=True)
        hit = j == idx
        bias = jnp.where(hit, 0.0, bias)
        score = jnp.where(hit, -jnp.inf, score)
    return bias


def _cmp_attn_kernel(slope_ref, q_ref, kc_ref, vct_ref, mt_ref, o_ref, sel_ref, *, tq, nc, ns):
    kvh = pl.program_id(1)
    qi = pl.program_id(2)
    qp = qi * tq + lax.broadcasted_iota(jnp.int32, (nc, tq), 1)
    blk_end = lax.broadcasted_iota(jnp.int32, (nc, tq), 0) * CMP_STRIDE + (CMP_BLOCK - 1)
    dist = qp - blk_end
    mask = dist >= 0
    distf = dist.astype(F32)
    kc = kc_ref[...]
    vct = vct_ref[...]
    imp = jnp.zeros((nc, tq), F32)
    for g in range(NSA_GROUP):
        s = _dot_nt(kc, q_ref[:, g * HEAD_DIM:(g + 1) * HEAD_DIM])
        s = jnp.where(mask, s - slope_ref[kvh * NSA_GROUP + g] * distf, -jnp.inf)
        m = jnp.max(s, axis=0, keepdims=True)
        m = jnp.where(m == -jnp.inf, 0.0, m)
        e = jnp.where(mask, jnp.exp(s - m), 0.0)
        p = e / jnp.maximum(jnp.sum(e, axis=0, keepdims=True), 1e-30)
        imp = imp + p
        o_ref[g] = _dot(vct, p.astype(BF16))
    score = jnp.dot(mt_ref[...], imp, preferred_element_type=F32, precision=lax.Precision.HIGHEST)
    j = lax.broadcasted_iota(jnp.int32, (ns, tq), 0)
    cur = (qi * tq + lax.broadcasted_iota(jnp.int32, (ns, tq), 1)) // SLC_BLOCK
    forced = (j == 0) | (j == cur) | (j == cur - 1)
    score = jnp.where(j <= cur, jnp.where(forced, BIG, score), -BIG)
    sel_ref[...] = _select_topk(score, j, min(SLC_TOPK, ns))


def _overlap_matrix(n_cmp, n_slc):
    cs = np.arange(n_cmp)[:, None] * CMP_STRIDE
    ss = np.arange(n_slc)[None, :] * SLC_BLOCK
    ov = np.minimum(cs + CMP_BLOCK, ss + SLC_BLOCK) - np.maximum(cs, ss)
    return (np.maximum(ov, 0).astype(np.float32) / CMP_BLOCK)


def cmp_attention_prompt(qn, kc, vct, slopes, b, l):
    nc = l // CMP_STRIDE
    ns = l // SLC_BLOCK
    tq = _tile(l, 512)
    nq = l // tq
    mt = jnp.asarray(_overlap_matrix(nc, ns).T)
    return pl.pallas_call(
        functools.partial(_cmp_attn_kernel, tq=tq, nc=nc, ns=ns),
        grid=(b, NSA_KV_HEADS, nq),
        in_specs=[
            pl.BlockSpec(memory_space=pltpu.SMEM),
            pl.BlockSpec((tq, GQ_W), lambda bi, h, i: (bi * nq + i, h)),
            pl.BlockSpec((None, None, nc, HEAD_DIM), lambda bi, h, i: (bi, h, 0, 0)),
            pl.BlockSpec((None, None, HEAD_DIM, nc), lambda bi, h, i: (bi, h, 0, 0)),
            pl.BlockSpec((ns, nc), lambda bi, h, i: (0, 0)),
        ],
        out_specs=[
            pl.BlockSpec((None, NSA_GROUP, HEAD_DIM, tq), lambda bi, h, i: (bi, h, 0, i)),
            pl.BlockSpec((None, None, ns, tq), lambda bi, h, i: (bi, h, 0, i)),
        ],
        out_shape=[jax.ShapeDtypeStruct((b, NSA_HEADS, HEAD_DIM, l), F32),
                   jax.ShapeDtypeStruct((b, NSA_KV_HEADS, ns, l), F32)],
        compiler_params=_cparams(("parallel", "parallel", "parallel")),
        name="cmp_attention",
    )(slopes, qn, kc, vct, mt)


def _flash_kernel(slope_ref, q_ref, k_ref, vt_ref, *rest, tq, tk, nkt, windowed):
    if windowed:
        g_ref, oc_ref, os_ref, o_ref, m_ref, l_ref, acc_ref = rest
    else:
        sel_ref, o_ref, m_ref, l_ref, acc_ref = rest
    kvh = pl.program_id(1)
    qi = pl.program_id(2)
    kt = pl.program_id(3)
    if windowed:
        ktile = qi - (nkt - 1) + kt
        valid = ktile >= 0
    else:
        ktile = kt
        valid = kt * tk <= qi * tq + (tq - 1)

    @pl.when(kt == 0)
    def _():
        m_ref[...] = jnp.full(m_ref.shape, NEG, F32)
        l_ref[...] = jnp.zeros(l_ref.shape, F32)
        acc_ref[...] = jnp.zeros(acc_ref.shape, F32)

    @pl.when(valid)
    def _():
        kpos = ktile * tk + lax.broadcasted_iota(jnp.int32, (tk, tq), 0)
        qpos = qi * tq + lax.broadcasted_iota(jnp.int32, (tk, tq), 1)
        dist = qpos - kpos
        distf = dist.astype(F32)
        if windowed:
            bias = jnp.where((dist >= 0) & (dist < WINDOW), 0.0, NEG)
        else:
            sel = sel_ref[...]
            sel = jnp.concatenate(
                [jnp.broadcast_to(sel[r:r + 1, :], (SLC_BLOCK, tq)) for r in range(tk // SLC_BLOCK)], axis=0)
            bias = jnp.where(dist >= 0, sel, NEG)
        k = k_ref[...]
        vt = vt_ref[...]
        for g in range(NSA_GROUP):
            s = _dot_nt(k, q_ref[:, g * HEAD_DIM:(g + 1) * HEAD_DIM])
            s = s - slope_ref[kvh * NSA_GROUP + g] * distf + bias
            m_prev = m_ref[pl.ds(g, 1), :]
            m_new = jnp.maximum(m_prev, jnp.max(s, axis=0, keepdims=True))
            alpha = jnp.exp(m_prev - m_new)
            e = jnp.exp(s - m_new)
            l_ref[pl.ds(g, 1), :] = alpha * l_ref[pl.ds(g, 1), :] + jnp.sum(e, axis=0, keepdims=True)
            acc_ref[g] = alpha * acc_ref[g] + _dot(vt, e.astype(BF16))
            m_ref[pl.ds(g, 1), :] = m_new

    @pl.when(kt == nkt - 1)
    def _():
        if windowed:
            gates = _sigmoid(g_ref[...])
        for g in range(NSA_GROUP):
            o = acc_ref[g] / jnp.maximum(l_ref[pl.ds(g, 1), :], 1e-30)
            if windowed:
                o = (gates[g:g + 1, :] * oc_ref[g] + gates[NSA_GROUP + g:NSA_GROUP + g + 1, :] * os_ref[g]
                     + gates[2 * NSA_GROUP + g:2 * NSA_GROUP + g + 1, :] * o)
                o_ref[:, g * HEAD_DIM:(g + 1) * HEAD_DIM] = o.T.astype(o_ref.dtype)
            else:
                o_ref[g] = o


def flash_prompt(qn, kn, vt, slopes, b, l, sel=None, gated=None):
    windowed = sel is None
    tq = _tile(l, 512)
    tk = tq
    nq = l // tq
    nkt = (min(WINDOW, l) // tk + 1) if windowed else l // tk
    if windowed:
        kmap = lambda qi, kt: jnp.maximum(qi - (nkt - 1) + kt, 0)
    else:
        kmap = lambda qi, kt: jnp.minimum(kt, (qi * tq + tq - 1) // tk)
    in_specs = [
        pl.BlockSpec(memory_space=pltpu.SMEM),
        pl.BlockSpec((tq, GQ_W), lambda bi, h, qi, kt: (bi * nq + qi, h)),
        pl.BlockSpec((tk, HEAD_DIM), lambda bi, h, qi, kt: (bi * nq + kmap(qi, kt), h)),
        pl.BlockSpec((None, None, HEAD_DIM, tk), lambda bi, h, qi, kt: (bi, h, 0, kmap(qi, kt))),
    ]
    args = [slopes, qn, kn, vt]
    tspec = pl.BlockSpec((None, NSA_GROUP, HEAD_DIM, tq), lambda bi, h, qi, kt: (bi, h, 0, qi))
    if windowed:
        gates, o_cmp, o_slc = gated
        ngr = 3 * NSA_GROUP
        gk = gates[:, :3 * NSA_HEADS].reshape(b, l, 3, NSA_KV_HEADS, NSA_GROUP).transpose(0, 3, 2, 4, 1)
        gk = jnp.pad(gk.reshape(b, NSA_KV_HEADS, ngr, l), ((0, 0), (0, 0), (0, -ngr % SUBLANE), (0, 0)))
        in_specs += [pl.BlockSpec((None, None, gk.shape[2], tq), lambda bi, h, qi, kt: (bi, h, 0, qi)), tspec, tspec]
        args += [gk, o_cmp, o_slc]
        out_spec = pl.BlockSpec((tq, GQ_W), lambda bi, h, qi, kt: (bi * nq + qi, h))
        out_shape = jax.ShapeDtypeStruct((b * l, NSA_HEADS * HEAD_DIM), BF16)
    else:
        in_specs.append(pl.BlockSpec((None, None, tk // SLC_BLOCK, tq),
                                     lambda bi, h, qi, kt: (bi, h, kmap(qi, kt), qi)))
        args.append(sel)
        out_spec = tspec
        out_shape = jax.ShapeDtypeStruct((b, NSA_HEADS, HEAD_DIM, l), F32)
    return pl.pallas_call(
        functools.partial(_flash_kernel, tq=tq, tk=tk, nkt=nkt, windowed=windowed),
        grid=(b, NSA_KV_HEADS, nq, nkt),
        in_specs=in_specs,
        out_specs=out_spec,
        out_shape=out_shape,
        scratch_shapes=[pltpu.VMEM((SUBLANE, tq), F32), pltpu.VMEM((SUBLANE, tq), F32),
                        pltpu.VMEM((NSA_GROUP, HEAD_DIM, tq), F32)],
        compiler_params=_cparams(("parallel", "parallel", "parallel", "arbitrary")),
        name="window_attention" if windowed else "selected_attention",
    )(*args)


def _group_norm_gate(o, rg, gn):
    mu = jnp.mean(o, axis=-1, keepdims=True)
    var = jnp.mean(jnp.square(o - mu), axis=-1, keepdims=True)
    return _silu(rg) * ((o - mu) * lax.rsqrt(var + EPS) * gn)


def _retention_kernel(q_ref, k_ref, v_ref, rg_ref, intra_ref, qd_ref, kd_ref, cd_ref, gn_ref,
                      o_ref, s_ref, st_ref, *, tl, hb):
    li = pl.program_id(2)
    nl = pl.num_programs(2)

    @pl.when(li == 0)
    def _():
        st_ref[...] = jnp.zeros(st_ref.shape, F32)

    def chunk(c, carry):
        rows = pl.ds(pl.multiple_of(c * RET_CHUNK, RET_CHUNK), RET_CHUNK)
        for h in range(hb):
            cols = slice(h * HEAD_DIM, (h + 1) * HEAD_DIM)
            qc = q_ref[rows, cols]
            kc = k_ref[rows, cols] * (HEAD_DIM ** -0.5)
            vc = v_ref[rows, cols].astype(BF16)
            state = st_ref[h]
            att = _dot_nt(qc.astype(BF16), kc.astype(BF16)) * intra_ref[h]
            o = _dot(att.astype(BF16), vc) + _dot((qc * qd_ref[h]).astype(BF16), state.astype(BF16))
            st_ref[h] = state * cd_ref[h] + _dot((kc * kd_ref[h]).T.astype(BF16), vc)
            o_ref[rows, cols] = _group_norm_gate(o, rg_ref[rows, cols], gn_ref[:, cols]).astype(o_ref.dtype)
        return carry

    lax.fori_loop(0, tl // RET_CHUNK, chunk, 0)

    @pl.when(li == nl - 1)
    def _():
        s_ref[...] = st_ref[...]


def _retention_tables(chunk):
    h = jnp.arange(RET_HEADS, dtype=F32)
    log_g = jnp.log1p(-jnp.exp2(-5.0 - h))
    i = jnp.arange(chunk, dtype=F32)
    diff = i[:, None] - i[None, :]
    intra = jnp.where(diff >= 0, jnp.exp(jnp.maximum(diff, 0.0)[None] * log_g[:, None, None]), 0.0)
    q_dec = jnp.exp((i[None, :] + 1.0) * log_g[:, None])
    k_dec = jnp.exp((chunk - 1.0 - i)[None, :] * log_g[:, None])
    c_dec = jnp.exp(chunk * log_g)
    return intra, q_dec, k_dec, c_dec


def retention_prompt(r, gn, b, l):
    intra, q_dec, k_dec, c_dec = _retention_tables(RET_CHUNK)
    bc = lambda t: jnp.broadcast_to(t[:, :, None], (RET_HEADS, RET_CHUNK, HEAD_DIM))
    qd, kd = bc(q_dec), bc(k_dec)
    cd = jnp.broadcast_to(c_dec[:, None, None], (RET_HEADS, HEAD_DIM, HEAD_DIM))
    hb = 8
    ng = RET_HEADS // hb
    tl = _tile(l, 1024)
    nl = l // tl
    w = hb * HEAD_DIM
    rspec = lambda part: pl.BlockSpec((tl, w), lambda bi, g, li: (bi * nl + li, part * ng + g))
    tspec = pl.BlockSpec((hb, RET_CHUNK, HEAD_DIM), lambda bi, g, li: (g, 0, 0))
    return pl.pallas_call(
        functools.partial(_retention_kernel, tl=tl, hb=hb),
        grid=(b, ng, nl),
        in_specs=[rspec(0), rspec(1), rspec(2), rspec(3), tspec, tspec, tspec, tspec,
                  pl.BlockSpec((1, w), lambda bi, g, li: (0, g))],
        out_specs=[pl.BlockSpec((tl, w), lambda bi, g, li: (bi * nl + li, g)),
                   pl.BlockSpec((None, hb, HEAD_DIM, HEAD_DIM), lambda bi, g, li: (bi, g, 0, 0))],
        out_shape=[jax.ShapeDtypeStruct((b * l, RET_HEADS * HEAD_DIM), BF16),
                   jax.ShapeDtypeStruct((b, RET_HEADS, HEAD_DIM, HEAD_DIM), F32)],
        scratch_shapes=[pltpu.VMEM((hb, HEAD_DIM, HEAD_DIM), F32)],
        compiler_params=_cparams(("parallel", "parallel", "arbitrary")),
        name="retention_prompt",
    )(r, r, r, r, intra, qd, kd, cd, gn.reshape(1, -1))


def _group_rows(kvh):
    row = lax.broadcasted_iota(jnp.int32, (NSA_HEADS, 1), 0)
    return (row >= kvh * NSA_GROUP) & (row < (kvh + 1) * NSA_GROUP)


def _sample_cmp_kernel(pt_ref, *refs, pp, npg, pos0):
    pages = refs[:pp]
    (q_ref, slope_ref, pea_ref, peb_ref, wa_ref, wb_ref, b1_ref, w2_ref, b2_ref, kn_ref, m_ref,
     o_ref, sc_ref, xa_ref, xb_ref, f_ref, s_ref, c_ref) = refs[pp:]
    jp = pl.program_id(1)
    cpp = PAGE_ROWS // CMP_STRIDE
    nck = npg * cpp
    ng = 2 * NSA_KV_HEADS
    rows = pp * cpp * ng
    nrow = nck * ng

    def is_k(shape):
        return (lax.broadcasted_iota(jnp.int32, shape, 0) & NSA_KV_HEADS) == 0

    @pl.when(jp == 0)
    def _():
        s_ref[pl.ds(nrow, ng), :] = jnp.zeros((ng, CMP_HIDDEN), F32)

    for t in range(pp):
        for i in range(cpp):
            for c in range(CMP_STRIDE):
                dst = (pl.ds((t * cpp + i) * ng, ng), pl.ds(c * HEAD_DIM, HEAD_DIM))
                row = pages[t][i * CMP_STRIDE + c]
                xa_ref[dst] = row + pea_ref[:, c * HEAD_DIM:(c + 1) * HEAD_DIM]
                xb_ref[dst] = row + peb_ref[:, c * HEAD_DIM:(c + 1) * HEAD_DIM]
    fa = _dot(xa_ref[...].astype(BF16), wa_ref[...])
    sb = _dot(xb_ref[...].astype(BF16), wb_ref[...])
    kmask = is_k((rows, CMP_HIDDEN))
    dst = pl.ds(pl.multiple_of(jp * rows, rows), rows)
    f_ref[dst, :] = jnp.where(kmask, fa[:, :CMP_HIDDEN], fa[:, CMP_HIDDEN:])
    s_ref[dst, :] = jnp.where(kmask, sb[:, :CMP_HIDDEN], sb[:, CMP_HIDDEN:])

    @pl.when(jp == pl.num_programs(1) - 1)
    def _():
        eb = min(nrow, 1024)

        def cblock(r, carry):
            lo = pl.multiple_of(r * eb, eb)
            b1 = jnp.where(is_k((eb, CMP_HIDDEN)), b1_ref[0:1, :], b1_ref[1:2, :])
            hid = _gelu_tanh(f_ref[pl.ds(lo, eb), :] + s_ref[pl.ds(lo + ng, eb), :] + b1)
            c2 = _dot(hid.astype(BF16), w2_ref[...])
            c_ref[pl.ds(lo, eb), :] = jnp.where(is_k((eb, HEAD_DIM)), c2[:, :HEAD_DIM] + b2_ref[0:1, :],
                                                c2[:, HEAD_DIM:] + b2_ref[1:2, :])
            return carry

        lax.fori_loop(0, nrow // eb, cblock, 0)

        q = q_ref[...]
        n_io = lax.broadcasted_iota(jnp.int32, (NSA_HEADS, nck), 1)
        dist = pos0 - (n_io * CMP_STRIDE + (CMP_BLOCK - 1))
        mask = dist >= 0
        bias = slope_ref[:, :nck] * dist.astype(F32)
        o_acc = jnp.zeros((NSA_HEADS, HEAD_DIM), F32)
        imp_rows = []
        for kvh in range(NSA_KV_HEADS):
            kc = _row_rms(c_ref[pl.ds(kvh, nck, stride=ng), :], kn_ref[...]).astype(BF16)
            vc = c_ref[pl.ds(NSA_KV_HEADS + kvh, nck, stride=ng), :].astype(BF16)
            s = jnp.where(mask, _dot_nt(q, kc) - bias, -jnp.inf)
            m = jnp.max(s, axis=1, keepdims=True)
            m = jnp.where(m == -jnp.inf, 0.0, m)
            e = jnp.where(mask, jnp.exp(s - m), 0.0)
            p = e / jnp.maximum(jnp.sum(e, axis=1, keepdims=True), 1e-30)
            grp = _group_rows(kvh)
            o_acc = o_acc + jnp.where(grp, _dot(p.astype(BF16), vc), 0.0)
            imp_rows.append(jnp.sum(jnp.where(grp, p, 0.0), axis=0, keepdims=True))
        o_ref[...] = o_acc
        imp = jnp.concatenate(imp_rows + [jnp.zeros((SUBLANE - NSA_KV_HEADS, nck), F32)], axis=0)
        sc_ref[...] = jnp.dot(imp, m_ref[...], preferred_element_type=F32, precision=lax.Precision.HIGHEST)


def _sample_topk_kernel(sc_ref, idx_ref, *, pos0):
    score = sc_ref[...]
    rows, nsp = score.shape
    j = lax.broadcasted_iota(jnp.int32, (rows, nsp), 1)
    cur = pos0 // SLC_BLOCK
    forced = (j == 0) | (j == cur) | (j == cur - 1)
    score = jnp.where(j <= cur, jnp.where(forced, BIG, score), -BIG)
    lane = lax.broadcasted_iota(jnp.int32, (rows, LANE), 1)
    out = jnp.zeros((rows, LANE), jnp.int32)
    for t in range(SLC_TOPK):
        mx = jnp.max(score, axis=1, keepdims=True)
        idx = jnp.min(jnp.where(score == mx, j, nsp), axis=1, keepdims=True)
        out = jnp.where(lane == t, idx, out)
        score = jnp.where(j == idx, -jnp.inf, score)
    idx_ref[...] = out


def sample_topk(score, pos0):
    rows = score.shape[0]
    return pl.pallas_call(
        functools.partial(_sample_topk_kernel, pos0=pos0),
        out_shape=jax.ShapeDtypeStruct((rows, LANE), jnp.int32),
        name="sample_topk",
    )(score)


PAGE_ROWS = 128


def sample_cmp_select(cache4, page_table, qn3, slopes_b, pe, w1, b1, w2, b2, kn, pos0):
    db, npg = page_table.shape
    pp = _tile(npg, 8)
    cpp = PAGE_ROWS // CMP_STRIDE
    nck = npg * cpp
    ng = 2 * NSA_KV_HEADS
    n_slc = (pos0 + 1 + SLC_BLOCK - 1) // SLC_BLOCK
    nsp = -(-n_slc // LANE) * LANE
    m = np.zeros((nck, nsp), np.float32)
    m[:, :n_slc] = _overlap_matrix(nck, n_slc)
    kdim = CMP_STRIDE * HEAD_DIM
    pe_rows = lambda a: jnp.repeat(a.reshape(2, 1, kdim), NSA_KV_HEADS, axis=1).reshape(ng, kdim)
    pea, peb = pe_rows(pe[:, :CMP_STRIDE]), pe_rows(pe[:, CMP_STRIDE:])
    side = lambda a: jnp.concatenate([a[0], a[1]], axis=-1)
    wa = side(w1[:, :CMP_STRIDE].reshape(2, kdim, CMP_HIDDEN))
    wb = side(w1[:, CMP_STRIDE:].reshape(2, kdim, CMP_HIDDEN))
    const = lambda shape: pl.BlockSpec(shape, lambda bi, jp, pt: (0,) * len(shape))
    page_specs = [
        pl.BlockSpec((None, PAGE_ROWS, ng, HEAD_DIM),
                     functools.partial(lambda bi, jp, pt, t: (pt[bi, jp * pp + t], 0, 0, 0), t=t))
        for t in range(pp)]
    grid_spec = pltpu.PrefetchScalarGridSpec(
        num_scalar_prefetch=1,
        grid=(db, npg // pp),
        in_specs=page_specs + [
            pl.BlockSpec((None, NSA_HEADS, HEAD_DIM), lambda bi, jp, pt: (bi, 0, 0)),
            const((NSA_HEADS, slopes_b.shape[1])),
            const((ng, kdim)), const((ng, kdim)),
            const((kdim, 2 * CMP_HIDDEN)), const((kdim, 2 * CMP_HIDDEN)),
            const((2, CMP_HIDDEN)), const((CMP_HIDDEN, 2 * HEAD_DIM)), const((2, HEAD_DIM)),
            const((1, HEAD_DIM)), const((nck, nsp)),
        ],
        out_specs=[pl.BlockSpec((None, NSA_HEADS, HEAD_DIM), lambda bi, jp, pt: (bi, 0, 0)),
                   pl.BlockSpec((None, SUBLANE, nsp), lambda bi, jp, pt: (bi, 0, 0))],
        scratch_shapes=[
            pltpu.VMEM((pp * cpp * ng, kdim), F32),
            pltpu.VMEM((pp * cpp * ng, kdim), F32),
            pltpu.VMEM((nck * ng, CMP_HIDDEN), F32),
            pltpu.VMEM((nck * ng + ng, CMP_HIDDEN), F32),
            pltpu.VMEM((nck * ng, HEAD_DIM), F32),
        ],
    )
    return pl.pallas_call(
        functools.partial(_sample_cmp_kernel, pp=pp, npg=npg, pos0=pos0),
        grid_spec=grid_spec,
        out_shape=[jax.ShapeDtypeStruct((db, NSA_HEADS, HEAD_DIM), F32),
                   jax.ShapeDtypeStruct((db, SUBLANE, nsp), F32)],
        compiler_params=_cparams(("parallel", "arbitrary")),
        name="sample_cmp_select",
    )(page_table, *([cache4] * pp), qn3, slopes_b, pea, peb, wa, wb, b1, side(w2), b2, kn.reshape(1, -1),
      jnp.asarray(m))


def _softmax_with_new_key(s, s_new):
    m = jnp.maximum(jnp.max(s, axis=1, keepdims=True), s_new)
    e = jnp.exp(s - m)
    e_new = jnp.exp(s_new - m)
    return e, e_new, jnp.maximum(jnp.sum(e, axis=1, keepdims=True) + e_new, 1e-30)


def _bf16_round(x):
    return x.astype(BF16).astype(F32)


def _sample_slc_kernel(row_ref, cache_ref, q_ref, kpos_ref, slope_ref, new_ref, kn_ref, o_ref, kbuf, vbuf, sem, *, pos0):
    bi = pl.program_id(0)
    nb = pl.num_programs(0)
    nsel = SLC_TOPK * SLC_BLOCK
    kcol = 2 * NSA_KV_HEADS
    vcol = 3 * NSA_KV_HEADS

    def gather(b, slot, start):
        for kvh in range(NSA_KV_HEADS):
            for t in range(SLC_TOPK):
                src = pl.ds(row_ref[(b * NSA_KV_HEADS + kvh) * SLC_TOPK + t], SLC_BLOCK)
                dst = pl.ds(t * SLC_BLOCK, SLC_BLOCK)
                for col, buf in ((kcol, kbuf), (vcol, vbuf)):
                    cp = pltpu.make_async_copy(cache_ref.at[src, col + kvh, :], buf.at[slot, kvh, dst, :], sem.at[slot])
                    if start:
                        cp.start()
                    else:
                        cp.wait()

    slot = bi % 2

    @pl.when(bi == 0)
    def _():
        gather(bi, 0, True)

    @pl.when(bi + 1 < nb)
    def _():
        gather(bi + 1, 1 - slot, True)

    gather(bi, slot, False)

    q = q_ref[...]
    o_acc = jnp.zeros((NSA_HEADS, HEAD_DIM), F32)
    for kvh in range(NSA_KV_HEADS):
        kn = _row_rms(kbuf[slot, kvh], kn_ref[...]).astype(BF16)
        v = vbuf[slot, kvh]
        dist = pos0 - kpos_ref[pl.ds(kvh, 1), :]
        s = _dot_nt(q, kn) - slope_ref[:, :nsel] * dist.astype(F32)
        s = jnp.where(dist >= 0, s, NEG)
        knew = _row_rms(new_ref[pl.ds(kvh, 1), :], kn_ref[...])
        s_new = jnp.sum(q.astype(F32) * _bf16_round(knew), axis=1, keepdims=True)
        e, e_new, l = _softmax_with_new_key(s, s_new)
        vnew = new_ref[pl.ds(NSA_KV_HEADS + kvh, 1), :]
        o = _dot(e.astype(BF16), v.astype(BF16)) + _bf16_round(e_new) * _bf16_round(vnew)
        o_acc = o_acc + jnp.where(_group_rows(kvh), o / l, 0.0)
    o_ref[...] = o_acc


def sample_slc_attention(cache3, row0, qn3, kpos, slopes_b, new_kv, kn, pos0):
    db = qn3.shape[0]
    nsel = SLC_TOPK * SLC_BLOCK
    ng = 2 * NSA_KV_HEADS
    grid_spec = pltpu.PrefetchScalarGridSpec(
        num_scalar_prefetch=1,
        grid=(db,),
        in_specs=[
            pl.BlockSpec(memory_space=pl.ANY),
            pl.BlockSpec((None, NSA_HEADS, HEAD_DIM), lambda bi, r: (bi, 0, 0)),
            pl.BlockSpec((None, NSA_KV_HEADS, nsel), lambda bi, r: (bi, 0, 0)),
            pl.BlockSpec((NSA_HEADS, slopes_b.shape[1]), lambda bi, r: (0, 0)),
            pl.BlockSpec((None, ng, HEAD_DIM), lambda bi, r: (bi, 0, 0)),
            pl.BlockSpec((1, HEAD_DIM), lambda bi, r: (0, 0)),
        ],
        out_specs=pl.BlockSpec((None, NSA_HEADS, HEAD_DIM), lambda bi, r: (bi, 0, 0)),
        scratch_shapes=[pltpu.VMEM((2, NSA_KV_HEADS, nsel, HEAD_DIM), F32),
                        pltpu.VMEM((2, NSA_KV_HEADS, nsel, HEAD_DIM), F32),
                        pltpu.SemaphoreType.DMA((2,))],
    )
    return pl.pallas_call(
        functools.partial(_sample_slc_kernel, pos0=pos0),
        grid_spec=grid_spec,
        out_shape=jax.ShapeDtypeStruct((db, NSA_HEADS, HEAD_DIM), F32),
        compiler_params=_cparams(("arbitrary",)),
        name="sample_slc_attention",
    )(row0, cache3, qn3, kpos, slopes_b, new_kv, kn.reshape(1, -1))


def _sample_win_kernel(sw_ref, q_ref, slope_ref, new_ref, kn_ref, g_ref, oc_ref, os_ref, o_ref, nw_ref, *, wb):
    q = q_ref[...]
    row = lax.broadcasted_iota(jnp.int32, (NSA_HEADS, wb), 1)
    dist = wb - row
    valid = dist < WINDOW
    bias = slope_ref[:, :wb] * dist.astype(F32)
    o_acc = jnp.zeros((NSA_HEADS, HEAD_DIM), F32)
    for kvh in range(NSA_KV_HEADS):
        k = sw_ref[:, kvh, :]
        v = sw_ref[:, NSA_KV_HEADS + kvh, :]
        kn = _row_rms(k, kn_ref[...]).astype(BF16)
        s = jnp.where(valid, _dot_nt(q, kn) - bias, NEG)
        knew = _row_rms(new_ref[pl.ds(kvh, 1), :], kn_ref[...])
        s_new = jnp.sum(q.astype(F32) * _bf16_round(knew), axis=1, keepdims=True)
        e, e_new, l = _softmax_with_new_key(s, s_new)
        vnew = new_ref[pl.ds(NSA_KV_HEADS + kvh, 1), :]
        o = _dot(e.astype(BF16), v.astype(BF16)) + _bf16_round(e_new) * _bf16_round(vnew)
        o_acc = o_acc + jnp.where(_group_rows(kvh), o / l, 0.0)
    g = _sigmoid(g_ref[...])
    o_ref[...] = (g[0] * oc_ref[...] + g[1] * os_ref[...] + g[2] * o_acc).astype(o_ref.dtype)
    nw_ref[pl.ds(0, wb - 1)] = sw_ref[pl.ds(1, wb - 1)]
    nw_ref[wb - 1] = new_ref[...]


def sample_win_attention(sw, qn3, slopes_b, win_new, kn, gates_b, o_cmp, o_slc):
    db, wb, ng, _ = sw.shape
    hspec = pl.BlockSpec((None, NSA_HEADS, HEAD_DIM), lambda bi: (bi, 0, 0))
    wspec = pl.BlockSpec((None, wb, ng, HEAD_DIM), lambda bi: (bi, 0, 0, 0))
    return pl.pallas_call(
        functools.partial(_sample_win_kernel, wb=wb),
        grid=(db,),
        in_specs=[
            wspec,
            hspec,
            pl.BlockSpec((NSA_HEADS, slopes_b.shape[1]), lambda bi: (0, 0)),
            pl.BlockSpec((None, ng, HEAD_DIM), lambda bi: (bi, 0, 0)),
            pl.BlockSpec((1, HEAD_DIM), lambda bi: (0, 0)),
            pl.BlockSpec((None, 3, NSA_HEADS, HEAD_DIM), lambda bi: (bi, 0, 0, 0)),
            hspec, hspec,
        ],
        out_specs=[hspec, wspec],
        out_shape=[jax.ShapeDtypeStruct((db, NSA_HEADS, HEAD_DIM), BF16),
                   jax.ShapeDtypeStruct((db, wb, ng, HEAD_DIM), F32)],
        compiler_params=_cparams(("parallel",)),
        name="sample_win_attention",
    )(sw, qn3, slopes_b, win_new, kn.reshape(1, -1), gates_b, o_cmp, o_slc)


def _sample_ret_kernel(st_ref, r_ref, gam_ref, gn_ref, o_ref, s_ref):
    q = r_ref[0]
    k = r_ref[1] * (HEAD_DIM ** -0.5)
    v = r_ref[2]
    rg = r_ref[3]
    gam = gam_ref[...]
    qb, kb, vb = _bf16_round(q), _bf16_round(k), _bf16_round(v)
    att = jnp.sum(qb * kb, axis=1, keepdims=True)
    kt = kb.T
    qd = (q * gam).astype(BF16)
    rows = []
    for h in range(RET_HEADS):
        state = st_ref[h]
        rows.append(_dot(qd[h:h + 1, :], state.astype(BF16)))
        s_ref[h] = state * gam[h:h + 1, :] + kt[:, h:h + 1] * vb[h:h + 1, :]
    o = _bf16_round(att) * vb + jnp.concatenate(rows, axis=0)
    o_ref[...] = _group_norm_gate(o, rg, gn_ref[...]).astype(o_ref.dtype)


def sample_retention(state, r4, gn):
    db = state.shape[0]
    _, q_dec, _, _ = _retention_tables(1)
    gam = jnp.broadcast_to(q_dec, (RET_HEADS, HEAD_DIM))
    return pl.pallas_call(
        _sample_ret_kernel,
        grid=(db,),
        in_specs=[
            pl.BlockSpec((None, RET_HEADS, HEAD_DIM, HEAD_DIM), lambda bi: (bi, 0, 0, 0)),
            pl.BlockSpec((None, 4, RET_HEADS, HEAD_DIM), lambda bi: (bi, 0, 0, 0)),
            pl.BlockSpec((RET_HEADS, HEAD_DIM), lambda bi: (0, 0)),
            pl.BlockSpec((RET_HEADS, HEAD_DIM), lambda bi: (0, 0)),
        ],
        out_specs=[pl.BlockSpec((None, RET_HEADS, HEAD_DIM), lambda bi: (bi, 0, 0)),
                   pl.BlockSpec((None, RET_HEADS, HEAD_DIM, HEAD_DIM), lambda bi: (bi, 0, 0, 0))],
        out_shape=[jax.ShapeDtypeStruct((db, RET_HEADS, HEAD_DIM), BF16),
                   jax.ShapeDtypeStruct((db, RET_HEADS, HEAD_DIM, HEAD_DIM), F32)],
        compiler_params=_cparams(("parallel",)),
        name="sample_retention",
    )(state, r4, gam, gn.reshape(RET_HEADS, HEAD_DIM))


def _alibi_slopes():
    i = np.arange(NSA_HEADS, dtype=np.float32)
    return np.exp2(-8.0 * (i + 1.0) / NSA_HEADS).astype(np.float32)


def _in_proj(n, w_t):
    nq = NSA_HEADS * HEAD_DIM
    nkv = 4 * NSA_KV_HEADS * HEAD_DIM
    nwin = 2 * NSA_KV_HEADS * HEAD_DIM
    ng = 3 * NSA_HEADS
    nr = 4 * RET_HEADS * HEAD_DIM
    assert w_t.shape[0] == nq + nkv + nwin + ng + nr and ng <= LANE
    o_g = nq + nkv + nwin
    return [matmul_nt(n, w_t, 0, nq), matmul_nt(n, w_t, nq, nkv), matmul_nt(n, w_t, nq + nkv, nwin),
            matmul_nt(n, w_t, o_g, LANE), matmul_nt(n, w_t, o_g + ng, nr)]


def kernel(x_prompt, x_sample, cache_nsa_kv, state_win_kv, state_ret, page_table, norm_g, ffn_w1, ffn_w3, ffn_w2,
           w_in, nsa_q_norm, nsa_k_norm, cmp_pe, cmp_w1, cmp_b1, cmp_w2, cmp_b2, ret_gn_g, w_out):
    assert cache_nsa_kv.shape[0] == 1 and x_sample.shape[1] == 1
    b, l, d = x_prompt.shape
    db = x_sample.shape[0]
    npg = page_table.shape[1]
    n_pool = cache_nsa_kv.shape[1]
    assert cache_nsa_kv.shape[2] == PAGE_ROWS
    pos0 = npg * PAGE_ROWS
    half = NSA_KV_HEADS * HEAD_DIM

    w_t = jnp.swapaxes(w_in, 1, 2)[0]
    ng = norm_g[0]
    qg, kg = nsa_q_norm[0], nsa_k_norm[0]
    pe = cmp_pe[0]
    cw1 = cmp_w1[0].astype(BF16)
    cb1 = cmp_b1[0]
    cw2 = cmp_w2[0].astype(BF16)
    cb2 = cmp_b2[0]
    gn = ret_gn_g[0]
    slopes = jnp.asarray(_alibi_slopes())
    q_scale = HEAD_DIM ** -0.5

    hp, hs = ffn_half_step(x_prompt.reshape(b * l, d), x_sample.reshape(db, d), ng[0], ffn_w1, ffn_w3, ffn_w2, 0)

    q, kv, win, gates, r = _in_proj(rmsnorm_bf16(hp, ng[1]), w_t)
    qn = headnorm_bf16(q, 0, NSA_HEADS, qg, q_scale)
    kc = compress_prompt(kv, 0, b, l, pe[0], cw1[0], cb1[0], cw2[0], cb2[0], kg[0])
    vct = compress_prompt(kv, 1, b, l, pe[1], cw1[1], cb1[1], cw2[1], cb2[1], kg[0])
    o_cmp, sel = cmp_attention_prompt(qn, kc, vct, slopes, b, l)
    k_slc = headnorm_bf16(kv, 2 * half, NSA_KV_HEADS, kg[1])
    v_slc = values_transposed(kv, 3 * half, b, l)
    o_slc = flash_prompt(qn, k_slc, v_slc, slopes, b, l, sel=sel)
    k_win = headnorm_bf16(win, 0, NSA_KV_HEADS, kg[2])
    v_win = values_transposed(win, half, b, l)
    o_nsa = flash_prompt(qn, k_win, v_win, slopes, b, l, gated=(gates, o_cmp, o_slc))
    o_ret, ret_p = retention_prompt(r, gn, b, l)
    hp = out_proj(o_nsa, o_ret, w_out, hp)
    kv_prompt = kv.reshape(1, b, l, 4, NSA_KV_HEADS, HEAD_DIM)
    wl = min(WINDOW, l)
    win_prompt = win.reshape(b, l, 2, NSA_KV_HEADS, HEAD_DIM)[:, l - wl:][None]

    q, kv, win, gates, r = _in_proj(rmsnorm_bf16(hs, ng[1]), w_t)
    qn3 = headnorm_bf16(q, 0, NSA_HEADS, qg, q_scale).reshape(db, NSA_HEADS, HEAD_DIM)
    slopes_b = jnp.broadcast_to(slopes[:, None], (NSA_HEADS, max(npg * (PAGE_ROWS // CMP_STRIDE),
                                                                 SLC_TOPK * SLC_BLOCK, state_win_kv.shape[2])))
    o_cmp, score = sample_cmp_select(cache_nsa_kv.reshape(n_pool, PAGE_ROWS, 4 * NSA_KV_HEADS, HEAD_DIM), page_table,
                                     qn3, slopes_b, pe, cw1, cb1, cw2, cb2, kg[0], pos0)
    idx = sample_topk(score.reshape(db * SUBLANE, -1), pos0).reshape(db, SUBLANE, LANE)
    idx = idx[:, :NSA_KV_HEADS, :SLC_TOPK]
    n_past = pos0 // SLC_BLOCK
    in_cache = idx < n_past
    pidx = jnp.minimum(idx, n_past - 1)
    per_page = PAGE_ROWS // SLC_BLOCK
    page = jnp.take_along_axis(page_table, (pidx // per_page).reshape(db, -1), axis=1).reshape(pidx.shape)
    row0 = (page * PAGE_ROWS + (pidx % per_page) * SLC_BLOCK).astype(jnp.int32).reshape(-1)
    offs = jnp.arange(SLC_BLOCK, dtype=jnp.int32)
    kpos = jnp.where(in_cache[..., None], idx[..., None] * SLC_BLOCK + offs, pos0 + 1).reshape(db, NSA_KV_HEADS, -1)
    o_slc = sample_slc_attention(cache_nsa_kv.reshape(n_pool * PAGE_ROWS, 4 * NSA_KV_HEADS, HEAD_DIM), row0, qn3, kpos,
                                 slopes_b, kv[:, 2 * half:].reshape(db, 2 * NSA_KV_HEADS, HEAD_DIM), kg[1], pos0)
    gates_b = jnp.broadcast_to(gates[:, :3 * NSA_HEADS].reshape(db, 3, NSA_HEADS, 1), (db, 3, NSA_HEADS, HEAD_DIM))
    wb = state_win_kv.shape[2]
    o_nsa, win_s = sample_win_attention(state_win_kv.reshape(db, wb, 2 * NSA_KV_HEADS, HEAD_DIM), qn3, slopes_b,
                                        win.reshape(db, 2 * NSA_KV_HEADS, HEAD_DIM), kg[2], gates_b, o_cmp, o_slc)
    o_ret, ret_s = sample_retention(state_ret[0], r.reshape(db, 4, RET_HEADS, HEAD_DIM), gn)
    hs = out_proj(o_nsa.reshape(db, -1), o_ret.reshape(db, -1), w_out, hs)
    y_prompt, y_sample = ffn_half_step(hp, hs, ng[2], ffn_w1, ffn_w3, ffn_w2, 1)
    y_prompt = y_prompt.reshape(b, l, d)
    y_sample = y_sample.reshape(db, 1, d)
    kv_sample = kv.reshape(1, db, 1, 4, NSA_KV_HEADS, HEAD_DIM)
    win_sample = win_s.reshape(1, db, wb, 2, NSA_KV_HEADS, HEAD_DIM)

    return (y_prompt, y_sample, kv_prompt, kv_sample, win_prompt, win_sample, ret_p[None], ret_s[None])
```
